```python
import jax
import jax.numpy as jnp
from jax import lax
import numpy as np

D_MODEL = 1024
BATCH = 16
SEQ = 2048
DEPTH = 1

GLA_HEADS = 4
GLA_DK = 64
GLA_DV = 128
GLA_RANK = 16
GLA_TAU = 16.0
GLA_CHUNK = 64
NSA_HEADS = 4
NSA_KV_HEADS = 2
NSA_DH = 128
NSA_BRANCHES = 3
CMP_BLOCK = 32
CMP_STRIDE = 16
CMP_HIDDEN = 256
SEL_BLOCK = 64
SEL_TOPK = 16
SEL_QBLK = 16
WINDOW = 512
WIN_QBLK = 128
D_FF = 4 * D_MODEL
N_MOD = 6
ROPE_THETA = 10000.0
EPS = 1e-6
NEG = -1e30
BIG = 1e30

GLA_WIDTH = GLA_HEADS * GLA_DV
NSA_WIDTH = NSA_HEADS * NSA_DH
NSA_KV_WIDTH = NSA_KV_HEADS * NSA_DH
IN_SPLITS = (GLA_HEADS * GLA_DK, GLA_HEADS * GLA_DK, GLA_WIDTH, GLA_WIDTH, GLA_RANK,
             NSA_WIDTH, NSA_KV_WIDTH, NSA_KV_WIDTH, NSA_KV_WIDTH, NSA_KV_WIDTH,
             NSA_KV_WIDTH, NSA_KV_WIDTH, NSA_HEADS * NSA_BRANCHES)
D_IN = sum(IN_SPLITS)

kernel_name = 'hybrid_gla_nsa_adaln_block'


def rms_norm(x, g):
    xf = x.astype(jnp.float32)
    y = xf * lax.rsqrt(jnp.mean(xf * xf, axis=-1, keepdims=True) + EPS)
    return (y * g.astype(jnp.float32)).astype(x.dtype)


def rope(x, pos):
    half = x.shape[-1] // 2
    inv = ROPE_THETA ** (-jnp.arange(half, dtype=jnp.float32) / half)
    ang = pos.astype(jnp.float32)[..., None] * inv
    cos = jnp.cos(ang)[:, :, None, :]
    sin = jnp.sin(ang)[:, :, None, :]
    xf = x.astype(jnp.float32)
    x1, x2 = xf[..., :half], xf[..., half:]
    return jnp.concatenate([x1 * cos - x2 * sin, x2 * cos + x1 * sin], axis=-1).astype(x.dtype)


def masked_softmax(s, mask):
    return jax.nn.softmax(jnp.where(mask, s.astype(jnp.float32), NEG), axis=-1)


def gla_chunked(q, k, v, log_a):
    B, S, H, dk = q.shape
    dv = v.shape[-1]
    C = GLA_CHUNK
    N = S // C

    def blocks(t):
        return t.astype(jnp.float32).reshape(B, N, C, H, t.shape[-1]).transpose(0, 3, 1, 2, 4)

    qf = blocks(q) * dk ** -0.5
    kf, vf, la = blocks(k), blocks(v), blocks(log_a)
    b = jnp.cumsum(la, axis=3)
    b_last = b[:, :, :, -1:, :]
    q_in = qf * jnp.exp(b)
    k_in = kf * jnp.exp(-b)
    causal = jnp.tril(jnp.ones((C, C), dtype=bool))
    att = jnp.where(causal, jnp.einsum('bhncd,bhnsd->bhncs', q_in, k_in), 0.0)
    o_intra = jnp.einsum('bhncs,bhnse->bhnce', att, vf)
    d_state = jnp.einsum('bhncd,bhnce->bhnde', kf * jnp.exp(b_last - b), vf)
    decay = jnp.exp(b_last[:, :, :, 0, :])

    def step(state, inp):
        dec, ds = inp
        return dec[..., None] * state + ds, state

    init = jnp.zeros((B, H, dk, dv), jnp.float32)
    _, s_prev = lax.scan(step, init, (jnp.moveaxis(decay, 2, 0), jnp.moveaxis(d_state, 2, 0)))
    s_prev = jnp.moveaxis(s_prev, 0, 2)
    o = o_intra + jnp.einsum('bhncd,bhnde->bhnce', q_in, s_prev)
    return o.transpose(0, 2, 3, 1, 4).reshape(B, S, H, dv)


def nsa(q, kc, vc, ks, vs, kw, vw, gates, pe_k, pe_v, ck_w1, ck_w2, cv_w1, cv_w2):
    B, S, H, Dh = q.shape
    Hk = NSA_KV_HEADS
    G = H // Hk
    dt = q.dtype
    scale = Dh ** -0.5
    t_idx = jnp.arange(S)
    qg = q.reshape(B, S, Hk, G, Dh)

    nc = (S - CMP_BLOCK) // CMP_STRIDE + 1
    win_idx = jnp.arange(nc)[:, None] * CMP_STRIDE + jnp.arange(CMP_BLOCK)[None, :]

    def compress(t, pe, w1, w2):
        blk = t[:, win_idx] + pe[None, None, :, None, :]
        blk = blk.transpose(0, 1, 3, 2, 4).reshape(B, nc, Hk, CMP_BLOCK * Dh)
        return jax.nn.gelu(blk @ w1) @ w2

    k_cmp = compress(kc, pe_k, ck_w1, ck_w2)
    v_cmp = compress(vc, pe_v, cv_w1, cv_w2)
    cmp_start = jnp.arange(nc) * CMP_STRIDE
    cmp_valid = (cmp_start + CMP_BLOCK - 1)[None, :] <= t_idx[:, None]
    s_c = jnp.einsum('bskgd,bnkd->bkgsn', qg, k_cmp).astype(jnp.float32) * scale
    p_c = masked_softmax(s_c, cmp_valid) * jnp.any(cmp_valid, axis=-1)[:, None]
    o_cmp = jnp.einsum('bkgsn,bnkd->bskgd', p_c.astype(dt), v_cmp)

    nsb = S // SEL_BLOCK
    sel_start = jnp.arange(nsb) * SEL_BLOCK
    lo = jnp.maximum(cmp_start[:, None], sel_start[None, :])
    hi = jnp.minimum(cmp_start[:, None] + CMP_BLOCK, sel_start[None, :] + SEL_BLOCK)
    overlap = jnp.clip(hi - lo, 0).astype(jnp.float32) / CMP_BLOCK
    imp = jnp.einsum('bkgsn,nj->bksj', p_c, overlap)
    j = jnp.arange(nsb)[None, :]
    blk_t = (t_idx // SEL_BLOCK)[:, None]
    forced = (j == 0) | (j == blk_t) | (j == blk_t - 1)
    causal_blk = sel_start[None, :] <= t_idx[:, None]
    imp = jnp.where(forced, BIG, jnp.where(causal_blk, imp, NEG))
    n_top = min(SEL_TOPK, nsb)
    _, sel_idx = lax.top_k(imp, n_top)

    ks_b = ks.reshape(B, nsb, SEL_BLOCK, Hk, Dh).transpose(0, 3, 1, 2, 4)
    vs_b = vs.reshape(B, nsb, SEL_BLOCK, Hk, Dh).transpose(0, 3, 1, 2, 4)
    gather = jax.vmap(jax.vmap(lambda kb, ix: kb[ix]))
    nq = S // SEL_QBLK
    q_ch = qg.reshape(B, nq, SEL_QBLK, Hk, G, Dh).transpose(1, 0, 3, 4, 2, 5)
    idx_ch = sel_idx.reshape(B, Hk, nq, SEL_QBLK, n_top).transpose(2, 0, 1, 3, 4)
    t_ch = t_idx.reshape(nq, SEL_QBLK)
    offs = jnp.arange(SEL_BLOCK)

    def sel_step(args):
        qc, ic, tc = args
        kg = gather(ks_b, ic)
        vg = gather(vs_b, ic)
        s = jnp.einsum('bkgqd,bkqnld->bkgqnl', qc, kg).astype(jnp.float32) * scale
        kpos = ic[..., None] * SEL_BLOCK + offs
        mask = (kpos <= tc[None, None, :, None, None]).reshape(B, Hk, 1, SEL_QBLK, n_top * SEL_BLOCK)
        p = masked_softmax(s.reshape(B, Hk, G, SEL_QBLK, n_top * SEL_BLOCK), mask)
        return jnp.einsum('bkgqm,bkqmd->bkgqd', p.astype(dt),
                          vg.reshape(B, Hk, SEL_QBLK, n_top * SEL_BLOCK, Dh))

    o_sel = lax.map(sel_step, (q_ch, idx_ch, t_ch))
    o_sel = o_sel.transpose(1, 0, 4, 2, 3, 5).reshape(B, S, Hk, G, Dh)

    nw = S // WIN_QBLK
    span = WINDOW + WIN_QBLK
    kw_p = jnp.pad(kw, ((0, 0), (WINDOW, 0), (0, 0), (0, 0)))
    vw_p = jnp.pad(vw, ((0, 0), (WINDOW, 0), (0, 0), (0, 0)))
    q_wb = qg.reshape(B, nw, WIN_QBLK, Hk, G, Dh).transpose(1, 0, 3, 4, 2, 5)

    def win_step(args):
        qc, i = args
        start = i * WIN_QBLK
        kb = lax.dynamic_slice_in_dim(kw_p, start, span, axis=1)
        vb = lax.dynamic_slice_in_dim(vw_p, start, span, axis=1)
        s = jnp.einsum('bkgqd,bskd->bkgqs', qc, kb).astype(jnp.float32) * scale
        tq = start + jnp.arange(WIN_QBLK)
        sk = start - WINDOW + jnp.arange(span)
        mask = (sk[None, :] <= tq[:, None]) & (sk[None, :] > tq[:, None] - WINDOW) & (sk[None, :] >= 0)
        p = masked_softmax(s, mask)
        return jnp.einsum('bkgqs,bskd->bkgqd', p.astype(dt), vb)

    o_win = lax.map(win_step, (q_wb, jnp.arange(nw)))
    o_win = o_win.transpose(1, 0, 4, 2, 3, 5).reshape(B, S, Hk, G, Dh)

    g = gates.reshape(B, S, Hk, G, NSA_BRANCHES)
    o = g[..., 0:1] * o_cmp + g[..., 1:2] * o_sel + g[..., 2:3] * o_win
    return o.reshape(B, S, H * Dh)


def hybrid_mixer(h, pos, w_in, gla_w_a2, gla_b_a, gla_norm_g, pe_k, pe_v,
                 ck_w1, ck_w2, cv_w1, cv_w2, nsa_norm_g, w_o):
    B, S, _ = h.shape
    proj = h @ w_in
    cuts = [int(i) for i in np.cumsum(IN_SPLITS)[:-1]]
    (g_q, g_k, g_v, g_g, g_r, n_q, n_kc, n_vc, n_ks, n_vs, n_kw, n_vw, n_gate) = jnp.split(proj, cuts, axis=-1)

    log_a = jax.nn.log_sigmoid((g_r @ gla_w_a2 + gla_b_a).astype(jnp.float32)) / GLA_TAU
    o_gla = gla_chunked(g_q.reshape(B, S, GLA_HEADS, GLA_DK), g_k.reshape(B, S, GLA_HEADS, GLA_DK),
                        g_v.reshape(B, S, GLA_HEADS, GLA_DV),
                        log_a.reshape(B, S, GLA_HEADS, GLA_DK)).astype(h.dtype)
    o_gla = rms_norm(o_gla, gla_norm_g.reshape(GLA_HEADS, GLA_DV)).reshape(B, S, GLA_WIDTH) * jax.nn.silu(g_g)

    q = rope(n_q.reshape(B, S, NSA_HEADS, NSA_DH), pos)
    kvh = lambda t: t.reshape(B, S, NSA_KV_HEADS, NSA_DH)
    o_nsa = nsa(q, rope(kvh(n_kc), pos), kvh(n_vc), rope(kvh(n_ks), pos), kvh(n_vs),
                rope(kvh(n_kw), pos), kvh(n_vw), jax.nn.sigmoid(n_gate),
                pe_k, pe_v, ck_w1, ck_w2, cv_w1, cv_w2)
    o_nsa = rms_norm(o_nsa, nsa_norm_g)

    return jnp.concatenate([o_gla, o_nsa], axis=-1) @ w_o


def setup_inputs(seed: int = 0) -> dict:
    key = jax.random.key(seed)
    ks = jax.random.split(key, 24)
    L = DEPTH
    f32 = jnp.float32

    def nrm(k, shape, scale):
        return jax.random.normal(k, shape, f32) * scale

    offsets = jax.random.randint(ks[2], (BATCH, 1), 0, 4096, dtype=jnp.int32)
    positions = offsets + jnp.arange(SEQ, dtype=jnp.int32)[None, :]
    return {
        'x': nrm(ks[0], (BATCH, SEQ, D_MODEL), 1.0),
        'c': nrm(ks[1], (BATCH, D_MODEL), 1.0),
        'positions': positions,
        'w_ada': nrm(ks[3], (L, D_MODEL, N_MOD * D_MODEL), D_MODEL ** -0.5),
        'b_ada': nrm(ks[4], (L, N_MOD * D_MODEL), 0.01),
        'norm1_g': 1.0 + nrm(ks[5], (L, D_MODEL), 0.01),
        'w_in': nrm(ks[6], (L, D_MODEL, D_IN), D_MODEL ** -0.5),
        'gla_w_a2': nrm(ks[7], (L, GLA_RANK, GLA_HEADS * GLA_DK), GLA_RANK ** -0.5),
        'gla_b_a': nrm(ks[8], (L, GLA_HEADS * GLA_DK), 0.01),
        'gla_norm_g': 1.0 + nrm(ks[9], (L, GLA_WIDTH), 0.01),
        'nsa_pe_k': nrm(ks[10], (L, CMP_BLOCK, NSA_DH), 0.02),
        'nsa_pe_v': nrm(ks[11], (L, CMP_BLOCK, NSA_DH), 0.02),
        'cmp_k_w1': nrm(ks[12], (L, CMP_BLOCK * NSA_DH, CMP_HIDDEN), (CMP_BLOCK * NSA_DH) ** -0.5),
        'cmp_k_w2': nrm(ks[13], (L, CMP_HIDDEN, NSA_DH), CMP_HIDDEN ** -0.5),
        'cmp_v_w1': nrm(ks[14], (L, CMP_BLOCK * NSA_DH, CMP_HIDDEN), (CMP_BLOCK * NSA_DH) ** -0.5),
        'cmp_v_w2': nrm(ks[15], (L, CMP_HIDDEN, NSA_DH), CMP_HIDDEN ** -0.5),
        'nsa_norm_g': 1.0 + nrm(ks[16], (L, NSA_WIDTH), 0.01),
        'w_o': nrm(ks[17], (L, GLA_WIDTH + NSA_WIDTH, D_MODEL), (GLA_WIDTH + NSA_WIDTH) ** -0.5),
        'norm2_g': 1.0 + nrm(ks[18], (L, D_MODEL), 0.01),
        'w_ff1': nrm(ks[19], (L, D_MODEL, D_FF), D_MODEL ** -0.5),
        'w_ff2': nrm(ks[20], (L, D_FF, D_MODEL), D_FF ** -0.5),
        'final_norm_g': 1.0 + nrm(ks[21], (D_MODEL,), 0.01),
    }


def reference(x, c, positions, w_ada, b_ada, norm1_g, w_in, gla_w_a2, gla_b_a, gla_norm_g,
              nsa_pe_k, nsa_pe_v, cmp_k_w1, cmp_k_w2, cmp_v_w1, cmp_v_w2, nsa_norm_g, w_o,
              norm2_g, w_ff1, w_ff2, final_norm_g):
    for l in range(DEPTH):
        mod = (jax.nn.silu(c) @ w_ada[l] + b_ada[l])[:, None, :]
        sh1, sc1, ga1, sh2, sc2, ga2 = jnp.split(mod, N_MOD, axis=-1)
        h = rms_norm(x, norm1_g[l]) * (1 + sc1) + sh1
        x = x + ga1 * hybrid_mixer(h, positions, w_in[l], gla_w_a2[l], gla_b_a[l], gla_norm_g[l],
                                   nsa_pe_k[l], nsa_pe_v[l], cmp_k_w1[l], cmp_k_w2[l],
                                   cmp_v_w1[l], cmp_v_w2[l], nsa_norm_g[l], w_o[l])
        h = rms_norm(x, norm2_g[l]) * (1 + sc2) + sh2
        x = x + ga2 * (jnp.square(jax.nn.relu(h @ w_ff1[l])) @ w_ff2[l])
    return rms_norm(x, final_norm_g)
```

```python
import functools

import numpy as np
import jax
import jax.numpy as jnp
from jax import lax
from jax.experimental import pallas as pl
from jax.experimental.pallas import tpu as pltpu

GLA_HEADS = 4
GLA_DK = 64
GLA_DV = 128
GLA_RANK = 16
GLA_TAU = 16.0
GLA_CHUNK = 64
NSA_HEADS = 4
NSA_KV_HEADS = 2
NSA_DH = 128
NSA_BRANCHES = 3
CMP_BLOCK = 32
CMP_STRIDE = 16
CMP_HIDDEN = 256
SEL_BLOCK = 64
SEL_TOPK = 16
WINDOW = 512
N_MOD = 6
ROPE_THETA = 10000.0
EPS = 1e-6
NEG = -1e30
BIG = 1e30

GLA_QK = GLA_HEADS * GLA_DK
GLA_WIDTH = GLA_HEADS * GLA_DV
NSA_WIDTH = NSA_HEADS * NSA_DH
NSA_KV_WIDTH = NSA_KV_HEADS * NSA_DH
N_GATE = NSA_HEADS * NSA_BRANCHES
IN_SPLITS = (GLA_QK, GLA_QK, GLA_WIDTH, GLA_WIDTH, GLA_RANK, NSA_WIDTH) + (NSA_KV_WIDTH,) * 6 + (N_GATE,)

LANES = 128
MISC_W = LANES
GATE_OFF = GLA_RANK
VMEM_LIMIT = 56 * 1024 * 1024

F32 = jnp.float32
BF16 = jnp.bfloat16
HIGHEST = lax.Precision.HIGHEST

_NT = (((1,), (1,)), ((), ()))
_TN = (((0,), (0,)), ((), ()))


def _params(*sem):
    return pltpu.CompilerParams(dimension_semantics=sem, vmem_limit_bytes=VMEM_LIMIT)


def _rms(x):
    return x * lax.rsqrt(jnp.mean(x * x, axis=-1, keepdims=True) + EPS)


def _adaln_kernel(c_ref, w_ref, b_ref, o_ref):
    c = c_ref[...]
    a = c * jax.nn.sigmoid(c)
    o_ref[...] = jnp.dot(a, w_ref[...], precision=HIGHEST, preferred_element_type=F32) + b_ref[...]


def _adaln(c, w_ada, b_ada):
    B, D = c.shape
    N = w_ada.shape[1]
    tn = D
    return pl.pallas_call(
        _adaln_kernel,
        grid=(N // tn,),
        in_specs=[pl.BlockSpec((B, D), lambda j: (0, 0)),
                  pl.BlockSpec((D, tn), lambda j: (0, j)),
                  pl.BlockSpec((1, tn), lambda j: (0, j))],
        out_specs=pl.BlockSpec((B, tn), lambda j: (0, j)),
        out_shape=jax.ShapeDtypeStruct((B, N), F32),
        compiler_params=_params("arbitrary"),
        name="adaln",
    )(c, w_ada, b_ada.reshape(1, N))


_PROJ_GROUPS = (("gq", GLA_QK, False), ("gk", GLA_QK, False), ("gv", GLA_WIDTH, False),
                ("gg", GLA_WIDTH, False), ("nq", NSA_WIDTH, True), ("kc", NSA_KV_WIDTH, True),
                ("vc", NSA_KV_WIDTH, False), ("ks", NSA_KV_WIDTH, True), ("vs", NSA_KV_WIDTH, False),
                ("kw", NSA_KV_WIDTH, True), ("vw", NSA_KV_WIDTH, False))
_PROJ_W = sum(g[1] for g in _PROJ_GROUPS) + MISC_W


def _in_proj_kernel(x_ref, mod_ref, g_ref, pos_ref, inv_ref, w_ref, *out_refs):
    x = x_ref[...]
    h = _rms(x) * g_ref[...] * (1.0 + mod_ref[1:2, :]) + mod_ref[0:1, :]
    hb = h.astype(BF16)

    ang = pos_ref[...].astype(F32) * inv_ref[...]
    lane = lax.broadcasted_iota(jnp.int32, (1, LANES), 1)
    cos = jnp.cos(ang)
    sin = jnp.sin(ang) * jnp.where(lane < NSA_DH // 2, -1.0, 1.0)
    qs = NSA_DH ** -0.5

    off = 0
    for (name, width, rot), o_ref in zip(_PROJ_GROUPS, out_refs[:-1]):
        y = jnp.dot(hb, w_ref[:, off:off + width], preferred_element_type=F32)
        if rot:
            c, s = (cos * qs, sin * qs) if name == "nq" else (cos, sin)
            for hd in range(width // NSA_DH):
                yh = y[:, hd * NSA_DH:(hd + 1) * NSA_DH]
                o_ref[:, hd * NSA_DH:(hd + 1) * NSA_DH] = (
                    yh * c + pltpu.roll(yh, NSA_DH // 2, axis=1) * s).astype(o_ref.dtype)
        else:
            o_ref[...] = y.astype(o_ref.dtype)
        off += width
    out_refs[-1][...] = jnp.dot(hb, w_ref[:, off:off + MISC_W], preferred_element_type=F32)


def _in_proj(x, mod, norm_g, pos3, inv, w_p, tm):
    B, S, D = x.shape
    row = lambda w: pl.BlockSpec((None, tm, w), lambda b, i: (b, i, 0))
    whole = lambda a: pl.BlockSpec(a.shape, lambda b, i: (0,) * a.ndim)
    out_shape = [jax.ShapeDtypeStruct((B, S, w), BF16) for _, w, _ in _PROJ_GROUPS]
    out_shape.append(jax.ShapeDtypeStruct((B, S, MISC_W), F32))
    return pl.pallas_call(
        _in_proj_kernel,
        grid=(B, S // tm),
        in_specs=[row(D), pl.BlockSpec((None, N_MOD, D), lambda b, i: (b, 0, 0)), whole(norm_g),
                  row(1), whole(inv), whole(w_p)],
        out_specs=[row(w) for _, w, _ in _PROJ_GROUPS] + [row(MISC_W)],
        out_shape=out_shape,
        compiler_params=_params("parallel", "parallel"),
        name="in_proj",
    )(x, mod, norm_g, pos3, inv, w_p)


def _log_sigmoid(z):
    return jnp.minimum(z, 0.0) - jnp.log(1.0 + jnp.exp(-jnp.abs(z)))


def _gla_kernel(q_ref, k_ref, v_ref, g_ref, misc_ref, wa_ref, ba_ref, gn_ref, o_ref, st_ref):
    C = GLA_CHUNK
    tg = q_ref.shape[0]

    @pl.when(pl.program_id(1) == 0)
    def _():
        st_ref[...] = jnp.zeros_like(st_ref)

    z = jnp.dot(misc_ref[...], wa_ref[...], precision=HIGHEST, preferred_element_type=F32) + ba_ref[...]
    la = _log_sigmoid(z) * (1.0 / GLA_TAU)
    r = lax.broadcasted_iota(jnp.int32, (C, C), 0)
    c = lax.broadcasted_iota(jnp.int32, (C, C), 1)
    causal = r >= c
    tri = causal.astype(F32)

    for ci in range(tg // C):
        rows = slice(ci * C, (ci + 1) * C)
        b = jnp.dot(tri, la[rows], precision=HIGHEST, preferred_element_type=F32)
        b_last = b[C - 1:C, :]
        qf = q_ref[rows, :].astype(F32)
        kf = k_ref[rows, :].astype(F32)
        q_in = (qf * GLA_DK ** -0.5 * jnp.exp(b)).astype(BF16)
        k_in = (kf * jnp.exp(-b)).astype(BF16)
        k_dec = (kf * jnp.exp(b_last - b)).astype(BF16)
        dec = jnp.exp(b_last)
        gate = g_ref[rows, :].astype(F32)
        gate = gate * jax.nn.sigmoid(gate)
        for h in range(GLA_HEADS):
            ks = slice(h * GLA_DK, (h + 1) * GLA_DK)
            vs = slice(h * GLA_DV, (h + 1) * GLA_DV)
            vh = v_ref[rows, vs]
            st = st_ref[h]
            att = lax.dot_general(q_in[:, ks], k_in[:, ks], _NT, preferred_element_type=F32)
            att = jnp.where(causal, att, 0.0).astype(BF16)
            o = (jnp.dot(att, vh, preferred_element_type=F32)
                 + lax.dot_general(q_in[:, ks], st.astype(BF16), _NT, preferred_element_type=F32))
            st_ref[h] = st * dec[:, ks] + lax.dot_general(vh, k_dec[:, ks], _TN, preferred_element_type=F32)
            o_ref[rows, vs] = (_rms(o) * gn_ref[:, vs] * gate[:, vs]).astype(o_ref.dtype)


def _gla(gq, gk, gv, gg, misc, wa_p, ba, gn, tg):
    B, S, _ = gq.shape
    row = lambda w: pl.BlockSpec((None, tg, w), lambda b, i: (b, i, 0))
    whole = lambda a: pl.BlockSpec(a.shape, lambda b, i: (0,) * a.ndim)
    return pl.pallas_call(
        _gla_kernel,
        grid=(B, S // tg),
        in_specs=[row(GLA_QK), row(GLA_QK), row(GLA_WIDTH), row(GLA_WIDTH), row(MISC_W),
                  whole(wa_p), whole(ba), whole(gn)],
        out_specs=row(GLA_WIDTH),
        out_shape=jax.ShapeDtypeStruct((B, S, GLA_WIDTH), BF16),
        scratch_shapes=[pltpu.VMEM((GLA_HEADS, GLA_DV, GLA_DK), F32)],
        compiler_params=_params("parallel", "arbitrary"),
        name="gla",
    )(gq, gk, gv, gg, misc, wa_p, ba, gn)


HALF_TOK = CMP_BLOCK // 2


def _compress_kernel(x_ref, w1_ref, pe_ref, w2_ref, o_ref):
    n_grp = o_ref.shape[0]
    hw = NSA_KV_HEADS * CMP_HIDDEN
    w1 = w1_ref[...]
    u = jnp.dot(x_ref[...], w1, preferred_element_type=F32)
    pb = jnp.dot(pe_ref[...], w1, preferred_element_type=F32)
    bias = pb[0:1, 0:hw] + pb[1:2, hw:2 * hw]
    hid = u[:, 0:hw] + pltpu.roll(u[:, hw:2 * hw], n_grp - 1, axis=0) + bias
    act = jax.nn.gelu(hid, approximate=True).astype(BF16)
    for k in range(NSA_KV_HEADS):
        o_ref[:, k * NSA_DH:(k + 1) * NSA_DH] = jnp.dot(
            act[:, k * CMP_HIDDEN:(k + 1) * CMP_HIDDEN], w2_ref[...],
            preferred_element_type=F32).astype(o_ref.dtype)


def _compress(t, w1_p, pe_p, w2):
    B, S, W = t.shape
    n_grp = S // CMP_STRIDE
    x = t.reshape(B, n_grp, CMP_STRIDE * W)
    whole = lambda a: pl.BlockSpec(a.shape, lambda b: (0,) * a.ndim)
    return pl.pallas_call(
        _compress_kernel,
        grid=(B,),
        in_specs=[pl.BlockSpec((None, n_grp, CMP_STRIDE * W), lambda b: (b, 0, 0)),
                  whole(w1_p), whole(pe_p), whole(w2)],
        out_specs=pl.BlockSpec((None, n_grp, W), lambda b: (b, 0, 0)),
        out_shape=jax.ShapeDtypeStruct((B, n_grp, W), BF16),
        compiler_params=_params("parallel"),
        name="compress",
    )(x, w1_p, pe_p, w2)


def _compress_weights(w1, pe):
    w1r = w1.reshape(2, HALF_TOK, NSA_DH, CMP_HIDDEN)
    eye = jnp.eye(NSA_KV_HEADS, dtype=w1.dtype)
    w1_p = jnp.einsum("hldj,kq->lkdhqj", w1r, eye).reshape(
        HALF_TOK * NSA_KV_HEADS * NSA_DH, 2 * NSA_KV_HEADS * CMP_HIDDEN)
    pe_r = jnp.broadcast_to(pe.reshape(2, HALF_TOK, 1, NSA_DH), (2, HALF_TOK, NSA_KV_HEADS, NSA_DH))
    pe_p = jnp.zeros((8, HALF_TOK * NSA_KV_HEADS * NSA_DH), pe.dtype).at[0:2].set(pe_r.reshape(2, -1))
    return w1_p.astype(BF16), pe_p.astype(BF16)


def _softmax_start(s, v):
    m = jnp.max(s, axis=-1, keepdims=True)
    p = jnp.exp(s - m)
    return m, jnp.sum(p, axis=-1, keepdims=True), jnp.dot(p.astype(BF16), v, preferred_element_type=F32)


def _softmax_step(carry, s, v):
    m, l, acc = carry
    m_new = jnp.maximum(m, jnp.max(s, axis=-1, keepdims=True))
    a = jnp.exp(m - m_new)
    p = jnp.exp(s - m_new)
    return (m_new, a * l + jnp.sum(p, axis=-1, keepdims=True),
            a * acc + jnp.dot(p.astype(BF16), v, preferred_element_type=F32))


def _nsa_kernel(q_ref, kc_ref, vc_ref, ks_ref, vs_ref, kw_ref, vw_ref, misc_ref, gn_ref, o_ref, kaug_ref):
    tq = q_ref.shape[0]
    S = ks_ref.shape[0]
    n_cmp = kc_ref.shape[0]
    n_sel = S // SEL_BLOCK
    G = NSA_HEADS // NSA_KV_HEADS
    i = pl.program_id(1)
    q0 = pl.multiple_of(i * tq, tq)

    @pl.when(i == 0)
    def _():
        pos = lax.broadcasted_iota(jnp.int32, (S, LANES), 0)
        lane = lax.broadcasted_iota(jnp.int32, (S, LANES), 1)
        onehot = jnp.where(lax.shift_right_logical(pos, 6) == lane, 1.0, 0.0).astype(BF16)
        for k in range(NSA_KV_HEADS):
            kaug_ref[k, :, 0:NSA_DH] = ks_ref[:, k * NSA_DH:(k + 1) * NSA_DH]
            kaug_ref[k, :, NSA_DH:2 * NSA_DH] = onehot

    t = q0 + lax.broadcasted_iota(jnp.int32, (tq, 1), 0)
    tg = jnp.concatenate([t] * G, axis=0)
    lane = lax.broadcasted_iota(jnp.int32, (1, LANES), 1)
    rr = lax.broadcasted_iota(jnp.int32, (tq, tq), 0)
    cc = lax.broadcasted_iota(jnp.int32, (tq, tq), 1)
    r_minus_c = jnp.concatenate([rr - cc] * G, axis=0)
    diag_ok = r_minus_c >= 0

    cmp_end = jnp.where(lane < n_cmp - 1, lane * CMP_STRIDE + (CMP_BLOCK - 1), jnp.int32(2 ** 30))
    cmp_valid = cmp_end <= tg
    any_valid = (tg >= CMP_BLOCK - 1).astype(F32)
    nn = lax.broadcasted_iota(jnp.int32, (n_cmp, LANES), 0) * CMP_STRIDE
    jj = lax.broadcasted_iota(jnp.int32, (n_cmp, LANES), 1) * SEL_BLOCK
    ov = jnp.maximum(jnp.minimum(nn + CMP_BLOCK, jj + SEL_BLOCK) - jnp.maximum(nn, jj), 0)
    ov = ov.astype(F32) * (1.0 / CMP_BLOCK)
    blk_t = lax.shift_right_logical(t, 6)
    forced = (lane == 0) | (lane == blk_t) | (lane == blk_t - 1)
    gates = jax.nn.sigmoid(misc_ref[...])

    heads = [None] * NSA_HEADS
    for k in range(NSA_KV_HEADS):
        hs = slice(k * NSA_DH, (k + 1) * NSA_DH)
        q2 = jnp.concatenate([q_ref[:, (k * G + g) * NSA_DH:(k * G + g + 1) * NSA_DH] for g in range(G)], axis=0)

        s = lax.dot_general(q2, kc_ref[:, hs], _NT, preferred_element_type=F32)
        s = jnp.where(cmp_valid, s, NEG)
        e = jnp.exp(s - jnp.max(s, axis=-1, keepdims=True))
        p = e / jnp.sum(e, axis=-1, keepdims=True) * any_valid
        o_cmp = jnp.dot(p.astype(BF16), vc_ref[:, hs], preferred_element_type=F32)

        p_grp = p[0:tq]
        for g in range(1, G):
            p_grp = p_grp + p[g * tq:(g + 1) * tq]
        imp = jnp.dot(p_grp, ov, precision=HIGHEST, preferred_element_type=F32)
        imp = jnp.where(forced, BIG, jnp.where(lane <= blk_t, imp, NEG))
        imp = jnp.where(lane < n_sel, imp, -3e38)
        rank = jnp.zeros((tq, LANES), F32)
        for j in range(n_sel):
            col = imp[:, j:j + 1]
            ahead = jnp.where(lane > j, jnp.where(col >= imp, 1.0, 0.0), jnp.where(col > imp, 1.0, 0.0))
            rank = rank + ahead
        bias = jnp.where((rank < float(min(SEL_TOPK, n_sel))) | (lane >= n_sel), 0.0, NEG).astype(BF16)
        q2a = jnp.concatenate([q2, jnp.concatenate([bias] * G, axis=0)], axis=1)

        s = lax.dot_general(q2a, kaug_ref[k, pl.ds(q0, tq), :], _NT, preferred_element_type=F32)
        carry = _softmax_start(jnp.where(diag_ok, s, NEG), vs_ref[pl.ds(q0, tq), hs])

        def sel_body(kt, carry):
            k0 = pl.multiple_of(kt * tq, tq)
            s = lax.dot_general(q2a, kaug_ref[k, pl.ds(k0, tq), :], _NT, preferred_element_type=F32)
            return _softmax_step(carry, s, vs_ref[pl.ds(k0, tq), hs])

        m, l, acc = lax.fori_loop(0, i, sel_body, carry)
        o_sel = acc / l

        s = lax.dot_general(q2, kw_ref[pl.ds(q0, tq), hs], _NT, preferred_element_type=F32)
        carry = _softmax_start(jnp.where(diag_ok, s, NEG), vw_ref[pl.ds(q0, tq), hs])

        def win_body(kt, carry):
            k0 = pl.multiple_of(kt * tq, tq)
            s = lax.dot_general(q2, kw_ref[pl.ds(k0, tq), hs], _NT, preferred_element_type=F32)
            dist = r_minus_c + (i - kt) * tq
            return _softmax_step(carry, jnp.where(dist < WINDOW, s, NEG), vw_ref[pl.ds(k0, tq), hs])

        m, l, acc = lax.fori_loop(jnp.maximum(i - (WINDOW + tq - 1) // tq, 0), i, win_body, carry)
        o_win = acc / l

        for g in range(G):
            hq = k * G + g
            gs = slice(g * tq, (g + 1) * tq)
            gl = GATE_OFF + hq * NSA_BRANCHES
            heads[hq] = (gates[:, gl:gl + 1] * o_cmp[gs] + gates[:, gl + 1:gl + 2] * o_sel[gs]
                         + gates[:, gl + 2:gl + 3] * o_win[gs])

    o = jnp.concatenate(heads, axis=1)
    o_ref[...] = (_rms(o) * gn_ref[...]).astype(o_ref.dtype)


def _nsa(nq, kcmp, vcmp, ks, vs, kw, vw, misc, gn, tq):
    B, S, _ = nq.shape
    n_cmp = kcmp.shape[1]
    assert n_cmp <= LANES and S // SEL_BLOCK <= LANES and SEL_BLOCK == 64
    row = lambda w: pl.BlockSpec((None, tq, w), lambda b, i: (b, i, 0))
    seq = lambda n: pl.BlockSpec((None, n, NSA_KV_WIDTH), lambda b, i: (b, 0, 0))
    return pl.pallas_call(
        _nsa_kernel,
        grid=(B, S // tq),
        in_specs=[row(NSA_WIDTH), seq(n_cmp), seq(n_cmp), seq(S), seq(S), seq(S), seq(S), row(MISC_W),
                  pl.BlockSpec(gn.shape, lambda b, i: (0, 0))],
        out_specs=row(NSA_WIDTH),
        out_shape=jax.ShapeDtypeStruct((B, S, NSA_WIDTH), BF16),
        scratch_shapes=[pltpu.VMEM((NSA_KV_HEADS, S, 2 * NSA_DH), BF16)],
        compiler_params=_params("parallel", "arbitrary"),
        name="nsa",
    )(nq, kcmp, vcmp, ks, vs, kw, vw, misc, gn)


def _out_ffn_kernel(x_ref, og_ref, on_ref, mod_ref, wo_ref, g2_ref, w1_ref, w2_ref, gf_ref, o_ref, *, ff_chunk, final):
    wg = og_ref.shape[1]
    mix = (jnp.dot(og_ref[...], wo_ref[0:wg, :], preferred_element_type=F32)
           + jnp.dot(on_ref[...], wo_ref[wg:, :], preferred_element_type=F32))
    x1 = x_ref[...] + mod_ref[2:3, :] * mix
    h = (_rms(x1) * g2_ref[...] * (1.0 + mod_ref[4:5, :]) + mod_ref[3:4, :]).astype(BF16)
    acc = jnp.zeros_like(x1)
    for c in range(w1_ref.shape[1] // ff_chunk):
        cs = slice(c * ff_chunk, (c + 1) * ff_chunk)
        a = jnp.maximum(jnp.dot(h, w1_ref[:, cs], preferred_element_type=F32), 0.0)
        acc = acc + jnp.dot((a * a).astype(BF16), w2_ref[cs, :], preferred_element_type=F32)
    x2 = x1 + mod_ref[5:6, :] * acc
    o_ref[...] = _rms(x2) * gf_ref[...] if final else x2


def _out_ffn(x, og, on, mod, wo, g2, w1, w2, gf, tm, final):
    B, S, D = x.shape
    row = lambda w: pl.BlockSpec((None, tm, w), lambda b, i: (b, i, 0))
    whole = lambda a: pl.BlockSpec(a.shape, lambda b, i: (0,) * a.ndim)
    return pl.pallas_call(
        functools.partial(_out_ffn_kernel, ff_chunk=1024, final=final),
        grid=(B, S // tm),
        in_specs=[row(D), row(og.shape[2]), row(on.shape[2]),
                  pl.BlockSpec((None, N_MOD, D), lambda b, i: (b, 0, 0)),
                  whole(wo), whole(g2), whole(w1), whole(w2), whole(gf)],
        out_specs=row(D),
        out_shape=jax.ShapeDtypeStruct((B, S, D), F32),
        compiler_params=_params("parallel", "parallel"),
        name="out_ffn",
    )(x, og, on, mod, wo, g2, w1, w2, gf)


def _reorder_w_in(w_in):
    cuts = [int(v) for v in np.cumsum(IN_SPLITS)[:-1]]
    gq, gk, gv, gg, gr, nq, kc, vc, ks, vs, kw, vw, ngate = jnp.split(w_in, cuts, axis=1)
    pad = jnp.zeros((w_in.shape[0], MISC_W - GLA_RANK - N_GATE), w_in.dtype)
    return jnp.concatenate([gq, gk, gv, gg, nq, kc, vc, ks, vs, kw, vw, gr, ngate, pad], axis=1).astype(BF16)


def kernel(x, c, positions, w_ada, b_ada, norm1_g, w_in, gla_w_a2, gla_b_a, gla_norm_g, nsa_pe_k, nsa_pe_v, cmp_k_w1, cmp_k_w2, cmp_v_w1, cmp_v_w2, nsa_norm_g, w_o, norm2_g, w_ff1, w_ff2, final_norm_g):
    B, S, D = x.shape
    depth = w_in.shape[0]
    half = NSA_DH // 2
    inv = ROPE_THETA ** (-jnp.arange(half, dtype=F32) / half)
    inv = jnp.concatenate([inv, inv]).reshape(1, NSA_DH)
    pos3 = positions.reshape(B, S, 1)
    for l in range(depth):
        mod = _adaln(c, w_ada[l], b_ada[l]).reshape(B, N_MOD, D)
        (gq, gk, gv, gg, nq, kc, vc, ks, vs, kw, vw, misc) = _in_proj(
            x, mod, norm1_g[l].reshape(1, D), pos3, inv, _reorder_w_in(w_in[l]), tm=512)

        wa_p = jnp.zeros((MISC_W, GLA_QK), F32).at[0:GLA_RANK].set(gla_w_a2[l])
        o_gla = _gla(gq, gk, gv, gg, misc, wa_p, gla_b_a[l].reshape(1, GLA_QK),
                     gla_norm_g[l].reshape(1, GLA_WIDTH), tg=256)

        kw1, kpe = _compress_weights(cmp_k_w1[l], nsa_pe_k[l])
        vw1, vpe = _compress_weights(cmp_v_w1[l], nsa_pe_v[l])
        k_cmp = _compress(kc, kw1, kpe, cmp_k_w2[l].astype(BF16))
        v_cmp = _compress(vc, vw1, vpe, cmp_v_w2[l].astype(BF16))
        o_nsa = _nsa(nq, k_cmp, v_cmp, ks, vs, kw, vw, misc, nsa_norm_g[l].reshape(1, NSA_WIDTH), tq=256)

        x = _out_ffn(x, o_gla, o_nsa, mod, w_o[l].astype(BF16), norm2_g[l].reshape(1, D),
                     w_ff1[l].astype(BF16), w_ff2[l].astype(BF16), final_norm_g.reshape(1, D),
                     tm=256, final=(l == depth - 1))
    return x
```

```python
import functools

import numpy as np
import jax
import jax.numpy as jnp
from jax import lax
from jax.experimental import pallas as pl
from jax.experimental.pallas import tpu as pltpu

GLA_HEADS = 4
GLA_DK = 64
GLA_DV = 128
GLA_RANK = 16
GLA_TAU = 16.0
GLA_CHUNK = 64
NSA_HEADS = 4
NSA_KV_HEADS = 2
NSA_DH = 128
NSA_BRANCHES = 3
CMP_BLOCK = 32
CMP_STRIDE = 16
CMP_HIDDEN = 256
SEL_BLOCK = 64
SEL_TOPK = 16
WINDOW = 512
N_MOD = 6
ROPE_THETA = 10000.0
EPS = 1e-6
NEG = -1e30
BIG = 1e30

GLA_QK = GLA_HEADS * GLA_DK
GLA_WIDTH = GLA_HEADS * GLA_DV
NSA_WIDTH = NSA_HEADS * NSA_DH
NSA_KV_WIDTH = NSA_KV_HEADS * NSA_DH
N_GATE = NSA_HEADS * NSA_BRANCHES
IN_SPLITS = (GLA_QK, GLA_QK, GLA_WIDTH, GLA_WIDTH, GLA_RANK, NSA_WIDTH) + (NSA_KV_WIDTH,) * 6 + (N_GATE,)

LANES = 128
MISC_W = LANES
GATE_OFF = GLA_RANK
VMEM_LIMIT = 56 * 1024 * 1024

F32 = jnp.float32
BF16 = jnp.bfloat16
HIGHEST = lax.Precision.HIGHEST

_NT = (((1,), (1,)), ((), ()))
_TN = (((0,), (0,)), ((), ()))


def _params(*sem):
    return pltpu.CompilerParams(dimension_semantics=sem, vmem_limit_bytes=VMEM_LIMIT)


def _rms(x):
    return x * lax.rsqrt(jnp.mean(x * x, axis=-1, keepdims=True) + EPS)


def _adaln_kernel(c_ref, w_ref, b_ref, o_ref):
    c = c_ref[...]
    a = c * jax.nn.sigmoid(c)
    o_ref[...] = jnp.dot(a, w_ref[...], precision=HIGHEST, preferred_element_type=F32) + b_ref[...]


def _adaln(c, w_ada, b_ada):
    B, D = c.shape
    N = w_ada.shape[1]
    tn = D
    return pl.pallas_call(
        _adaln_kernel,
        grid=(N // tn,),
        in_specs=[pl.BlockSpec((B, D), lambda j: (0, 0)),
                  pl.BlockSpec((D, tn), lambda j: (0, j)),
                  pl.BlockSpec((1, tn), lambda j: (0, j))],
        out_specs=pl.BlockSpec((B, tn), lambda j: (0, j)),
        out_shape=jax.ShapeDtypeStruct((B, N), F32),
        compiler_params=_params("arbitrary"),
        name="adaln",
    )(c, w_ada, b_ada.reshape(1, N))


_PROJ_GROUPS = (("gq", GLA_QK, False), ("gk", GLA_QK, False), ("gv", GLA_WIDTH, False),
                ("gg", GLA_WIDTH, False), ("nq", NSA_WIDTH, True), ("kc", NSA_KV_WIDTH, True),
                ("vc", NSA_KV_WIDTH, False), ("ks", NSA_KV_WIDTH, True), ("vs", NSA_KV_WIDTH, False),
                ("kw", NSA_KV_WIDTH, True), ("vw", NSA_KV_WIDTH, False))
_PROJ_W = sum(g[1] for g in _PROJ_GROUPS) + MISC_W


_GROUPED = ("kc", "vc")
GRP_W = CMP_STRIDE * NSA_DH


def _in_proj_kernel(x_ref, mod_ref, g_ref, pos_ref, inv_ref, w_ref, *refs):
    n_out = len(_PROJ_GROUPS) + 1
    out_refs, stage_refs = refs[:n_out], refs[n_out:]
    tm = x_ref.shape[0]
    x = x_ref[...]
    h = _rms(x) * g_ref[...] * (1.0 + mod_ref[1:2, :]) + mod_ref[0:1, :]
    hb = h.astype(BF16)

    ang = pos_ref[...].astype(F32) * inv_ref[...]
    lane = lax.broadcasted_iota(jnp.int32, (1, LANES), 1)
    cos = jnp.cos(ang)
    sin = jnp.sin(ang) * jnp.where(lane < NSA_DH // 2, -1.0, 1.0)
    qs = NSA_DH ** -0.5

    off = 0
    for (name, width, rot), o_ref in zip(_PROJ_GROUPS, out_refs[:-1]):
        y = jnp.dot(hb, w_ref[:, off:off + width], preferred_element_type=F32)
        grouped = name in _GROUPED
        dst = stage_refs[_GROUPED.index(name)] if grouped else o_ref
        for hd in range(width // NSA_DH if (rot or grouped) else 0):
            yh = y[:, hd * NSA_DH:(hd + 1) * NSA_DH]
            if rot:
                c, s = (cos * qs, sin * qs) if name == "nq" else (cos, sin)
                yh = yh * c + pltpu.roll(yh, NSA_DH // 2, axis=1) * s
            if grouped:
                dst[hd] = yh
            else:
                dst[:, hd * NSA_DH:(hd + 1) * NSA_DH] = yh.astype(dst.dtype)
        if not (rot or grouped):
            dst[...] = y.astype(dst.dtype)
        if grouped:
            for k in range(NSA_KV_HEADS):
                for tok in range(CMP_STRIDE):
                    c0 = k * GRP_W + tok * NSA_DH
                    o_ref[:, c0:c0 + NSA_DH] = dst[k, pl.ds(tok, tm // CMP_STRIDE, stride=CMP_STRIDE), :].astype(
                        o_ref.dtype)
        off += width
    out_refs[-1][...] = jnp.dot(hb, w_ref[:, off:off + MISC_W], preferred_element_type=F32)


def _in_proj(x, mod, norm_g, pos3, inv, w_p, tm):
    B, S, D = x.shape
    row = lambda w: pl.BlockSpec((None, tm, w), lambda b, i: (b, i, 0))
    whole = lambda a: pl.BlockSpec(a.shape, lambda b, i: (0,) * a.ndim)
    out_shape, out_specs = [], []
    for name, w, _ in _PROJ_GROUPS:
        if name in _GROUPED:
            out_shape.append(jax.ShapeDtypeStruct((B, S // CMP_STRIDE, CMP_STRIDE * w), BF16))
            out_specs.append(pl.BlockSpec((None, tm // CMP_STRIDE, CMP_STRIDE * w), lambda b, i: (b, i, 0)))
        else:
            out_shape.append(jax.ShapeDtypeStruct((B, S, w), BF16))
            out_specs.append(row(w))
    out_shape.append(jax.ShapeDtypeStruct((B, S, MISC_W), F32))
    out_specs.append(row(MISC_W))
    return pl.pallas_call(
        _in_proj_kernel,
        grid=(B, S // tm),
        in_specs=[row(D), pl.BlockSpec((None, N_MOD, D), lambda b, i: (b, 0, 0)), whole(norm_g),
                  row(1), whole(inv), whole(w_p)],
        out_specs=out_specs,
        out_shape=out_shape,
        scratch_shapes=[pltpu.VMEM((NSA_KV_HEADS, tm, NSA_DH), F32) for _ in _GROUPED],
        compiler_params=_params("parallel", "parallel"),
        name="in_proj",
    )(x, mod, norm_g, pos3, inv, w_p)


def _log_sigmoid(z):
    return jnp.minimum(z, 0.0) - jnp.log(1.0 + jnp.exp(-jnp.abs(z)))


def _gla_kernel(q_ref, k_ref, v_ref, g_ref, misc_ref, wa_ref, ba_ref, gn_ref, o_ref, st_ref):
    C = GLA_CHUNK
    tg = q_ref.shape[0]

    @pl.when(pl.program_id(1) == 0)
    def _():
        st_ref[...] = jnp.zeros_like(st_ref)

    z = jnp.dot(misc_ref[...], wa_ref[...], precision=HIGHEST, preferred_element_type=F32) + ba_ref[...]
    la = _log_sigmoid(z) * (1.0 / GLA_TAU)
    r = lax.broadcasted_iota(jnp.int32, (tg, tg), 0)
    c = lax.broadcasted_iota(jnp.int32, (tg, tg), 1)
    causal = (r >= c) & (r - c <= (r & (C - 1)))
    tri = jnp.where(causal, 1.0, 0.0)
    b = jnp.dot(tri, la, precision=HIGHEST, preferred_element_type=F32)
    n_chunk = tg // C
    b_lasts = [b[(ci + 1) * C - 1:(ci + 1) * C, :] for ci in range(n_chunk)]
    b_last = jnp.concatenate([jnp.broadcast_to(bl, (C, bl.shape[1])) for bl in b_lasts], axis=0)
    qf = q_ref[...].astype(F32)
    kf = k_ref[...].astype(F32)
    q_in = (qf * GLA_DK ** -0.5 * jnp.exp(b)).astype(BF16)
    k_in = (kf * jnp.exp(-b)).astype(BF16)
    k_dec = (kf * jnp.exp(b_last - b)).astype(BF16)
    decs = [jnp.exp(bl) for bl in b_lasts]
    gate = g_ref[...].astype(F32)
    gate = gate * jax.nn.sigmoid(gate)
    for h in range(GLA_HEADS):
        ks = slice(h * GLA_DK, (h + 1) * GLA_DK)
        vs = slice(h * GLA_DV, (h + 1) * GLA_DV)
        vh = v_ref[:, vs]
        att = lax.dot_general(q_in[:, ks], k_in[:, ks], _NT, preferred_element_type=F32)
        att = jnp.where(causal, att, 0.0).astype(BF16)
        o_intra = jnp.dot(att, vh, preferred_element_type=F32)
        st = st_ref[h]
        o_inter = []
        for ci in range(n_chunk):
            rows = slice(ci * C, (ci + 1) * C)
            o_inter.append(lax.dot_general(q_in[rows, ks], st.astype(BF16), _NT, preferred_element_type=F32))
            st = st * decs[ci][:, ks] + lax.dot_general(vh[rows], k_dec[rows, ks], _TN,
                                                         preferred_element_type=F32)
        st_ref[h] = st
        o = o_intra + jnp.concatenate(o_inter, axis=0)
        o_ref[:, vs] = (_rms(o) * gn_ref[:, vs] * gate[:, vs]).astype(o_ref.dtype)


def _gla(gq, gk, gv, gg, misc, wa_p, ba, gn, tg):
    B, S, _ = gq.shape
    row = lambda w: pl.BlockSpec((None, tg, w), lambda b, i: (b, i, 0))
    whole = lambda a: pl.BlockSpec(a.shape, lambda b, i: (0,) * a.ndim)
    return pl.pallas_call(
        _gla_kernel,
        grid=(B, S // tg),
        in_specs=[row(GLA_QK), row(GLA_QK), row(GLA_WIDTH), row(GLA_WIDTH), row(MISC_W),
                  whole(wa_p), whole(ba), whole(gn)],
        out_specs=row(GLA_WIDTH),
        out_shape=jax.ShapeDtypeStruct((B, S, GLA_WIDTH), BF16),
        scratch_shapes=[pltpu.VMEM((GLA_HEADS, GLA_DV, GLA_DK), F32)],
        compiler_params=_params("parallel", "arbitrary"),
        name="gla",
    )(gq, gk, gv, gg, misc, wa_p, ba, gn)


HALF_TOK = CMP_BLOCK // 2


def _compress_kernel(xk_ref, xv_ref, kw1_ref, kpe_ref, kw2_ref, vw1_ref, vpe_ref, vw2_ref, ok_ref, ov_ref):
    nb, n_grp, _ = xk_ref.shape
    rows = nb * n_grp
    for x_ref, w1_ref, pe_ref, w2_ref, o_ref in ((xk_ref, kw1_ref, kpe_ref, kw2_ref, ok_ref),
                                                 (xv_ref, vw1_ref, vpe_ref, vw2_ref, ov_ref)):
        w1a = w1_ref[0:GRP_W, :]
        w1b = w1_ref[GRP_W:2 * GRP_W, :]
        x = jnp.concatenate([x_ref[:, :, k * GRP_W:(k + 1) * GRP_W].reshape(rows, GRP_W)
                             for k in range(NSA_KV_HEADS)], axis=0)
        bias = (jnp.dot(pe_ref[:, 0:GRP_W], w1a, preferred_element_type=F32)[0:1]
                + jnp.dot(pe_ref[:, GRP_W:2 * GRP_W], w1b, preferred_element_type=F32)[0:1])
        ua = jnp.dot(x, w1a, preferred_element_type=F32)
        ub = jnp.dot(x, w1b, preferred_element_type=F32)
        hid = ua + pltpu.roll(ub, NSA_KV_HEADS * rows - 1, axis=0) + bias
        act = jax.nn.gelu(hid, approximate=True).astype(BF16)
        y = jnp.dot(act, w2_ref[...], preferred_element_type=F32).astype(o_ref.dtype)
        for k in range(NSA_KV_HEADS):
            o_ref[:, :, k * NSA_DH:(k + 1) * NSA_DH] = y[k * rows:(k + 1) * rows].reshape(nb, n_grp, NSA_DH)


def _compress(xk, xv, kw1, kpe, kw2, vw1, vpe, vw2, nb):
    B, n_grp, gw = xk.shape
    W = NSA_KV_WIDTH
    whole = lambda a: pl.BlockSpec(a.shape, lambda b: (0,) * a.ndim)
    xspec = pl.BlockSpec((nb, n_grp, gw), lambda b: (b, 0, 0))
    ospec = pl.BlockSpec((nb, n_grp, W), lambda b: (b, 0, 0))
    return pl.pallas_call(
        _compress_kernel,
        grid=(B // nb,),
        in_specs=[xspec, xspec, whole(kw1), whole(kpe), whole(kw2), whole(vw1), whole(vpe), whole(vw2)],
        out_specs=[ospec, ospec],
        out_shape=[jax.ShapeDtypeStruct((B, n_grp, W), BF16)] * 2,
        compiler_params=_params("parallel"),
        name="compress",
    )(xk, xv, kw1, kpe, kw2, vw1, vpe, vw2)


def _compress_pe(pe):
    return jnp.zeros((8, 2 * GRP_W), F32).at[0].set(pe.reshape(2 * GRP_W)).astype(BF16)


def _softmax_start(s, v):
    m = jnp.max(s, axis=-1, keepdims=True)
    p = jnp.exp(s - m)
    return m, jnp.sum(p, axis=-1, keepdims=True), jnp.dot(p.astype(BF16), v, preferred_element_type=F32)


def _softmax_step(carry, s, v):
    m, l, acc = carry
    m_new = jnp.maximum(m, jnp.max(s, axis=-1, keepdims=True))
    a = jnp.exp(m - m_new)
    p = jnp.exp(s - m_new)
    return (m_new, a * l + jnp.sum(p, axis=-1, keepdims=True),
            a * acc + jnp.dot(p.astype(BF16), v, preferred_element_type=F32))


def _nsa_kernel(q_ref, kc_ref, vc_ref, ks_ref, vs_ref, kw_ref, vw_ref, misc_ref, gn_ref, o_ref, kaug_ref):
    tq = q_ref.shape[0]
    S = ks_ref.shape[0]
    n_cmp = kc_ref.shape[0]
    n_sel = S // SEL_BLOCK
    G = NSA_HEADS // NSA_KV_HEADS
    i = pl.program_id(1)
    q0 = pl.multiple_of(i * tq, tq)

    @pl.when(i == 0)
    def _():
        pos = lax.broadcasted_iota(jnp.int32, (S, LANES), 0)
        lane = lax.broadcasted_iota(jnp.int32, (S, LANES), 1)
        onehot = jnp.where(lax.shift_right_logical(pos, 6) == lane, 1.0, 0.0).astype(BF16)
        for k in range(NSA_KV_HEADS):
            kaug_ref[k, :, 0:NSA_DH] = ks_ref[:, k * NSA_DH:(k + 1) * NSA_DH]
            kaug_ref[k, :, NSA_DH:2 * NSA_DH] = onehot

    t = q0 + lax.broadcasted_iota(jnp.int32, (tq, 1), 0)
    tg = jnp.concatenate([t] * G, axis=0)
    lane = lax.broadcasted_iota(jnp.int32, (1, LANES), 1)
    rr = lax.broadcasted_iota(jnp.int32, (tq, tq), 0)
    cc = lax.broadcasted_iota(jnp.int32, (tq, tq), 1)
    r_minus_c = jnp.concatenate([rr - cc] * G, axis=0)
    diag_ok = r_minus_c >= 0

    cmp_end = jnp.where(lane < n_cmp - 1, lane * CMP_STRIDE + (CMP_BLOCK - 1), jnp.int32(2 ** 30))
    cmp_valid = cmp_end <= tg
    any_valid = (tg >= CMP_BLOCK - 1).astype(F32)
    jj = lax.broadcasted_iota(jnp.int32, (n_sel, n_cmp), 0) * SEL_BLOCK
    nn = lax.broadcasted_iota(jnp.int32, (n_sel, n_cmp), 1) * CMP_STRIDE
    ov_t = jnp.maximum(jnp.minimum(nn + CMP_BLOCK, jj + SEL_BLOCK) - jnp.maximum(nn, jj), 0)
    ov_t = (ov_t.astype(F32) * (1.0 / CMP_BLOCK)).astype(BF16)
    j_row = lax.broadcasted_iota(jnp.int32, (n_sel, tq), 0)
    blk_t = lax.shift_right_logical(q0 + lax.broadcasted_iota(jnp.int32, (n_sel, tq), 1), 6)
    forced = (j_row == 0) | (j_row == blk_t) | (j_row == blk_t - 1)
    in_past = j_row <= blk_t
    gates = jax.nn.sigmoid(misc_ref[...])

    q2s, q2as, o_cmps = [], [], []
    for k in range(NSA_KV_HEADS):
        hs = slice(k * NSA_DH, (k + 1) * NSA_DH)
        q2 = jnp.concatenate([q_ref[:, (k * G + g) * NSA_DH:(k * G + g + 1) * NSA_DH] for g in range(G)], axis=0)

        s = lax.dot_general(q2, kc_ref[:, hs], _NT, preferred_element_type=F32)
        s = jnp.where(cmp_valid, s, NEG)
        e = jnp.exp(s - jnp.max(s, axis=-1, keepdims=True))
        p = e / jnp.sum(e, axis=-1, keepdims=True) * any_valid
        o_cmps.append(jnp.dot(p.astype(BF16), vc_ref[:, hs], preferred_element_type=F32))

        p_grp = p[0:tq]
        for g in range(1, G):
            p_grp = p_grp + p[g * tq:(g + 1) * tq]
        p_hi = p_grp.astype(BF16)
        p_lo = (p_grp - p_hi.astype(F32)).astype(BF16)
        imp = (lax.dot_general(ov_t, p_hi, _NT, preferred_element_type=F32)
               + lax.dot_general(ov_t, p_lo, _NT, preferred_element_type=F32))
        imp = jnp.where(forced, BIG, jnp.where(in_past, imp, NEG))
        rank = jnp.zeros((n_sel, tq), F32)
        for j in range(n_sel):
            row = imp[j:j + 1, :]
            rank = rank + jnp.where(j_row > j, jnp.where(row >= imp, 1.0, 0.0), jnp.where(row > imp, 1.0, 0.0))
        bias_t = jnp.where(rank < float(min(SEL_TOPK, n_sel)), 0.0, NEG)
        bias_t = jnp.concatenate([bias_t, jnp.zeros((LANES - n_sel, tq), F32)], axis=0)
        bias = bias_t.T.astype(BF16)
        q2s.append(q2)
        q2as.append(jnp.concatenate([q2, jnp.concatenate([bias] * G, axis=0)], axis=1))

    sel, win = [], []
    for k in range(NSA_KV_HEADS):
        hs = slice(k * NSA_DH, (k + 1) * NSA_DH)
        s = lax.dot_general(q2as[k], kaug_ref[k, pl.ds(q0, tq), :], _NT, preferred_element_type=F32)
        sel.append(_softmax_start(jnp.where(diag_ok, s, NEG), vs_ref[pl.ds(q0, tq), hs]))
        s = lax.dot_general(q2s[k], kw_ref[pl.ds(q0, tq), hs], _NT, preferred_element_type=F32)
        win.append(_softmax_start(jnp.where(diag_ok, s, NEG), vw_ref[pl.ds(q0, tq), hs]))

    def sel_body(kt, carry):
        k0 = pl.multiple_of(kt * tq, tq)
        out = []
        for k in range(NSA_KV_HEADS):
            s = lax.dot_general(q2as[k], kaug_ref[k, pl.ds(k0, tq), :], _NT, preferred_element_type=F32)
            out.append(_softmax_step(carry[k], s, vs_ref[pl.ds(k0, tq), k * NSA_DH:(k + 1) * NSA_DH]))
        return tuple(out)

    sel = lax.fori_loop(0, i, sel_body, tuple(sel))

    def win_body(kt, carry):
        k0 = pl.multiple_of(kt * tq, tq)
        far = r_minus_c + (i - kt) * tq >= WINDOW
        out = []
        for k in range(NSA_KV_HEADS):
            hs = slice(k * NSA_DH, (k + 1) * NSA_DH)
            s = lax.dot_general(q2s[k], kw_ref[pl.ds(k0, tq), hs], _NT, preferred_element_type=F32)
            out.append(_softmax_step(carry[k], jnp.where(far, NEG, s), vw_ref[pl.ds(k0, tq), hs]))
        return tuple(out)

    win = lax.fori_loop(jnp.maximum(i - (WINDOW + tq - 1) // tq, 0), i, win_body, tuple(win))

    heads = []
    for k in range(NSA_KV_HEADS):
        o_sel = sel[k][2] / sel[k][1]
        o_win = win[k][2] / win[k][1]
        for g in range(G):
            gs = slice(g * tq, (g + 1) * tq)
            gl = GATE_OFF + (k * G + g) * NSA_BRANCHES
            heads.append(gates[:, gl:gl + 1] * o_cmps[k][gs] + gates[:, gl + 1:gl + 2] * o_sel[gs]
                         + gates[:, gl + 2:gl + 3] * o_win[gs])

    o = jnp.concatenate(heads, axis=1)
    o_ref[...] = (_rms(o) * gn_ref[...]).astype(o_ref.dtype)


def _nsa(nq, kcmp, vcmp, ks, vs, kw, vw, misc, gn, tq):
    B, S, _ = nq.shape
    n_cmp = kcmp.shape[1]
    assert n_cmp <= LANES and S // SEL_BLOCK <= LANES and SEL_BLOCK == 64
    row = lambda w: pl.BlockSpec((None, tq, w), lambda b, i: (b, i, 0))
    seq = lambda n: pl.BlockSpec((None, n, NSA_KV_WIDTH), lambda b, i: (b, 0, 0))
    return pl.pallas_call(
        _nsa_kernel,
        grid=(B, S // tq),
        in_specs=[row(NSA_WIDTH), seq(n_cmp), seq(n_cmp), seq(S), seq(S), seq(S), seq(S), row(MISC_W),
                  pl.BlockSpec(gn.shape, lambda b, i: (0, 0))],
        out_specs=row(NSA_WIDTH),
        out_shape=jax.ShapeDtypeStruct((B, S, NSA_WIDTH), BF16),
        scratch_shapes=[pltpu.VMEM((NSA_KV_HEADS, S, 2 * NSA_DH), BF16)],
        compiler_params=_params("parallel", "arbitrary"),
        name="nsa",
    )(nq, kcmp, vcmp, ks, vs, kw, vw, misc, gn)


def _out_ffn_kernel(x_ref, og_ref, on_ref, mod_ref, wo_ref, g2_ref, w1_ref, w2_ref, gf_ref, o_ref, *, ff_chunk, final):
    wg = og_ref.shape[1]
    mix = (jnp.dot(og_ref[...], wo_ref[0:wg, :], preferred_element_type=F32)
           + jnp.dot(on_ref[...], wo_ref[wg:, :], preferred_element_type=F32))
    x1 = x_ref[...] + mod_ref[2:3, :] * mix
    h = (_rms(x1) * g2_ref[...] * (1.0 + mod_ref[4:5, :]) + mod_ref[3:4, :]).astype(BF16)
    acc = jnp.zeros_like(x1)
    for c in range(w1_ref.shape[1] // ff_chunk):
        cs = slice(c * ff_chunk, (c + 1) * ff_chunk)
        a = jnp.maximum(jnp.dot(h, w1_ref[:, cs], preferred_element_type=F32), 0.0)
        acc = acc + jnp.dot((a * a).astype(BF16), w2_ref[cs, :], preferred_element_type=F32)
    x2 = x1 + mod_ref[5:6, :] * acc
    o_ref[...] = _rms(x2) * gf_ref[...] if final else x2


def _out_ffn(x, og, on, mod, wo, g2, w1, w2, gf, tm, final):
    B, S, D = x.shape
    row = lambda w: pl.BlockSpec((None, tm, w), lambda b, i: (b, i, 0))
    whole = lambda a: pl.BlockSpec(a.shape, lambda b, i: (0,) * a.ndim)
    return pl.pallas_call(
        functools.partial(_out_ffn_kernel, ff_chunk=1024, final=final),
        grid=(B, S // tm),
        in_specs=[row(D), row(og.shape[2]), row(on.shape[2]),
                  pl.BlockSpec((None, N_MOD, D), lambda b, i: (b, 0, 0)),
                  whole(wo), whole(g2), whole(w1), whole(w2), whole(gf)],
        out_specs=row(D),
        out_shape=jax.ShapeDtypeStruct((B, S, D), F32),
        compiler_params=_params("parallel", "parallel"),
        name="out_ffn",
    )(x, og, on, mod, wo, g2, w1, w2, gf)


def _reorder_w_in(w_in):
    cuts = [int(v) for v in np.cumsum(IN_SPLITS)[:-1]]
    gq, gk, gv, gg, gr, nq, kc, vc, ks, vs, kw, vw, ngate = jnp.split(w_in, cuts, axis=1)
    pad = jnp.zeros((w_in.shape[0], MISC_W - GLA_RANK - N_GATE), w_in.dtype)
    return jnp.concatenate([gq, gk, gv, gg, nq, kc, vc, ks, vs, kw, vw, gr, ngate, pad], axis=1).astype(BF16)


def kernel(x, c, positions, w_ada, b_ada, norm1_g, w_in, gla_w_a2, gla_b_a, gla_norm_g, nsa_pe_k, nsa_pe_v, cmp_k_w1, cmp_k_w2, cmp_v_w1, cmp_v_w2, nsa_norm_g, w_o, norm2_g, w_ff1, w_ff2, final_norm_g):
    B, S, D = x.shape
    depth = w_in.shape[0]
    half = NSA_DH // 2
    inv = ROPE_THETA ** (-jnp.arange(half, dtype=F32) / half)
    inv = jnp.concatenate([inv, inv]).reshape(1, NSA_DH)
    pos3 = positions.reshape(B, S, 1)
    for l in range(depth):
        mod = _adaln(c, w_ada[l], b_ada[l]).reshape(B, N_MOD, D)
        (gq, gk, gv, gg, nq, kc, vc, ks, vs, kw, vw, misc) = _in_proj(
            x, mod, norm1_g[l].reshape(1, D), pos3, inv, _reorder_w_in(w_in[l]), tm=512)

        wa_p = jnp.zeros((MISC_W, GLA_QK), F32).at[0:GLA_RANK].set(gla_w_a2[l])
        o_gla = _gla(gq, gk, gv, gg, misc, wa_p, gla_b_a[l].reshape(1, GLA_QK),
                     gla_norm_g[l].reshape(1, GLA_WIDTH), tg=256)

        k_cmp, v_cmp = _compress(kc, vc, cmp_k_w1[l].astype(BF16), _compress_pe(nsa_pe_k[l]),
                                 cmp_k_w2[l].astype(BF16), cmp_v_w1[l].astype(BF16),
                                 _compress_pe(nsa_pe_v[l]), cmp_v_w2[l].astype(BF16), nb=int(np.gcd(B, 4)))
        o_nsa = _nsa(nq, k_cmp, v_cmp, ks, vs, kw, vw, misc, nsa_norm_g[l].reshape(1, NSA_WIDTH), tq=256)

        x = _out_ffn(x, o_gla, o_nsa, mod, w_o[l].astype(BF16), norm2_g[l].reshape(1, D),
                     w_ff1[l].astype(BF16), w_ff2[l].astype(BF16), final_norm_g.reshape(1, D),
                     tm=256, final=(l == depth - 1))
    return x
```

```python
import functools

import numpy as np
import jax
import jax.numpy as jnp
from jax import lax
from jax.experimental import pallas as pl
from jax.experimental.pallas import tpu as pltpu

GLA_HEADS = 4
GLA_DK = 64
GLA_DV = 128
GLA_RANK = 16
GLA_TAU = 16.0
GLA_CHUNK = 64
GLA_SUBTILE = 256
NSA_HEADS = 4
NSA_KV_HEADS = 2
NSA_DH = 128
NSA_BRANCHES = 3
CMP_BLOCK = 32
CMP_STRIDE = 16
CMP_HIDDEN = 256
SEL_BLOCK = 64
SEL_TOPK = 16
WINDOW = 512
N_MOD = 6
ROPE_THETA = 10000.0
EPS = 1e-6
NEG = -1e30
BIG = 1e30

GLA_QK = GLA_HEADS * GLA_DK
GLA_WIDTH = GLA_HEADS * GLA_DV
NSA_WIDTH = NSA_HEADS * NSA_DH
NSA_KV_WIDTH = NSA_KV_HEADS * NSA_DH
N_GATE = NSA_HEADS * NSA_BRANCHES
IN_SPLITS = (GLA_QK, GLA_QK, GLA_WIDTH, GLA_WIDTH, GLA_RANK, NSA_WIDTH) + (NSA_KV_WIDTH,) * 6 + (N_GATE,)

LANES = 128
MISC_W = LANES
GATE_OFF = GLA_RANK
VMEM_LIMIT = 56 * 1024 * 1024

F32 = jnp.float32
BF16 = jnp.bfloat16
HIGHEST = lax.Precision.HIGHEST

_NT = (((1,), (1,)), ((), ()))
_TN = (((0,), (0,)), ((), ()))


def _params(*sem):
    return pltpu.CompilerParams(dimension_semantics=sem, vmem_limit_bytes=VMEM_LIMIT)


def _rms(x):
    return x * lax.rsqrt(jnp.mean(x * x, axis=-1, keepdims=True) + EPS)


def _adaln_kernel(c_ref, w_ref, b_ref, o_ref):
    c = c_ref[...]
    a = c * jax.nn.sigmoid(c)
    o_ref[...] = jnp.dot(a, w_ref[...], precision=HIGHEST, preferred_element_type=F32) + b_ref[...]


def _adaln(c, w_ada, b_ada):
    B, D = c.shape
    N = w_ada.shape[1]
    tn = D
    return pl.pallas_call(
        _adaln_kernel,
        grid=(N // tn,),
        in_specs=[pl.BlockSpec((B, D), lambda j: (0, 0)),
                  pl.BlockSpec((D, tn), lambda j: (0, j)),
                  pl.BlockSpec((1, tn), lambda j: (0, j))],
        out_specs=pl.BlockSpec((B, tn), lambda j: (0, j)),
        out_shape=jax.ShapeDtypeStruct((B, N), F32),
        compiler_params=_params("arbitrary"),
        name="adaln",
    )(c, w_ada, b_ada.reshape(1, N))


_PROJ_GROUPS = (("gq", GLA_QK, False), ("gk", GLA_QK, False), ("gv", GLA_WIDTH, False),
                ("gg", GLA_WIDTH, False), ("nq", NSA_WIDTH, True), ("kc", NSA_KV_WIDTH, True),
                ("vc", NSA_KV_WIDTH, False), ("ks", NSA_KV_WIDTH, True), ("vs", NSA_KV_WIDTH, False),
                ("kw", NSA_KV_WIDTH, True), ("vw", NSA_KV_WIDTH, False))
_PROJ_W = sum(g[1] for g in _PROJ_GROUPS) + MISC_W


_GROUPED = ("kc", "vc")
GRP_W = CMP_STRIDE * NSA_DH


def _in_proj_kernel(x_ref, mod_ref, g_ref, pos_ref, inv_ref, w_ref, *refs):
    n_out = len(_PROJ_GROUPS) + 1
    out_refs, stage_refs = refs[:n_out], refs[n_out:]
    tm = x_ref.shape[0]
    x = x_ref[...]
    h = _rms(x) * g_ref[...] * (1.0 + mod_ref[1:2, :]) + mod_ref[0:1, :]
    hb = h.astype(BF16)

    ang = pos_ref[...].astype(F32) * inv_ref[...]
    lane = lax.broadcasted_iota(jnp.int32, (1, LANES), 1)
    cos = jnp.cos(ang)
    sin = jnp.sin(ang) * jnp.where(lane < NSA_DH // 2, -1.0, 1.0)
    qs = NSA_DH ** -0.5 * float(np.log2(np.e))

    off = 0
    for (name, width, rot), o_ref in zip(_PROJ_GROUPS, out_refs[:-1]):
        y = jnp.dot(hb, w_ref[:, off:off + width], preferred_element_type=F32)
        grouped = name in _GROUPED
        dst = stage_refs[_GROUPED.index(name)] if grouped else o_ref
        for hd in range(width // NSA_DH if (rot or grouped) else 0):
            yh = y[:, hd * NSA_DH:(hd + 1) * NSA_DH]
            if rot:
                c, s = (cos * qs, sin * qs) if name == "nq" else (cos, sin)
                yh = yh * c + pltpu.roll(yh, NSA_DH // 2, axis=1) * s
            if grouped:
                dst[hd] = yh
            else:
                dst[:, hd * NSA_DH:(hd + 1) * NSA_DH] = yh.astype(dst.dtype)
        if not (rot or grouped):
            dst[...] = y.astype(dst.dtype)
        if grouped:
            for k in range(NSA_KV_HEADS):
                for tok in range(CMP_STRIDE):
                    c0 = k * GRP_W + tok * NSA_DH
                    o_ref[:, c0:c0 + NSA_DH] = dst[k, pl.ds(tok, tm // CMP_STRIDE, stride=CMP_STRIDE), :].astype(
                        o_ref.dtype)
        off += width
    out_refs[-1][...] = jnp.dot(hb, w_ref[:, off:off + MISC_W], preferred_element_type=F32)


def _in_proj(x, mod, norm_g, pos3, inv, w_p, tm):
    B, S, D = x.shape
    row = lambda w: pl.BlockSpec((None, tm, w), lambda b, i: (b, i, 0))
    whole = lambda a: pl.BlockSpec(a.shape, lambda b, i: (0,) * a.ndim)
    out_shape, out_specs = [], []
    for name, w, _ in _PROJ_GROUPS:
        if name in _GROUPED:
            out_shape.append(jax.ShapeDtypeStruct((B, S // CMP_STRIDE, CMP_STRIDE * w), BF16))
            out_specs.append(pl.BlockSpec((None, tm // CMP_STRIDE, CMP_STRIDE * w), lambda b, i: (b, i, 0)))
        else:
            out_shape.append(jax.ShapeDtypeStruct((B, S, w), BF16))
            out_specs.append(row(w))
    out_shape.append(jax.ShapeDtypeStruct((B, S, MISC_W), F32))
    out_specs.append(row(MISC_W))
    return pl.pallas_call(
        _in_proj_kernel,
        grid=(B, S // tm),
        in_specs=[row(D), pl.BlockSpec((None, N_MOD, D), lambda b, i: (b, 0, 0)), whole(norm_g),
                  row(1), whole(inv), whole(w_p)],
        out_specs=out_specs,
        out_shape=out_shape,
        scratch_shapes=[pltpu.VMEM((NSA_KV_HEADS, tm, NSA_DH), F32) for _ in _GROUPED],
        compiler_params=_params("parallel", "parallel"),
        name="in_proj",
    )(x, mod, norm_g, pos3, inv, w_p)


def _split_bf16(x):
    hi = x.astype(BF16)
    return hi, (x - hi.astype(F32)).astype(BF16)


def _log_sigmoid(z):
    return jnp.minimum(z, 0.0) - jnp.log(1.0 + jnp.exp(-jnp.abs(z)))


def _gla_kernel(q_ref, k_ref, v_ref, g_ref, misc_ref, wa_ref, ba_ref, gn_ref, o_ref, st_ref):
    C = GLA_CHUNK
    tg = q_ref.shape[0]

    @pl.when(pl.program_id(1) == 0)
    def _():
        st_ref[...] = jnp.zeros_like(st_ref)

    ts = GLA_SUBTILE
    n_chunk = ts // C
    r = lax.broadcasted_iota(jnp.int32, (ts, ts), 0)
    c = lax.broadcasted_iota(jnp.int32, (ts, ts), 1)
    causal = (r >= c) & (r - c <= (r & (C - 1)))
    tri = jnp.where(causal, 1.0, 0.0).astype(BF16)
    sr = lax.broadcasted_iota(jnp.int32, st_ref.shape, 0)
    sc = lax.broadcasted_iota(jnp.int32, st_ref.shape, 1)
    own_head = (lax.shift_right_logical(sr, GLA_DK.bit_length() - 1)
                == lax.shift_right_logical(sc, GLA_DV.bit_length() - 1))
    w_hi, w_lo = _split_bf16(wa_ref[...])
    st = st_ref[...]

    for t0 in range(0, tg, ts):
        tr = slice(t0, t0 + ts)
        r_hi, r_lo = _split_bf16(misc_ref[tr, :])
        z = (jnp.dot(r_hi, w_hi, preferred_element_type=F32) + jnp.dot(r_lo, w_hi, preferred_element_type=F32)
             + jnp.dot(r_hi, w_lo, preferred_element_type=F32) + ba_ref[...])
        la_hi, la_lo = _split_bf16(_log_sigmoid(z) * (1.0 / GLA_TAU))
        b = (jnp.dot(tri, la_hi, preferred_element_type=F32)
             + jnp.dot(tri, la_lo, preferred_element_type=F32))
        b_lasts = [b[(ci + 1) * C - 1:(ci + 1) * C, :] for ci in range(n_chunk)]
        b_last = jnp.concatenate([jnp.broadcast_to(bl, (C, bl.shape[1])) for bl in b_lasts], axis=0)
        qf = q_ref[tr, :].astype(F32)
        kf = k_ref[tr, :].astype(F32)
        q_in = (qf * GLA_DK ** -0.5 * jnp.exp(b)).astype(BF16)
        k_in = (kf * jnp.exp(-b)).astype(BF16)
        k_dec = (kf * jnp.exp(b_last - b)).astype(BF16)
        pad = jnp.zeros((LANES - n_chunk, b.shape[1]), F32)
        dec_cols = jnp.exp(jnp.concatenate(b_lasts + [pad], axis=0).T)
        gate = g_ref[tr, :].astype(F32)
        gate = gate * jax.nn.sigmoid(gate)

        o_intra = []
        for h in range(GLA_HEADS):
            ks = slice(h * GLA_DK, (h + 1) * GLA_DK)
            att = lax.dot_general(q_in[:, ks], k_in[:, ks], _NT, preferred_element_type=F32)
            att = jnp.where(causal, att, 0.0).astype(BF16)
            o_intra.append(jnp.dot(att, v_ref[tr, h * GLA_DV:(h + 1) * GLA_DV], preferred_element_type=F32))

        o_inter = []
        for ci in range(n_chunk):
            rows = slice(ci * C, (ci + 1) * C)
            vrows = slice(t0 + ci * C, t0 + (ci + 1) * C)
            o_inter.append(jnp.dot(q_in[rows], st.astype(BF16), preferred_element_type=F32))
            d_st = lax.dot_general(k_dec[rows], v_ref[vrows, :], _TN, preferred_element_type=F32)
            st = st * dec_cols[:, ci:ci + 1] + jnp.where(own_head, d_st, 0.0)
        o_inter = jnp.concatenate(o_inter, axis=0)
        for h in range(GLA_HEADS):
            vs = slice(h * GLA_DV, (h + 1) * GLA_DV)
            o = o_intra[h] + o_inter[:, vs]
            o_ref[tr, vs] = (_rms(o) * gn_ref[:, vs] * gate[:, vs]).astype(o_ref.dtype)
    st_ref[...] = st


def _gla(gq, gk, gv, gg, misc, wa_p, ba, gn, tg):
    B, S, _ = gq.shape
    row = lambda w: pl.BlockSpec((None, tg, w), lambda b, i: (b, i, 0))
    whole = lambda a: pl.BlockSpec(a.shape, lambda b, i: (0,) * a.ndim)
    return pl.pallas_call(
        _gla_kernel,
        grid=(B, S // tg),
        in_specs=[row(GLA_QK), row(GLA_QK), row(GLA_WIDTH), row(GLA_WIDTH), row(MISC_W),
                  whole(wa_p), whole(ba), whole(gn)],
        out_specs=row(GLA_WIDTH),
        out_shape=jax.ShapeDtypeStruct((B, S, GLA_WIDTH), BF16),
        scratch_shapes=[pltpu.VMEM((GLA_QK, GLA_WIDTH), F32)],
        compiler_params=_params("parallel", "arbitrary"),
        name="gla",
    )(gq, gk, gv, gg, misc, wa_p, ba, gn)


HALF_TOK = CMP_BLOCK // 2


def _compress_kernel(xk_ref, xv_ref, kw1_ref, kpe_ref, kw2_ref, vw1_ref, vpe_ref, vw2_ref, ok_ref, ov_ref):
    nb, n_grp, _ = xk_ref.shape
    rows = nb * n_grp
    for x_ref, w1_ref, pe_ref, w2_ref, o_ref in ((xk_ref, kw1_ref, kpe_ref, kw2_ref, ok_ref),
                                                 (xv_ref, vw1_ref, vpe_ref, vw2_ref, ov_ref)):
        w1a = w1_ref[0:GRP_W, :]
        w1b = w1_ref[GRP_W:2 * GRP_W, :]
        x = jnp.concatenate([x_ref[:, :, k * GRP_W:(k + 1) * GRP_W].reshape(rows, GRP_W)
                             for k in range(NSA_KV_HEADS)], axis=0)
        bias = (jnp.dot(pe_ref[:, 0:GRP_W], w1a, preferred_element_type=F32)[0:1]
                + jnp.dot(pe_ref[:, GRP_W:2 * GRP_W], w1b, preferred_element_type=F32)[0:1])
        ua = jnp.dot(x, w1a, preferred_element_type=F32)
        ub = jnp.dot(x, w1b, preferred_element_type=F32)
        hid = ua + pltpu.roll(ub, NSA_KV_HEADS * rows - 1, axis=0) + bias
        act = jax.nn.gelu(hid, approximate=True).astype(BF16)
        y = jnp.dot(act, w2_ref[...], preferred_element_type=F32).astype(o_ref.dtype)
        for k in range(NSA_KV_HEADS):
            o_ref[:, :, k * NSA_DH:(k + 1) * NSA_DH] = y[k * rows:(k + 1) * rows].reshape(nb, n_grp, NSA_DH)


def _compress(xk, xv, kw1, kpe, kw2, vw1, vpe, vw2, nb):
    B, n_grp, gw = xk.shape
    W = NSA_KV_WIDTH
    whole = lambda a: pl.BlockSpec(a.shape, lambda b: (0,) * a.ndim)
    xspec = pl.BlockSpec((nb, n_grp, gw), lambda b: (b, 0, 0))
    ospec = pl.BlockSpec((nb, n_grp, W), lambda b: (b, 0, 0))
    return pl.pallas_call(
        _compress_kernel,
        grid=(B // nb,),
        in_specs=[xspec, xspec, whole(kw1), whole(kpe), whole(kw2), whole(vw1), whole(vpe), whole(vw2)],
        out_specs=[ospec, ospec],
        out_shape=[jax.ShapeDtypeStruct((B, n_grp, W), BF16)] * 2,
        compiler_params=_params("parallel"),
        name="compress",
    )(xk, xv, kw1, kpe, kw2, vw1, vpe, vw2)


def _compress_pe(pe):
    return jnp.zeros((8, 2 * GRP_W), F32).at[0].set(pe.reshape(2 * GRP_W)).astype(BF16)


def _softmax_start(s, v):
    m = jnp.max(s, axis=-1, keepdims=True)
    p = jnp.exp(s - m)
    return m, jnp.sum(p, axis=-1, keepdims=True), jnp.dot(p.astype(BF16), v, preferred_element_type=F32)


def _softmax_step(carry, s, v):
    m, l, acc = carry
    m_new = jnp.maximum(m, jnp.max(s, axis=-1, keepdims=True))
    a = jnp.exp(m - m_new)
    p = jnp.exp(s - m_new)
    return (m_new, a * l + jnp.sum(p, axis=-1, keepdims=True),
            a * acc + jnp.dot(p.astype(BF16), v, preferred_element_type=F32))


def _nsa_kernel(q_ref, kc_ref, vc_ref, ks_ref, vs_ref, kw_ref, vw_ref, misc_ref, gn_ref, o_ref, kaug_ref):
    tq = q_ref.shape[0]
    S = ks_ref.shape[0]
    n_cmp = kc_ref.shape[0]
    n_sel = S // SEL_BLOCK
    G = NSA_HEADS // NSA_KV_HEADS
    i = pl.program_id(1)
    q0 = pl.multiple_of(i * tq, tq)

    @pl.when(i == 0)
    def _():
        pos = lax.broadcasted_iota(jnp.int32, (S, LANES), 0)
        lane = lax.broadcasted_iota(jnp.int32, (S, LANES), 1)
        onehot = jnp.where(lax.shift_right_logical(pos, 6) == lane, 1.0, 0.0).astype(BF16)
        for k in range(NSA_KV_HEADS):
            kaug_ref[k, :, 0:NSA_DH] = ks_ref[:, k * NSA_DH:(k + 1) * NSA_DH]
            kaug_ref[k, :, NSA_DH:2 * NSA_DH] = onehot

    t = q0 + lax.broadcasted_iota(jnp.int32, (tq, 1), 0)
    tg = jnp.concatenate([t] * G, axis=0)
    lane = lax.broadcasted_iota(jnp.int32, (1, LANES), 1)
    rr = lax.broadcasted_iota(jnp.int32, (tq, tq), 0)
    cc = lax.broadcasted_iota(jnp.int32, (tq, tq), 1)
    r_minus_c = jnp.concatenate([rr - cc] * G, axis=0)
    diag_ok = r_minus_c >= 0

    cmp_end = jnp.where(lane < n_cmp - 1, lane * CMP_STRIDE + (CMP_BLOCK - 1), jnp.int32(2 ** 30))
    cmp_valid = cmp_end <= tg
    any_valid = (tg >= CMP_BLOCK - 1).astype(F32)
    jj = lax.broadcasted_iota(jnp.int32, (n_sel, n_cmp), 0) * SEL_BLOCK
    nn = lax.broadcasted_iota(jnp.int32, (n_sel, n_cmp), 1) * CMP_STRIDE
    ov_t = jnp.maximum(jnp.minimum(nn + CMP_BLOCK, jj + SEL_BLOCK) - jnp.maximum(nn, jj), 0)
    ov_t = (ov_t.astype(F32) * (1.0 / CMP_BLOCK)).astype(BF16)
    j_row = lax.broadcasted_iota(jnp.int32, (n_sel, tq), 0)
    blk_t = lax.shift_right_logical(q0 + lax.broadcasted_iota(jnp.int32, (n_sel, tq), 1), 6)
    forced = (j_row == 0) | (j_row == blk_t) | (j_row == blk_t - 1)
    in_past = j_row <= blk_t
    gates = jax.nn.sigmoid(misc_ref[...])

    q2s, q2as, o_cmps = [], [], []
    for k in range(NSA_KV_HEADS):
        hs = slice(k * NSA_DH, (k + 1) * NSA_DH)
        q2 = jnp.concatenate([q_ref[:, (k * G + g) * NSA_DH:(k * G + g + 1) * NSA_DH] for g in range(G)], axis=0)

        s = lax.dot_general(q2, kc_ref[:, hs], _NT, preferred_element_type=F32)
        s = jnp.where(cmp_valid, s, NEG)
        e = jnp.exp(s - jnp.max(s, axis=-1, keepdims=True))
        p = e / jnp.sum(e, axis=-1, keepdims=True) * any_valid
        o_cmps.append(jnp.dot(p.astype(BF16), vc_ref[:, hs], preferred_element_type=F32))

        p_grp = p[0:tq]
        for g in range(1, G):
            p_grp = p_grp + p[g * tq:(g + 1) * tq]
        p_hi = p_grp.astype(BF16)
        p_lo = (p_grp - p_hi.astype(F32)).astype(BF16)
        imp = (lax.dot_general(ov_t, p_hi, _NT, preferred_element_type=F32)
               + lax.dot_general(ov_t, p_lo, _NT, preferred_element_type=F32))
        imp = jnp.where(forced, BIG, jnp.where(in_past, imp, NEG))
        rank = jnp.zeros((n_sel, tq), F32)
        for j in range(n_sel):
            row = imp[j:j + 1, :]
            rank = rank + jnp.where(j_row > j, jnp.where(row >= imp, 1.0, 0.0), jnp.where(row > imp, 1.0, 0.0))
        bias_t = jnp.where(rank < float(min(SEL_TOPK, n_sel)), 0.0, NEG)
        bias_t = jnp.concatenate([bias_t, jnp.zeros((LANES - n_sel, tq), F32)], axis=0)
        bias = bias_t.T.astype(BF16)
        q2s.append(q2)
        q2as.append(jnp.concatenate([q2, jnp.concatenate([bias] * G, axis=0)], axis=1))

    sel, win = [], []
    for k in range(NSA_KV_HEADS):
        hs = slice(k * NSA_DH, (k + 1) * NSA_DH)
        s = lax.dot_general(q2as[k], kaug_ref[k, pl.ds(q0, tq), :], _NT, preferred_element_type=F32)
        sel.append(_softmax_start(jnp.where(diag_ok, s, NEG), vs_ref[pl.ds(q0, tq), hs]))
        s = lax.dot_general(q2s[k], kw_ref[pl.ds(q0, tq), hs], _NT, preferred_element_type=F32)
        win.append(_softmax_start(jnp.where(diag_ok, s, NEG), vw_ref[pl.ds(q0, tq), hs]))

    def sel_body(kt, carry):
        k0 = pl.multiple_of(kt * tq, tq)
        out = []
        for k in range(NSA_KV_HEADS):
            s = lax.dot_general(q2as[k], kaug_ref[k, pl.ds(k0, tq), :], _NT, preferred_element_type=F32)
            out.append(_softmax_step(carry[k], s, vs_ref[pl.ds(k0, tq), k * NSA_DH:(k + 1) * NSA_DH]))
        return tuple(out)

    sel = lax.fori_loop(0, i, sel_body, tuple(sel))

    def win_body(kt, carry):
        k0 = pl.multiple_of(kt * tq, tq)
        far = r_minus_c + (i - kt) * tq >= WINDOW
        out = []
        for k in range(NSA_KV_HEADS):
            hs = slice(k * NSA_DH, (k + 1) * NSA_DH)
            s = lax.dot_general(q2s[k], kw_ref[pl.ds(k0, tq), hs], _NT, preferred_element_type=F32)
            out.append(_softmax_step(carry[k], jnp.where(far, NEG, s), vw_ref[pl.ds(k0, tq), hs]))
        return tuple(out)

    win = lax.fori_loop(jnp.maximum(i - (WINDOW + tq - 1) // tq, 0), i, win_body, tuple(win))

    heads = []
    for k in range(NSA_KV_HEADS):
        o_sel = sel[k][2] / sel[k][1]
        o_win = win[k][2] / win[k][1]
        for g in range(G):
            gs = slice(g * tq, (g + 1) * tq)
            gl = GATE_OFF + (k * G + g) * NSA_BRANCHES
            heads.append(gates[:, gl:gl + 1] * o_cmps[k][gs] + gates[:, gl + 1:gl + 2] * o_sel[gs]
                         + gates[:, gl + 2:gl + 3] * o_win[gs])

    o = jnp.concatenate(heads, axis=1)
    o_ref[...] = (_rms(o) * gn_ref[...]).astype(o_ref.dtype)


def _nsa(nq, kcmp, vcmp, ks, vs, kw, vw, misc, gn, tq):
    B, S, _ = nq.shape
    n_cmp = kcmp.shape[1]
    assert n_cmp <= LANES and S // SEL_BLOCK <= LANES and SEL_BLOCK == 64
    row = lambda w: pl.BlockSpec((None, tq, w), lambda b, i: (b, i, 0))
    seq = lambda n: pl.BlockSpec((None, n, NSA_KV_WIDTH), lambda b, i: (b, 0, 0))
    return pl.pallas_call(
        _nsa_kernel,
        grid=(B, S // tq),
        in_specs=[row(NSA_WIDTH), seq(n_cmp), seq(n_cmp), seq(S), seq(S), seq(S), seq(S), row(MISC_W),
                  pl.BlockSpec(gn.shape, lambda b, i: (0, 0))],
        out_specs=row(NSA_WIDTH),
        out_shape=jax.ShapeDtypeStruct((B, S, NSA_WIDTH), BF16),
        scratch_shapes=[pltpu.VMEM((NSA_KV_HEADS, S, 2 * NSA_DH), BF16)],
        compiler_params=_params("parallel", "arbitrary"),
        name="nsa",
    )(nq, kcmp, vcmp, ks, vs, kw, vw, misc, gn)


def _t_f32(x):
    return x.astype(F32).T


def _col_softmax_start(s, vt):
    m = jnp.max(s, axis=0, keepdims=True)
    p = jnp.exp(s - m)
    return m, jnp.sum(p, axis=0, keepdims=True), jnp.dot(vt, p.astype(BF16), preferred_element_type=F32)


def _col_softmax_step(carry, s, vt):
    m, l, acc = carry
    m_new = jnp.maximum(m, jnp.max(s, axis=0, keepdims=True))
    a = jnp.exp(m - m_new)
    p = jnp.exp(s - m_new)
    return (m_new, a * l + jnp.sum(p, axis=0, keepdims=True),
            a * acc + jnp.dot(vt, p.astype(BF16), preferred_element_type=F32))


def _nsa_t_kernel(q_ref, kc_ref, vc_ref, ks_ref, vs_ref, kw_ref, vw_ref, misc_ref, gn_ref, o_ref,
                  kaug_ref, vst_ref, vwt_ref):
    tq = q_ref.shape[0]
    tk = vst_ref.shape[3]
    S = ks_ref.shape[0]
    n_cmp = kc_ref.shape[0]
    n_sel = S // SEL_BLOCK
    G = NSA_HEADS // NSA_KV_HEADS
    M = G * tq
    i = pl.program_id(1)
    q0 = pl.multiple_of(i * tq, tq)

    @pl.when(i == 0)
    def _():
        pos = lax.broadcasted_iota(jnp.int32, (S, LANES), 0)
        lane = lax.broadcasted_iota(jnp.int32, (S, LANES), 1)
        onehot = jnp.where(lax.shift_right_logical(pos, 6) == lane, 1.0, 0.0).astype(BF16)
        for k in range(NSA_KV_HEADS):
            hs = slice(k * NSA_DH, (k + 1) * NSA_DH)
            kaug_ref[k, :, 0:NSA_DH] = ks_ref[:, hs]
            kaug_ref[k, :, NSA_DH:2 * NSA_DH] = onehot
            for kt in range(S // tk):
                rows = slice(kt * tk, (kt + 1) * tk)
                vst_ref[k, kt] = _t_f32(vs_ref[rows, hs]).astype(BF16)
                vwt_ref[k, kt] = _t_f32(vw_ref[rows, hs]).astype(BF16)

    t_q = q0 + lax.broadcasted_iota(jnp.int32, (1, tq), 1)
    t_m = jnp.concatenate([t_q] * G, axis=1)
    key_off = lax.broadcasted_iota(jnp.int32, (tk, 1), 0)

    n_col = lax.broadcasted_iota(jnp.int32, (n_cmp, 1), 0)
    cmp_end = jnp.where(n_col < n_cmp - 1, n_col * CMP_STRIDE + (CMP_BLOCK - 1), jnp.int32(2 ** 30))
    cmp_valid = cmp_end <= t_m
    any_valid = (t_m >= CMP_BLOCK - 1).astype(F32)
    jj = lax.broadcasted_iota(jnp.int32, (n_sel, n_cmp), 0) * SEL_BLOCK
    nn = lax.broadcasted_iota(jnp.int32, (n_sel, n_cmp), 1) * CMP_STRIDE
    ov_t = jnp.maximum(jnp.minimum(nn + CMP_BLOCK, jj + SEL_BLOCK) - jnp.maximum(nn, jj), 0)
    ov_t = (ov_t.astype(F32) * (1.0 / CMP_BLOCK)).astype(BF16)
    j_row = lax.broadcasted_iota(jnp.int32, (n_sel, tq), 0)
    blk_t = lax.shift_right_logical(q0 + lax.broadcasted_iota(jnp.int32, (n_sel, tq), 1), 6)
    forced = (j_row == 0) | (j_row == blk_t) | (j_row == blk_t - 1)
    in_past = j_row <= blk_t

    qts, qtas, o_cmps = [], [], []
    for k in range(NSA_KV_HEADS):
        hs = slice(k * NSA_DH, (k + 1) * NSA_DH)
        qt = jnp.concatenate([_t_f32(q_ref[:, (k * G + g) * NSA_DH:(k * G + g + 1) * NSA_DH])
                              for g in range(G)], axis=1).astype(BF16)

        s = jnp.dot(kc_ref[:, hs], qt, preferred_element_type=F32)
        s = jnp.where(cmp_valid, s, NEG)
        e = jnp.exp(s - jnp.max(s, axis=0, keepdims=True))
        p = e / jnp.sum(e, axis=0, keepdims=True) * any_valid
        o_cmps.append(jnp.dot(_t_f32(vc_ref[:, hs]).astype(BF16), p.astype(BF16), preferred_element_type=F32))

        p_grp = p[:, 0:tq]
        for g in range(1, G):
            p_grp = p_grp + p[:, g * tq:(g + 1) * tq]
        p_hi = p_grp.astype(BF16)
        p_lo = (p_grp - p_hi.astype(F32)).astype(BF16)
        imp = (jnp.dot(ov_t, p_hi, preferred_element_type=F32)
               + jnp.dot(ov_t, p_lo, preferred_element_type=F32))
        imp = jnp.where(forced, BIG, jnp.where(in_past, imp, NEG))
        rank = jnp.zeros((n_sel, tq), F32)
        for j in range(n_sel):
            row = imp[j:j + 1, :]
            rank = rank + jnp.where(j_row > j, jnp.where(row >= imp, 1.0, 0.0), jnp.where(row > imp, 1.0, 0.0))
        bias = jnp.where(rank < float(min(SEL_TOPK, n_sel)), 0.0, NEG).astype(BF16)
        qts.append(qt)
        qtas.append(jnp.concatenate([qt, jnp.concatenate([bias] * G, axis=1),
                                     jnp.zeros((NSA_DH - n_sel, M), BF16)], axis=0))

    n_diag = tq // tk
    kt0 = i * n_diag
    sel, win = [], []
    for k in range(NSA_KV_HEADS):
        hs = slice(k * NSA_DH, (k + 1) * NSA_DH)
        for d in range(n_diag):
            k0 = pl.multiple_of(q0 + d * tk, tk)
            ahead = k0 + key_off > t_m
            s = jnp.dot(kaug_ref[k, pl.ds(k0, tk), :], qtas[k], preferred_element_type=F32)
            s = jnp.where(ahead, NEG, s)
            sel_k = (_col_softmax_start(s, vst_ref[k, kt0 + d]) if d == 0
                     else _col_softmax_step(sel_k, s, vst_ref[k, kt0 + d]))
            s = jnp.dot(kw_ref[pl.ds(k0, tk), hs], qts[k], preferred_element_type=F32)
            s = jnp.where(ahead, NEG, s)
            win_k = (_col_softmax_start(s, vwt_ref[k, kt0 + d]) if d == 0
                     else _col_softmax_step(win_k, s, vwt_ref[k, kt0 + d]))
        sel.append(sel_k)
        win.append(win_k)

    def sel_body(kt, carry):
        k0 = pl.multiple_of(kt * tk, tk)
        out = []
        for k in range(NSA_KV_HEADS):
            s = jnp.dot(kaug_ref[k, pl.ds(k0, tk), :], qtas[k], preferred_element_type=F32)
            out.append(_col_softmax_step(carry[k], s, vst_ref[k, kt]))
        return tuple(out)

    sel = lax.fori_loop(0, kt0, sel_body, tuple(sel))

    def win_body(kt, carry):
        k0 = pl.multiple_of(kt * tk, tk)
        far = t_m - (k0 + key_off) >= WINDOW
        out = []
        for k in range(NSA_KV_HEADS):
            hs = slice(k * NSA_DH, (k + 1) * NSA_DH)
            s = jnp.dot(kw_ref[pl.ds(k0, tk), hs], qts[k], preferred_element_type=F32)
            out.append(_col_softmax_step(carry[k], jnp.where(far, NEG, s), vwt_ref[k, kt]))
        return tuple(out)

    win = lax.fori_loop(jnp.maximum(kt0 - (WINDOW + tk - 1) // tk, 0), kt0, win_body, tuple(win))

    gates_t = _t_f32(jax.nn.sigmoid(misc_ref[...]))
    heads = []
    for k in range(NSA_KV_HEADS):
        o_sel = sel[k][2] / sel[k][1]
        o_win = win[k][2] / win[k][1]
        for g in range(G):
            gs = slice(g * tq, (g + 1) * tq)
            gl = GATE_OFF + (k * G + g) * NSA_BRANCHES
            heads.append(gates_t[gl:gl + 1, :] * o_cmps[k][:, gs] + gates_t[gl + 1:gl + 2, :] * o_sel[:, gs]
                         + gates_t[gl + 2:gl + 3, :] * o_win[:, gs])
    ssq = heads[0] * heads[0]
    for o in heads[1:]:
        ssq = ssq + o * o
    inv = lax.rsqrt(jnp.sum(ssq, axis=0, keepdims=True) * (1.0 / NSA_WIDTH) + EPS)
    for hq, o in enumerate(heads):
        cs = slice(hq * NSA_DH, (hq + 1) * NSA_DH)
        o_ref[:, cs] = ((o * inv).T * gn_ref[:, cs]).astype(o_ref.dtype)


def _nsa_t(nq, kcmp, vcmp, ks, vs, kw, vw, misc, gn, tq, tk):
    B, S, _ = nq.shape
    n_cmp = kcmp.shape[1]
    assert n_cmp <= LANES and S // SEL_BLOCK <= NSA_DH and SEL_BLOCK == 64 and tq % tk == 0
    row = lambda w: pl.BlockSpec((None, tq, w), lambda b, i: (b, i, 0))
    seq = lambda n: pl.BlockSpec((None, n, NSA_KV_WIDTH), lambda b, i: (b, 0, 0))
    return pl.pallas_call(
        _nsa_t_kernel,
        grid=(B, S // tq),
        in_specs=[row(NSA_WIDTH), seq(n_cmp), seq(n_cmp), seq(S), seq(S), seq(S), seq(S), row(MISC_W),
                  pl.BlockSpec(gn.shape, lambda b, i: (0, 0))],
        out_specs=row(NSA_WIDTH),
        out_shape=jax.ShapeDtypeStruct((B, S, NSA_WIDTH), BF16),
        scratch_shapes=[pltpu.VMEM((NSA_KV_HEADS, S, 2 * NSA_DH), BF16),
                        pltpu.VMEM((NSA_KV_HEADS, S // tk, NSA_DH, tk), BF16),
                        pltpu.VMEM((NSA_KV_HEADS, S // tk, NSA_DH, tk), BF16)],
        compiler_params=_params("parallel", "arbitrary"),
        name="nsa",
    )(nq, kcmp, vcmp, ks, vs, kw, vw, misc, gn)


V_AUG = NSA_DH + 16


def _attend(k_rows, qt, vt_aug, masked):
    s = jnp.dot(k_rows, qt, preferred_element_type=F32)
    if masked is not None:
        s = jnp.where(masked, NEG, s)
    m = jnp.max(s, axis=0, keepdims=True)
    p = jnp.exp2((s - m).astype(BF16))
    return m, jnp.dot(vt_aug, p, preferred_element_type=F32)


def _merge(parts):
    m_all = parts[0][0]
    for m, _ in parts[1:]:
        m_all = jnp.maximum(m_all, m)
    tot = None
    for m, acc in parts:
        w = acc * jnp.exp2(m - m_all)
        tot = w if tot is None else tot + w
    return tot[0:NSA_DH] / tot[NSA_DH:NSA_DH + 1]


def _nsa_step(c, tq, tk, q_ref, kc_ref, vc_ref, kw_ref, misc_ref, gn_ref, o_ref, kaug_ref, vst_ref, vwt_ref):
    S = kw_ref.shape[0]
    n_cmp = kc_ref.shape[0]
    n_sel = S // SEL_BLOCK
    G = NSA_HEADS // NSA_KV_HEADS
    M = G * tq
    q0 = c * tq

    t_q = q0 + lax.broadcasted_iota(jnp.int32, (1, tq), 1)
    t_m = jnp.concatenate([t_q] * G, axis=1)
    key_off = lax.broadcasted_iota(jnp.int32, (tk, 1), 0)

    n_col = lax.broadcasted_iota(jnp.int32, (n_cmp, 1), 0)
    cmp_end = jnp.where(n_col < n_cmp - 1, n_col * CMP_STRIDE + (CMP_BLOCK - 1), jnp.int32(2 ** 30))
    cmp_valid = cmp_end <= t_m
    any_valid = (t_m >= CMP_BLOCK - 1).astype(F32)
    jj = lax.broadcasted_iota(jnp.int32, (n_sel, n_cmp), 0) * SEL_BLOCK
    nn = lax.broadcasted_iota(jnp.int32, (n_sel, n_cmp), 1) * CMP_STRIDE
    ov_t = jnp.maximum(jnp.minimum(nn + CMP_BLOCK, jj + SEL_BLOCK) - jnp.maximum(nn, jj), 0)
    ov_t = (ov_t.astype(F32) * (1.0 / CMP_BLOCK)).astype(BF16)
    j_row = lax.broadcasted_iota(jnp.int32, (n_sel, tq), 0)
    blk_t = lax.shift_right_logical(q0 + lax.broadcasted_iota(jnp.int32, (n_sel, tq), 1), 6)
    forced = (j_row == 0) | (j_row == blk_t) | (j_row == blk_t - 1)
    in_past = j_row <= blk_t

    o_cmp, o_sel, o_win = [], [], []
    for k in range(NSA_KV_HEADS):
        hs = slice(k * NSA_DH, (k + 1) * NSA_DH)
        qt = jnp.concatenate([_t_f32(q_ref[:, (k * G + g) * NSA_DH:(k * G + g + 1) * NSA_DH])
                              for g in range(G)], axis=1).astype(BF16)

        s = jnp.dot(kc_ref[:, hs], qt, preferred_element_type=F32)
        s = jnp.where(cmp_valid, s, NEG)
        e = jnp.exp2(s - jnp.max(s, axis=0, keepdims=True))
        p = e / jnp.sum(e, axis=0, keepdims=True) * any_valid
        o_cmp.append(jnp.dot(_t_f32(vc_ref[:, hs]).astype(BF16), p.astype(BF16), preferred_element_type=F32))

        p_grp = p[:, 0:tq]
        for g in range(1, G):
            p_grp = p_grp + p[:, g * tq:(g + 1) * tq]
        p_hi = p_grp.astype(BF16)
        p_lo = (p_grp - p_hi.astype(F32)).astype(BF16)
        imp = (jnp.dot(ov_t, p_hi, preferred_element_type=F32)
               + jnp.dot(ov_t, p_lo, preferred_element_type=F32))
        imp = jnp.where(forced, BIG, jnp.where(in_past, imp, NEG))
        rank = jnp.zeros((n_sel, tq), F32)
        for j in range(n_sel):
            row = imp[j:j + 1, :]
            rank = rank + jnp.where(j_row > j, jnp.where(row >= imp, 1.0, 0.0), jnp.where(row > imp, 1.0, 0.0))
        bias = jnp.where(rank < float(min(SEL_TOPK, n_sel)), 0.0, NEG).astype(BF16)
        qta = jnp.concatenate([qt, jnp.concatenate([bias] * G, axis=1),
                               jnp.zeros((NSA_DH - n_sel, M), BF16)], axis=0)

        parts = []
        for kt in range((q0 + tq) // tk):
            k0 = kt * tk
            ahead = (k0 + key_off > t_m) if k0 + tk > q0 else None
            parts.append(_attend(kaug_ref[k, k0:k0 + tk, :], qta, vst_ref[k, kt], ahead))
        o_sel.append(_merge(parts))

        parts = []
        for kt in range(max(q0 - WINDOW, 0) // tk, (q0 + tq) // tk):
            k0 = kt * tk
            if k0 + tk > q0:
                masked = k0 + key_off > t_m
            else:
                masked = t_m - (k0 + key_off) >= WINDOW
            parts.append(_attend(kw_ref[k0:k0 + tk, hs], qt, vwt_ref[k, kt], masked))
        o_win.append(_merge(parts))

    gates_t = _t_f32(jax.nn.sigmoid(misc_ref[...]))
    heads = []
    for k in range(NSA_KV_HEADS):
        for g in range(G):
            gs = slice(g * tq, (g + 1) * tq)
            gl = GATE_OFF + (k * G + g) * NSA_BRANCHES
            heads.append(gates_t[gl:gl + 1, :] * o_cmp[k][:, gs] + gates_t[gl + 1:gl + 2, :] * o_sel[k][:, gs]
                         + gates_t[gl + 2:gl + 3, :] * o_win[k][:, gs])
    ssq = heads[0] * heads[0]
    for o in heads[1:]:
        ssq = ssq + o * o
    inv = lax.rsqrt(jnp.sum(ssq, axis=0, keepdims=True) * (1.0 / NSA_WIDTH) + EPS)
    for hq, o in enumerate(heads):
        cs = slice(hq * NSA_DH, (hq + 1) * NSA_DH)
        o_ref[:, cs] = ((o * inv).T * gn_ref[:, cs]).astype(o_ref.dtype)


def _nsa_s_kernel(q_ref, kc_ref, vc_ref, ks_ref, vs_ref, kw_ref, vw_ref, misc_ref, gn_ref, o_ref,
                  kaug_ref, vst_ref, vwt_ref):
    tq = q_ref.shape[0]
    tk = vst_ref.shape[3]
    S = ks_ref.shape[0]
    i = pl.program_id(1)

    @pl.when(i == 0)
    def _():
        pos = lax.broadcasted_iota(jnp.int32, (S, LANES), 0)
        lane = lax.broadcasted_iota(jnp.int32, (S, LANES), 1)
        onehot = jnp.where(lax.shift_right_logical(pos, 6) == lane, 1.0, 0.0).astype(BF16)
        row = lax.broadcasted_iota(jnp.int32, (V_AUG - NSA_DH, tk), 0)
        ones_rows = jnp.where(row == 0, 1.0, 0.0).astype(BF16)
        for k in range(NSA_KV_HEADS):
            hs = slice(k * NSA_DH, (k + 1) * NSA_DH)
            kaug_ref[k, :, 0:NSA_DH] = ks_ref[:, hs]
            kaug_ref[k, :, NSA_DH:2 * NSA_DH] = onehot
            for kt in range(S // tk):
                rows = slice(kt * tk, (kt + 1) * tk)
                for src, dst in ((vs_ref, vst_ref), (vw_ref, vwt_ref)):
                    dst[k, kt, 0:NSA_DH, :] = _t_f32(src[rows, hs]).astype(BF16)
                    dst[k, kt, NSA_DH:V_AUG, :] = ones_rows

    for c in range(S // tq):
        pl.when(i == c)(functools.partial(_nsa_step, c, tq, tk, q_ref, kc_ref, vc_ref, kw_ref, misc_ref,
                                          gn_ref, o_ref, kaug_ref, vst_ref, vwt_ref))


def _nsa_s(nq, kcmp, vcmp, ks, vs, kw, vw, misc, gn, tq, tk):
    B, S, _ = nq.shape
    n_cmp = kcmp.shape[1]
    assert n_cmp <= LANES and S // SEL_BLOCK <= NSA_DH and SEL_BLOCK == 64
    assert tq % tk == 0 and tq <= WINDOW and WINDOW % tk == 0
    row = lambda w: pl.BlockSpec((None, tq, w), lambda b, i: (b, i, 0))
    seq = lambda n: pl.BlockSpec((None, n, NSA_KV_WIDTH), lambda b, i: (b, 0, 0))
    vt_scratch = pltpu.VMEM((NSA_KV_HEADS, S // tk, V_AUG, tk), BF16)
    return pl.pallas_call(
        _nsa_s_kernel,
        grid=(B, S // tq),
        in_specs=[row(NSA_WIDTH), seq(n_cmp), seq(n_cmp), seq(S), seq(S), seq(S), seq(S), row(MISC_W),
                  pl.BlockSpec(gn.shape, lambda b, i: (0, 0))],
        out_specs=row(NSA_WIDTH),
        out_shape=jax.ShapeDtypeStruct((B, S, NSA_WIDTH), BF16),
        scratch_shapes=[pltpu.VMEM((NSA_KV_HEADS, S, 2 * NSA_DH), BF16), vt_scratch, vt_scratch],
        compiler_params=_params("parallel", "arbitrary"),
        name="nsa",
    )(nq, kcmp, vcmp, ks, vs, kw, vw, misc, gn)


def _out_ffn_kernel(x_ref, og_ref, on_ref, mod_ref, wo_ref, g2_ref, w1_ref, w2_ref, gf_ref, o_ref, *, ff_chunk, final):
    wg = og_ref.shape[1]
    mix = (jnp.dot(og_ref[...], wo_ref[0:wg, :], preferred_element_type=F32)
           + jnp.dot(on_ref[...], wo_ref[wg:, :], preferred_element_type=F32))
    x1 = x_ref[...] + mod_ref[2:3, :] * mix
    h = (_rms(x1) * g2_ref[...] * (1.0 + mod_ref[4:5, :]) + mod_ref[3:4, :]).astype(BF16)
    acc = jnp.zeros_like(x1)
    for c in range(w1_ref.shape[1] // ff_chunk):
        cs = slice(c * ff_chunk, (c + 1) * ff_chunk)
        a = jnp.maximum(jnp.dot(h, w1_ref[:, cs], preferred_element_type=F32), 0.0)
        acc = acc + jnp.dot((a * a).astype(BF16), w2_ref[cs, :], preferred_element_type=F32)
    x2 = x1 + mod_ref[5:6, :] * acc
    o_ref[...] = _rms(x2) * gf_ref[...] if final else x2


def _out_ffn(x, og, on, mod, wo, g2, w1, w2, gf, tm, final):
    B, S, D = x.shape
    row = lambda w: pl.BlockSpec((None, tm, w), lambda b, i: (b, i, 0))
    whole = lambda a: pl.BlockSpec(a.shape, lambda b, i: (0,) * a.ndim)
    return pl.pallas_call(
        functools.partial(_out_ffn_kernel, ff_chunk=1024, final=final),
        grid=(B, S // tm),
        in_specs=[row(D), row(og.shape[2]), row(on.shape[2]),
                  pl.BlockSpec((None, N_MOD, D), lambda b, i: (b, 0, 0)),
                  whole(wo), whole(g2), whole(w1), whole(w2), whole(gf)],
        out_specs=row(D),
        out_shape=jax.ShapeDtypeStruct((B, S, D), F32),
        compiler_params=_params("parallel", "parallel"),
        name="out_ffn",
    )(x, og, on, mod, wo, g2, w1, w2, gf)


def _reorder_w_in(w_in):
    cuts = [int(v) for v in np.cumsum(IN_SPLITS)[:-1]]
    gq, gk, gv, gg, gr, nq, kc, vc, ks, vs, kw, vw, ngate = jnp.split(w_in, cuts, axis=1)
    pad = jnp.zeros((w_in.shape[0], MISC_W - GLA_RANK - N_GATE), w_in.dtype)
    return jnp.concatenate([gq, gk, gv, gg, nq, kc, vc, ks, vs, kw, vw, gr, ngate, pad], axis=1).astype(BF16)


def kernel(x, c, positions, w_ada, b_ada, norm1_g, w_in, gla_w_a2, gla_b_a, gla_norm_g, nsa_pe_k, nsa_pe_v, cmp_k_w1, cmp_k_w2, cmp_v_w1, cmp_v_w2, nsa_norm_g, w_o, norm2_g, w_ff1, w_ff2, final_norm_g):
    B, S, D = x.shape
    depth = w_in.shape[0]
    half = NSA_DH // 2
    inv = ROPE_THETA ** (-jnp.arange(half, dtype=F32) / half)
    inv = jnp.concatenate([inv, inv]).reshape(1, NSA_DH)
    pos3 = positions.reshape(B, S, 1)
    for l in range(depth):
        mod = _adaln(c, w_ada[l], b_ada[l]).reshape(B, N_MOD, D)
        (gq, gk, gv, gg, nq, kc, vc, ks, vs, kw, vw, misc) = _in_proj(
            x, mod, norm1_g[l].reshape(1, D), pos3, inv, _reorder_w_in(w_in[l]), tm=512)

        wa_p = jnp.zeros((MISC_W, GLA_QK), F32).at[0:GLA_RANK].set(gla_w_a2[l])
        o_gla = _gla(gq, gk, gv, gg, misc, wa_p, gla_b_a[l].reshape(1, GLA_QK),
                     gla_norm_g[l].reshape(1, GLA_WIDTH), tg=512)

        k_cmp, v_cmp = _compress(kc, vc, cmp_k_w1[l].astype(BF16), _compress_pe(nsa_pe_k[l]),
                                 cmp_k_w2[l].astype(BF16), cmp_v_w1[l].astype(BF16),
                                 _compress_pe(nsa_pe_v[l]), cmp_v_w2[l].astype(BF16), nb=int(np.gcd(B, 4)))
        o_nsa = _nsa_s(nq, k_cmp, v_cmp, ks, vs, kw, vw, misc, nsa_norm_g[l].reshape(1, NSA_WIDTH),
                       tq=512, tk=512)

        x = _out_ffn(x, o_gla, o_nsa, mod, w_o[l].astype(BF16), norm2_g[l].reshape(1, D),
                     w_ff1[l].astype(BF16), w_ff2[l].astype(BF16), final_norm_g.reshape(1, D),
                     tm=256, final=(l == depth - 1))
    return x
```

```python
import functools

import numpy as np
import jax
import jax.numpy as jnp
from jax import lax
from jax.experimental import pallas as pl
from jax.experimental.pallas import tpu as pltpu

GLA_HEADS = 4
GLA_DK = 64
GLA_DV = 128
GLA_RANK = 16
GLA_TAU = 16.0
GLA_CHUNK = 64
NSA_HEADS = 4
NSA_KV_HEADS = 2
NSA_DH = 128
NSA_BRANCHES = 3
CMP_BLOCK = 32
CMP_STRIDE = 16
CMP_HIDDEN = 256
SEL_BLOCK = 64
SEL_TOPK = 16
WINDOW = 512
N_MOD = 6
ROPE_THETA = 10000.0
EPS = 1e-6
NEG = -1e30
BIG = 1e30

GLA_QK = GLA_HEADS * GLA_DK
GLA_WIDTH = GLA_HEADS * GLA_DV
NSA_WIDTH = NSA_HEADS * NSA_DH
NSA_KV_WIDTH = NSA_KV_HEADS * NSA_DH
N_GATE = NSA_HEADS * NSA_BRANCHES
IN_SPLITS = (GLA_QK, GLA_QK, GLA_WIDTH, GLA_WIDTH, GLA_RANK, NSA_WIDTH) + (NSA_KV_WIDTH,) * 6 + (N_GATE,)

LANES = 128
MISC_W = LANES
GATE_OFF = GLA_RANK
GLA_TILE = 256
VMEM_LIMIT = 56 * 1024 * 1024

F32 = jnp.float32
BF16 = jnp.bfloat16
HIGHEST = lax.Precision.HIGHEST

_NT = (((1,), (1,)), ((), ()))
_TN = (((0,), (0,)), ((), ()))


def _params(*sem):
    return pltpu.CompilerParams(dimension_semantics=sem, vmem_limit_bytes=VMEM_LIMIT)


def _rms(x):
    return x * lax.rsqrt(jnp.mean(x * x, axis=-1, keepdims=True) + EPS)


def _split_bf16(x):
    hi = x.astype(BF16)
    return hi, (x - hi.astype(F32)).astype(BF16)


def _t_f32(x):
    return x.astype(F32).T


def _adaln_kernel(c_ref, w_ref, b_ref, o_ref):
    c = c_ref[...]
    a = c * jax.nn.sigmoid(c)
    o_ref[...] = jnp.dot(a, w_ref[...], precision=HIGHEST, preferred_element_type=F32) + b_ref[...]


def _adaln(c, w_ada, b_ada):
    B, D = c.shape
    N = w_ada.shape[1]
    tn = D
    return pl.pallas_call(
        _adaln_kernel,
        grid=(N // tn,),
        in_specs=[pl.BlockSpec((B, D), lambda j: (0, 0)),
                  pl.BlockSpec((D, tn), lambda j: (0, j)),
                  pl.BlockSpec((1, tn), lambda j: (0, j))],
        out_specs=pl.BlockSpec((B, tn), lambda j: (0, j)),
        out_shape=jax.ShapeDtypeStruct((B, N), F32),
        compiler_params=_params("arbitrary"),
        name="adaln",
    )(c, w_ada, b_ada.reshape(1, N))


_PROJ_GROUPS = (("gq", GLA_QK, False), ("gk", GLA_QK, False), ("gv", GLA_WIDTH, False),
                ("gg", GLA_WIDTH, False), ("nq", NSA_WIDTH, True), ("kc", NSA_KV_WIDTH, True),
                ("vc", NSA_KV_WIDTH, False), ("ks", NSA_KV_WIDTH, True), ("vs", NSA_KV_WIDTH, False),
                ("kw", NSA_KV_WIDTH, True), ("vw", NSA_KV_WIDTH, False))
_GROUPED = ("kc", "vc")
GRP_W = CMP_STRIDE * NSA_DH


def _in_proj_kernel(x_ref, mod_ref, g_ref, pos_ref, inv_ref, w_ref, *refs):
    n_out = len(_PROJ_GROUPS) + 1
    out_refs, stage_refs = refs[:n_out], refs[n_out:]
    tm = x_ref.shape[0]
    x = x_ref[...]
    h = _rms(x) * g_ref[...] * (1.0 + mod_ref[1:2, :]) + mod_ref[0:1, :]
    hb = h.astype(BF16)

    ang = pos_ref[...].astype(F32) * inv_ref[...]
    lane = lax.broadcasted_iota(jnp.int32, (1, LANES), 1)
    cos = jnp.cos(ang)
    sin = jnp.sin(ang) * jnp.where(lane < NSA_DH // 2, -1.0, 1.0)
    qs = NSA_DH ** -0.5 * float(np.log2(np.e))

    off = 0
    for (name, width, rot), o_ref in zip(_PROJ_GROUPS, out_refs[:-1]):
        y = jnp.dot(hb, w_ref[:, off:off + width], preferred_element_type=F32)
        grouped = name in _GROUPED
        dst = stage_refs[_GROUPED.index(name)] if grouped else o_ref
        for hd in range(width // NSA_DH if (rot or grouped) else 0):
            yh = y[:, hd * NSA_DH:(hd + 1) * NSA_DH]
            if rot:
                c, s = (cos * qs, sin * qs) if name == "nq" else (cos, sin)
                yh = yh * c + pltpu.roll(yh, NSA_DH // 2, axis=1) * s
            if grouped:
                dst[hd] = yh
            else:
                dst[:, hd * NSA_DH:(hd + 1) * NSA_DH] = yh.astype(dst.dtype)
        if not (rot or grouped):
            dst[...] = y.astype(dst.dtype)
        if grouped:
            for k in range(NSA_KV_HEADS):
                for tok in range(CMP_STRIDE):
                    c0 = k * GRP_W + tok * NSA_DH
                    o_ref[:, c0:c0 + NSA_DH] = dst[k, pl.ds(tok, tm // CMP_STRIDE, stride=CMP_STRIDE), :].astype(
                        o_ref.dtype)
        off += width
    out_refs[-1][...] = jnp.dot(hb, w_ref[:, off:off + MISC_W], preferred_element_type=F32)


def _in_proj(x, mod, norm_g, pos3, inv, w_p, tm):
    B, S, D = x.shape
    row = lambda w: pl.BlockSpec((None, tm, w), lambda b, i: (b, i, 0))
    whole = lambda a: pl.BlockSpec(a.shape, lambda b, i: (0,) * a.ndim)
    out_shape, out_specs = [], []
    for name, w, _ in _PROJ_GROUPS:
        if name in _GROUPED:
            out_shape.append(jax.ShapeDtypeStruct((B, S // CMP_STRIDE, CMP_STRIDE * w), BF16))
            out_specs.append(pl.BlockSpec((None, tm // CMP_STRIDE, CMP_STRIDE * w), lambda b, i: (b, i, 0)))
        else:
            out_shape.append(jax.ShapeDtypeStruct((B, S, w), BF16))
            out_specs.append(row(w))
    out_shape.append(jax.ShapeDtypeStruct((B, S, MISC_W), F32))
    out_specs.append(row(MISC_W))
    return pl.pallas_call(
        _in_proj_kernel,
        grid=(B, S // tm),
        in_specs=[row(D), pl.BlockSpec((None, N_MOD, D), lambda b, i: (b, 0, 0)), whole(norm_g),
                  row(1), whole(inv), whole(w_p)],
        out_specs=out_specs,
        out_shape=out_shape,
        scratch_shapes=[pltpu.VMEM((NSA_KV_HEADS, tm, NSA_DH), F32) for _ in _GROUPED],
        compiler_params=_params("parallel", "parallel"),
        name="in_proj",
    )(x, mod, norm_g, pos3, inv, w_p)


def _compress_kernel(xk_ref, xv_ref, kw1_ref, kpe_ref, kw2_ref, vw1_ref, vpe_ref, vw2_ref, ok_ref, ov_ref):
    nb, n_grp, _ = xk_ref.shape
    rows = nb * n_grp
    for x_ref, w1_ref, pe_ref, w2_ref, o_ref in ((xk_ref, kw1_ref, kpe_ref, kw2_ref, ok_ref),
                                                 (xv_ref, vw1_ref, vpe_ref, vw2_ref, ov_ref)):
        w1a = w1_ref[0:GRP_W, :]
        w1b = w1_ref[GRP_W:2 * GRP_W, :]
        x = jnp.concatenate([x_ref[:, :, k * GRP_W:(k + 1) * GRP_W].reshape(rows, GRP_W)
                             for k in range(NSA_KV_HEADS)], axis=0)
        bias = (jnp.dot(pe_ref[:, 0:GRP_W], w1a, preferred_element_type=F32)[0:1]
                + jnp.dot(pe_ref[:, GRP_W:2 * GRP_W], w1b, preferred_element_type=F32)[0:1])
        ua = jnp.dot(x, w1a, preferred_element_type=F32)
        ub = jnp.dot(x, w1b, preferred_element_type=F32)
        hid = ua + pltpu.roll(ub, NSA_KV_HEADS * rows - 1, axis=0) + bias
        act = jax.nn.gelu(hid, approximate=True).astype(BF16)
        y = jnp.dot(act, w2_ref[...], preferred_element_type=F32).astype(o_ref.dtype)
        for k in range(NSA_KV_HEADS):
            o_ref[:, :, k * NSA_DH:(k + 1) * NSA_DH] = y[k * rows:(k + 1) * rows].reshape(nb, n_grp, NSA_DH)


def _compress(xk, xv, kw1, kpe, kw2, vw1, vpe, vw2, nb):
    B, n_grp, gw = xk.shape
    W = NSA_KV_WIDTH
    whole = lambda a: pl.BlockSpec(a.shape, lambda b: (0,) * a.ndim)
    xspec = pl.BlockSpec((nb, n_grp, gw), lambda b: (b, 0, 0))
    ospec = pl.BlockSpec((nb, n_grp, W), lambda b: (b, 0, 0))
    return pl.pallas_call(
        _compress_kernel,
        grid=(B // nb,),
        in_specs=[xspec, xspec, whole(kw1), whole(kpe), whole(kw2), whole(vw1), whole(vpe), whole(vw2)],
        out_specs=[ospec, ospec],
        out_shape=[jax.ShapeDtypeStruct((B, n_grp, W), BF16)] * 2,
        compiler_params=_params("parallel"),
        name="compress",
    )(xk, xv, kw1, kpe, kw2, vw1, vpe, vw2)


def _compress_pe(pe):
    return jnp.zeros((8, 2 * GRP_W), F32).at[0].set(pe.reshape(2 * GRP_W)).astype(BF16)


V_AUG = NSA_DH + 16


def _attend(k_rows, qt, vt_aug, masked):
    s = jnp.dot(k_rows(), qt, preferred_element_type=F32)
    if masked is not None:
        s = jnp.where(masked(), NEG, s)
    yield
    m = jnp.max(s, axis=0, keepdims=True)
    p = jnp.exp2((s - m).astype(BF16))
    yield
    return m, jnp.dot(vt_aug(), p, preferred_element_type=F32)


def _pipelined(gens, depth=2):
    tasks = [_Staged(g) for g in gens]
    for n in range(len(tasks) + depth):
        for t in tasks[max(0, n - depth):n + 1]:
            t.step()
    return [t.finish() for t in tasks]


def _merge(parts):
    m_all = parts[0][0]
    for m, _ in parts[1:]:
        m_all = jnp.maximum(m_all, m)
    tot = None
    for m, acc in parts:
        w = acc * jnp.exp2(m - m_all)
        tot = w if tot is None else tot + w
    return tot[0:NSA_DH] / tot[NSA_DH:NSA_DH + 1]


def _nsa_step(c, tq, tk, q_ref, kc_ref, vc_ref, kw_ref, misc_ref, gn_ref, o_ref, kaug_ref, vst_ref, vwt_ref):
    S = kw_ref.shape[0]
    n_cmp = kc_ref.shape[0]
    n_sel = S // SEL_BLOCK
    G = NSA_HEADS // NSA_KV_HEADS
    M = G * tq
    q0 = c * tq

    t_q = q0 + lax.broadcasted_iota(jnp.int32, (1, tq), 1)
    t_m = jnp.concatenate([t_q] * G, axis=1)
    key_off = lax.broadcasted_iota(jnp.int32, (tk, 1), 0)

    n_col = lax.broadcasted_iota(jnp.int32, (n_cmp, 1), 0)
    cmp_end = jnp.where(n_col < n_cmp - 1, n_col * CMP_STRIDE + (CMP_BLOCK - 1), jnp.int32(2 ** 30))
    cmp_valid = cmp_end <= t_m
    any_valid = (t_m >= CMP_BLOCK - 1).astype(F32)
    jj = lax.broadcasted_iota(jnp.int32, (n_sel, n_cmp), 0) * SEL_BLOCK
    nn = lax.broadcasted_iota(jnp.int32, (n_sel, n_cmp), 1) * CMP_STRIDE
    ov_t = jnp.maximum(jnp.minimum(nn + CMP_BLOCK, jj + SEL_BLOCK) - jnp.maximum(nn, jj), 0)
    ov_t = (ov_t.astype(F32) * (1.0 / CMP_BLOCK)).astype(BF16)
    j_row = lax.broadcasted_iota(jnp.int32, (n_sel, tq), 0)
    blk_t = lax.shift_right_logical(q0 + lax.broadcasted_iota(jnp.int32, (n_sel, tq), 1), 6)
    forced = (j_row == 0) | (j_row == blk_t) | (j_row == blk_t - 1)
    in_past = j_row <= blk_t

    o_cmp = []
    o_sel, o_win = [[] for _ in range(NSA_KV_HEADS)], [[] for _ in range(NSA_KV_HEADS)]
    tasks, owner = [], []
    for k in range(NSA_KV_HEADS):
        hs = slice(k * NSA_DH, (k + 1) * NSA_DH)
        qt = jnp.concatenate([_t_f32(q_ref[:, (k * G + g) * NSA_DH:(k * G + g + 1) * NSA_DH])
                              for g in range(G)], axis=1).astype(BF16)

        s = jnp.dot(kc_ref[:, hs], qt, preferred_element_type=F32)
        s = jnp.where(cmp_valid, s, NEG)
        e = jnp.exp2(s - jnp.max(s, axis=0, keepdims=True))
        p = e / jnp.sum(e, axis=0, keepdims=True) * any_valid
        o_cmp.append(jnp.dot(_t_f32(vc_ref[:, hs]).astype(BF16), p.astype(BF16), preferred_element_type=F32))

        p_grp = p[:, 0:tq]
        for g in range(1, G):
            p_grp = p_grp + p[:, g * tq:(g + 1) * tq]
        p_hi, p_lo = _split_bf16(p_grp)
        imp = (jnp.dot(ov_t, p_hi, preferred_element_type=F32)
               + jnp.dot(ov_t, p_lo, preferred_element_type=F32))
        imp = jnp.where(forced, BIG, jnp.where(in_past, imp, NEG))
        rank = jnp.zeros((n_sel, tq), F32)
        for j in range(n_sel):
            row = imp[j:j + 1, :]
            rank = rank + jnp.where(j_row > j, jnp.where(row >= imp, 1.0, 0.0), jnp.where(row > imp, 1.0, 0.0))
        bias = jnp.where(rank < float(min(SEL_TOPK, n_sel)), 0.0, NEG).astype(BF16)
        qta = jnp.concatenate([qt, jnp.concatenate([bias] * G, axis=1),
                               jnp.zeros((NSA_DH - n_sel, M), BF16)], axis=0)

        for kt in range((q0 + tq) // tk):
            k0 = kt * tk
            ahead = (lambda k0=k0: k0 + key_off > t_m) if k0 + tk > q0 else None
            tasks.append(_attend(lambda k=k, k0=k0: kaug_ref[k, k0:k0 + tk, :], qta,
                                 lambda k=k, kt=kt: vst_ref[k, kt], ahead))
            owner.append((o_sel, k))

        for kt in range(max(q0 - WINDOW, 0) // tk, (q0 + tq) // tk):
            k0 = kt * tk
            if k0 + tk > q0:
                masked = lambda k0=k0: k0 + key_off > t_m
            else:
                masked = lambda k0=k0: t_m - (k0 + key_off) >= WINDOW
            tasks.append(_attend(lambda k0=k0, hs=hs: kw_ref[k0:k0 + tk, hs], qt,
                                 lambda k=k, kt=kt: vwt_ref[k, kt], masked))
            owner.append((o_win, k))

    for (dst, k), part in zip(owner, _pipelined(tasks)):
        dst[k].append(part)
    o_sel = [_merge(parts) for parts in o_sel]
    o_win = [_merge(parts) for parts in o_win]

    gates_t = _t_f32(jax.nn.sigmoid(misc_ref[...]))
    heads = []
    for k in range(NSA_KV_HEADS):
        for g in range(G):
            gs = slice(g * tq, (g + 1) * tq)
            gl = GATE_OFF + (k * G + g) * NSA_BRANCHES
            heads.append(gates_t[gl:gl + 1, :] * o_cmp[k][:, gs] + gates_t[gl + 1:gl + 2, :] * o_sel[k][:, gs]
                         + gates_t[gl + 2:gl + 3, :] * o_win[k][:, gs])
    ssq = heads[0] * heads[0]
    for o in heads[1:]:
        ssq = ssq + o * o
    inv = lax.rsqrt(jnp.sum(ssq, axis=0, keepdims=True) * (1.0 / NSA_WIDTH) + EPS)
    for hq, o in enumerate(heads):
        cs = slice(hq * NSA_DH, (hq + 1) * NSA_DH)
        o_ref[:, cs] = ((o * inv).T * gn_ref[:, cs]).astype(o_ref.dtype)


def _nsa_kernel(q_ref, kc_ref, vc_ref, ks_ref, vs_ref, kw_ref, vw_ref, misc_ref, gn_ref, o_ref,
                kaug_ref, vst_ref, vwt_ref):
    tq = q_ref.shape[0]
    tk = vst_ref.shape[3]
    S = ks_ref.shape[0]
    i = pl.program_id(1)

    @pl.when(i == 0)
    def _():
        pos = lax.broadcasted_iota(jnp.int32, (S, LANES), 0)
        lane = lax.broadcasted_iota(jnp.int32, (S, LANES), 1)
        onehot = jnp.where(lax.shift_right_logical(pos, 6) == lane, 1.0, 0.0).astype(BF16)
        row = lax.broadcasted_iota(jnp.int32, (V_AUG - NSA_DH, tk), 0)
        ones_rows = jnp.where(row == 0, 1.0, 0.0).astype(BF16)
        for k in range(NSA_KV_HEADS):
            hs = slice(k * NSA_DH, (k + 1) * NSA_DH)
            kaug_ref[k, :, 0:NSA_DH] = ks_ref[:, hs]
            kaug_ref[k, :, NSA_DH:2 * NSA_DH] = onehot
            for kt in range(S // tk):
                rows = slice(kt * tk, (kt + 1) * tk)
                for src, dst in ((vs_ref, vst_ref), (vw_ref, vwt_ref)):
                    dst[k, kt, 0:NSA_DH, :] = _t_f32(src[rows, hs]).astype(BF16)
                    dst[k, kt, NSA_DH:V_AUG, :] = ones_rows

    for c in range(S // tq):
        pl.when(i == c)(functools.partial(_nsa_step, c, tq, tk, q_ref, kc_ref, vc_ref, kw_ref, misc_ref,
                                          gn_ref, o_ref, kaug_ref, vst_ref, vwt_ref))


def _nsa(nq, kcmp, vcmp, ks, vs, kw, vw, misc, gn, tq, tk):
    B, S, _ = nq.shape
    n_cmp = kcmp.shape[1]
    assert n_cmp <= LANES and S // SEL_BLOCK <= NSA_DH and SEL_BLOCK == 64
    assert tq % tk == 0 and tq <= WINDOW and WINDOW % tk == 0
    row = lambda w: pl.BlockSpec((None, tq, w), lambda b, i: (b, i, 0))
    seq = lambda n: pl.BlockSpec((None, n, NSA_KV_WIDTH), lambda b, i: (b, 0, 0))
    vt_scratch = pltpu.VMEM((NSA_KV_HEADS, S // tk, V_AUG, tk), BF16)
    return pl.pallas_call(
        _nsa_kernel,
        grid=(B, S // tq),
        in_specs=[row(NSA_WIDTH), seq(n_cmp), seq(n_cmp), seq(S), seq(S), seq(S), seq(S), row(MISC_W),
                  pl.BlockSpec(gn.shape, lambda b, i: (0, 0))],
        out_specs=row(NSA_WIDTH),
        out_shape=jax.ShapeDtypeStruct((B, S, NSA_WIDTH), BF16),
        scratch_shapes=[pltpu.VMEM((NSA_KV_HEADS, S, 2 * NSA_DH), BF16), vt_scratch, vt_scratch],
        compiler_params=_params("parallel", "arbitrary"),
        name="nsa",
    )(nq, kcmp, vcmp, ks, vs, kw, vw, misc, gn)


def _log_sigmoid(z):
    return jnp.minimum(z, 0.0) - jnp.log(1.0 + jnp.exp(-jnp.abs(z)))


class _Staged:
    def __init__(self, gen):
        self.gen, self.value, self.done = gen, None, False

    def step(self):
        if not self.done:
            try:
                next(self.gen)
            except StopIteration as stop:
                self.value, self.done = stop.value, True

    def finish(self):
        while not self.done:
            self.step()
        return self.value


def _gla_tile(q, k, v, g, misc, wa, ba, gn, st):
    C = GLA_CHUNK
    T = q.shape[0]
    n_chunk = T // C
    r = lax.broadcasted_iota(jnp.int32, (T, T), 0)
    c = lax.broadcasted_iota(jnp.int32, (T, T), 1)
    causal = (r >= c) & (r - c <= (r & (C - 1)))
    tri = jnp.where(causal, 1.0, 0.0).astype(BF16)
    sr = lax.broadcasted_iota(jnp.int32, st.shape, 0)
    sc = lax.broadcasted_iota(jnp.int32, st.shape, 1)
    own_head = (lax.shift_right_logical(sr, GLA_DK.bit_length() - 1)
                == lax.shift_right_logical(sc, GLA_DV.bit_length() - 1))

    r_hi, r_lo = _split_bf16(misc)
    w_hi, w_lo = _split_bf16(wa)
    z = (jnp.dot(r_hi, w_hi, preferred_element_type=F32) + jnp.dot(r_lo, w_hi, preferred_element_type=F32)
         + jnp.dot(r_hi, w_lo, preferred_element_type=F32) + ba)
    la_hi, la_lo = _split_bf16(_log_sigmoid(z) * (1.0 / GLA_TAU))
    yield
    b = (jnp.dot(tri, la_hi, preferred_element_type=F32)
         + jnp.dot(tri, la_lo, preferred_element_type=F32))
    b_lasts = [b[(ci + 1) * C - 1:(ci + 1) * C, :] for ci in range(n_chunk)]
    b_last = jnp.concatenate([jnp.broadcast_to(bl, (C, bl.shape[1])) for bl in b_lasts], axis=0)
    qf = q.astype(F32)
    kf = k.astype(F32)
    q_in = (qf * GLA_DK ** -0.5 * jnp.exp(b)).astype(BF16)
    k_in = (kf * jnp.exp(-b)).astype(BF16)
    k_dec = (kf * jnp.exp(b_last - b)).astype(BF16)
    pad = jnp.zeros((LANES - n_chunk, b.shape[1]), F32)
    dec_cols = jnp.exp(jnp.concatenate(b_lasts + [pad], axis=0).T)
    yield

    o_intra = []
    for h in range(GLA_HEADS):
        ks = slice(h * GLA_DK, (h + 1) * GLA_DK)
        att = lax.dot_general(q_in[:, ks], k_in[:, ks], _NT, preferred_element_type=F32)
        att = jnp.where(causal, att, 0.0).astype(BF16)
        o_intra.append(jnp.dot(att, v[:, h * GLA_DV:(h + 1) * GLA_DV], preferred_element_type=F32))
        yield

    o_inter = []
    for ci in range(n_chunk):
        rows = slice(ci * C, (ci + 1) * C)
        o_inter.append(jnp.dot(q_in[rows], st.astype(BF16), preferred_element_type=F32))
        d_st = lax.dot_general(k_dec[rows], v[rows], _TN, preferred_element_type=F32)
        st = st * dec_cols[:, ci:ci + 1] + jnp.where(own_head, d_st, 0.0)
        yield
    o_inter = jnp.concatenate(o_inter, axis=0)
    gate = g.astype(F32)
    gate = gate * jax.nn.sigmoid(gate)
    outs = []
    for h in range(GLA_HEADS):
        vs = slice(h * GLA_DV, (h + 1) * GLA_DV)
        outs.append(_rms(o_intra[h] + o_inter[:, vs]) * gn[:, vs] * gate[:, vs])
    return jnp.concatenate(outs, axis=1), st


def _gla_ffn_kernel(x_ref, on_ref, mod_ref, wo_ref, g2_ref, w1_ref, w2_ref, gf_ref,
                    q0_ref, k0_ref, v0_ref, g0_ref, m0_ref, q1_ref, k1_ref, v1_ref, g1_ref, m1_ref,
                    wa_ref, ba_ref, gn_ref, o_ref, st_ref, og_ref, *, ff_chunk, final):
    def gla(q_ref, k_ref, v_ref, g_ref, m_ref, st):
        return _Staged(_gla_tile(q_ref[...], k_ref[...], v_ref[...], g_ref[...], m_ref[...],
                                 wa_ref[...], ba_ref[...], gn_ref[...], st))

    @pl.when(pl.program_id(1) == 0)
    def _():
        o, st = gla(q0_ref, k0_ref, v0_ref, g0_ref, m0_ref, jnp.zeros(st_ref.shape, F32)).finish()
        og_ref[...] = o.astype(BF16)
        st_ref[...] = st

    ahead = gla(q1_ref, k1_ref, v1_ref, g1_ref, m1_ref, st_ref[...])

    ahead.step()
    mix = (jnp.dot(og_ref[...], wo_ref[0:GLA_WIDTH, :], preferred_element_type=F32)
           + jnp.dot(on_ref[...], wo_ref[GLA_WIDTH:, :], preferred_element_type=F32))
    x1 = x_ref[...] + mod_ref[2:3, :] * mix
    h = (_rms(x1) * g2_ref[...] * (1.0 + mod_ref[4:5, :]) + mod_ref[3:4, :]).astype(BF16)
    acc = jnp.zeros_like(x1)
    ahead.step()
    for c in range(w1_ref.shape[1] // ff_chunk):
        cs = slice(c * ff_chunk, (c + 1) * ff_chunk)
        a = jnp.maximum(jnp.dot(h, w1_ref[:, cs], preferred_element_type=F32), 0.0)
        ahead.step()
        acc = acc + jnp.dot((a * a).astype(BF16), w2_ref[cs, :], preferred_element_type=F32)
        ahead.step()
    x2 = x1 + mod_ref[5:6, :] * acc
    o_ref[...] = _rms(x2) * gf_ref[...] if final else x2

    o, st = ahead.finish()
    og_ref[...] = o.astype(BF16)
    st_ref[...] = st


def _gla_ffn(x, on, mod, wo, g2, w1, w2, gf, gq, gk, gv, gg, misc, wa_p, ba, gn, final):
    B, S, D = x.shape
    tm = GLA_TILE
    n = S // tm
    row = lambda w: pl.BlockSpec((None, tm, w), lambda b, i: (b, i, 0))
    first = lambda w: pl.BlockSpec((None, tm, w), lambda b, i: (b, 0, 0))
    ahead = lambda w: pl.BlockSpec((None, tm, w), lambda b, i: (b, jnp.minimum(i + 1, n - 1), 0))
    whole = lambda a: pl.BlockSpec(a.shape, lambda b, i: (0,) * a.ndim)
    gla_in = (gq, gk, gv, gg, misc)
    return pl.pallas_call(
        functools.partial(_gla_ffn_kernel, ff_chunk=1024, final=final),
        grid=(B, n),
        in_specs=([row(D), row(on.shape[2]), pl.BlockSpec((None, N_MOD, D), lambda b, i: (b, 0, 0)),
                   whole(wo), whole(g2), whole(w1), whole(w2), whole(gf)]
                  + [first(a.shape[2]) for a in gla_in] + [ahead(a.shape[2]) for a in gla_in]
                  + [whole(wa_p), whole(ba), whole(gn)]),
        out_specs=row(D),
        out_shape=jax.ShapeDtypeStruct((B, S, D), F32),
        scratch_shapes=[pltpu.VMEM((GLA_QK, GLA_WIDTH), F32), pltpu.VMEM((tm, GLA_WIDTH), BF16)],
        compiler_params=_params("parallel", "arbitrary"),
        name="gla_ffn",
    )(x, on, mod, wo, g2, w1, w2, gf, *gla_in, *gla_in, wa_p, ba, gn)


def _reorder_w_in(w_in):
    cuts = [int(v) for v in np.cumsum(IN_SPLITS)[:-1]]
    gq, gk, gv, gg, gr, nq, kc, vc, ks, vs, kw, vw, ngate = jnp.split(w_in, cuts, axis=1)
    pad = jnp.zeros((w_in.shape[0], MISC_W - GLA_RANK - N_GATE), w_in.dtype)
    return jnp.concatenate([gq, gk, gv, gg, nq, kc, vc, ks, vs, kw, vw, gr, ngate, pad], axis=1).astype(BF16)


def kernel(x, c, positions, w_ada, b_ada, norm1_g, w_in, gla_w_a2, gla_b_a, gla_norm_g, nsa_pe_k, nsa_pe_v, cmp_k_w1, cmp_k_w2, cmp_v_w1, cmp_v_w2, nsa_norm_g, w_o, norm2_g, w_ff1, w_ff2, final_norm_g):
    B, S, D = x.shape
    depth = w_in.shape[0]
    half = NSA_DH // 2
    inv = ROPE_THETA ** (-jnp.arange(half, dtype=F32) / half)
    inv = jnp.concatenate([inv, inv]).reshape(1, NSA_DH)
    pos3 = positions.reshape(B, S, 1)
    for l in range(depth):
        mod = _adaln(c, w_ada[l], b_ada[l]).reshape(B, N_MOD, D)
        (gq, gk, gv, gg, nq, kc, vc, ks, vs, kw, vw, misc) = _in_proj(
            x, mod, norm1_g[l].reshape(1, D), pos3, inv, _reorder_w_in(w_in[l]), tm=512)

        k_cmp, v_cmp = _compress(kc, vc, cmp_k_w1[l].astype(BF16), _compress_pe(nsa_pe_k[l]),
                                 cmp_k_w2[l].astype(BF16), cmp_v_w1[l].astype(BF16),
                                 _compress_pe(nsa_pe_v[l]), cmp_v_w2[l].astype(BF16), nb=int(np.gcd(B, 4)))
        o_nsa = _nsa(nq, k_cmp, v_cmp, ks, vs, kw, vw, misc, nsa_norm_g[l].reshape(1, NSA_WIDTH),
                     tq=512, tk=512)

        wa_p = jnp.zeros((MISC_W, GLA_QK), F32).at[0:GLA_RANK].set(gla_w_a2[l])
        x = _gla_ffn(x, o_nsa, mod, w_o[l].astype(BF16), norm2_g[l].reshape(1, D),
                     w_ff1[l].astype(BF16), w_ff2[l].astype(BF16), final_norm_g.reshape(1, D),
                     gq, gk, gv, gg, misc, wa_p, gla_b_a[l].reshape(1, GLA_QK),
                     gla_norm_g[l].reshape(1, GLA_WIDTH), final=(l == depth - 1))
    return x
```

```python
import functools

import numpy as np
import jax
import jax.numpy as jnp
from jax import lax
from jax.experimental import pallas as pl
from jax.experimental.pallas import tpu as pltpu

GLA_HEADS = 4
GLA_DK = 64
GLA_DV = 128
GLA_RANK = 16
GLA_TAU = 16.0
GLA_CHUNK = 64
NSA_HEADS = 4
NSA_KV_HEADS = 2
NSA_DH = 128
NSA_BRANCHES = 3
CMP_BLOCK = 32
CMP_STRIDE = 16
CMP_HIDDEN = 256
SEL_BLOCK = 64
SEL_TOPK = 16
WINDOW = 512
N_MOD = 6
ROPE_THETA = 10000.0
EPS = 1e-6
NEG = -1e30
BIG = 1e30

GLA_QK = GLA_HEADS * GLA_DK
GLA_WIDTH = GLA_HEADS * GLA_DV
NSA_WIDTH = NSA_HEADS * NSA_DH
NSA_KV_WIDTH = NSA_KV_HEADS * NSA_DH
N_GATE = NSA_HEADS * NSA_BRANCHES
IN_SPLITS = (GLA_QK, GLA_QK, GLA_WIDTH, GLA_WIDTH, GLA_RANK, NSA_WIDTH) + (NSA_KV_WIDTH,) * 6 + (N_GATE,)

LANES = 128
SUBLANES = 8
MISC_W = LANES
GATE_OFF = GLA_RANK
GLA_TILE = 256
VMEM_LIMIT = 56 * 1024 * 1024

F32 = jnp.float32
BF16 = jnp.bfloat16
HIGHEST = lax.Precision.HIGHEST

_NT = (((1,), (1,)), ((), ()))
_TN = (((0,), (0,)), ((), ()))


def _params(*sem):
    return pltpu.CompilerParams(dimension_semantics=sem, vmem_limit_bytes=VMEM_LIMIT)


def _rms(x):
    return x * lax.rsqrt(jnp.mean(x * x, axis=-1, keepdims=True) + EPS)


def _split_bf16(x):
    hi = x.astype(BF16)
    return hi, (x - hi.astype(F32)).astype(BF16)


def _t_f32(x):
    return x.astype(F32).T


def _adaln_kernel(c_ref, w_ref, b_ref, o_ref):
    c = c_ref[...]
    a = c * jax.nn.sigmoid(c)
    o_ref[...] = jnp.dot(a, w_ref[...], precision=HIGHEST, preferred_element_type=F32) + b_ref[...]


def _adaln(c, w_ada, b_ada):
    B, D = c.shape
    N = w_ada.shape[1]
    tn = D
    return pl.pallas_call(
        _adaln_kernel,
        grid=(N // tn,),
        in_specs=[pl.BlockSpec((B, D), lambda j: (0, 0)),
                  pl.BlockSpec((D, tn), lambda j: (0, j)),
                  pl.BlockSpec((1, tn), lambda j: (0, j))],
        out_specs=pl.BlockSpec((B, tn), lambda j: (0, j)),
        out_shape=jax.ShapeDtypeStruct((B, N), F32),
        compiler_params=_params("arbitrary"),
        name="adaln",
    )(c, w_ada, b_ada.reshape(1, N))


_PROJ_GROUPS = (("gq", GLA_QK, False), ("gk", GLA_QK, False), ("gv", GLA_WIDTH, False),
                ("gg", GLA_WIDTH, False), ("nq", NSA_WIDTH, True), ("kc", NSA_KV_WIDTH, True),
                ("vc", NSA_KV_WIDTH, False), ("ks", NSA_KV_WIDTH, True), ("vs", NSA_KV_WIDTH, False),
                ("kw", NSA_KV_WIDTH, True), ("vw", NSA_KV_WIDTH, False))
_GROUPED = ("kc", "vc")
GRP_W = CMP_STRIDE * NSA_DH


def _in_proj_kernel(x_ref, mod_ref, g_ref, pos_ref, inv_ref, w_ref, *refs):
    n_out = len(_PROJ_GROUPS) + 1
    out_refs, stage_refs = refs[:n_out], refs[n_out:]
    tm = x_ref.shape[0]
    x = x_ref[...]
    h = _rms(x) * g_ref[...] * (1.0 + mod_ref[1:2, :]) + mod_ref[0:1, :]
    hb = h.astype(BF16)

    qs = NSA_DH ** -0.5 * float(np.log2(np.e))
    cos = sin = None

    off = 0
    for (name, width, rot), o_ref in zip(_PROJ_GROUPS, out_refs[:-1]):
        if rot and cos is None:
            ang = pos_ref[...].astype(F32) * inv_ref[...]
            lane = lax.broadcasted_iota(jnp.int32, (1, LANES), 1)
            cos = jnp.cos(ang)
            sin = jnp.sin(ang) * jnp.where(lane < NSA_DH // 2, -1.0, 1.0)
        y = jnp.dot(hb, w_ref[:, off:off + width], preferred_element_type=F32)
        grouped = name in _GROUPED
        dst = stage_refs[_GROUPED.index(name)] if grouped else o_ref
        for hd in range(width // NSA_DH if (rot or grouped) else 0):
            yh = y[:, hd * NSA_DH:(hd + 1) * NSA_DH]
            if rot:
                c, s = (cos * qs, sin * qs) if name == "nq" else (cos, sin)
                yh = yh * c + pltpu.roll(yh, NSA_DH // 2, axis=1) * s
            if grouped:
                dst[hd] = yh
            else:
                dst[:, hd * NSA_DH:(hd + 1) * NSA_DH] = yh.astype(dst.dtype)
        if not (rot or grouped):
            dst[...] = y.astype(dst.dtype)
        if grouped:
            for k in range(NSA_KV_HEADS):
                for tok in range(CMP_STRIDE):
                    c0 = k * GRP_W + tok * NSA_DH
                    o_ref[:, c0:c0 + NSA_DH] = dst[k, pl.ds(tok, tm // CMP_STRIDE, stride=CMP_STRIDE), :].astype(
                        o_ref.dtype)
        off += width
    out_refs[-1][...] = jnp.dot(hb, w_ref[:, off:off + MISC_W], preferred_element_type=F32)


def _in_proj(x, mod, norm_g, pos3, inv, w_p, tm):
    B, S, D = x.shape
    row = lambda w: pl.BlockSpec((None, tm, w), lambda b, i: (b, i, 0))
    whole = lambda a: pl.BlockSpec(a.shape, lambda b, i: (0,) * a.ndim)
    out_shape, out_specs = [], []
    for name, w, _ in _PROJ_GROUPS:
        if name in _GROUPED:
            out_shape.append(jax.ShapeDtypeStruct((B, S // CMP_STRIDE, CMP_STRIDE * w), BF16))
            out_specs.append(pl.BlockSpec((None, tm // CMP_STRIDE, CMP_STRIDE * w), lambda b, i: (b, i, 0)))
        else:
            out_shape.append(jax.ShapeDtypeStruct((B, S, w), BF16))
            out_specs.append(row(w))
    out_shape.append(jax.ShapeDtypeStruct((B, S, MISC_W), F32))
    out_specs.append(row(MISC_W))
    return pl.pallas_call(
        _in_proj_kernel,
        grid=(B, S // tm),
        in_specs=[row(D), pl.BlockSpec((None, N_MOD, D), lambda b, i: (b, 0, 0)), whole(norm_g),
                  row(1), whole(inv), whole(w_p)],
        out_specs=out_specs,
        out_shape=out_shape,
        scratch_shapes=[pltpu.VMEM((NSA_KV_HEADS, tm, NSA_DH), F32) for _ in _GROUPED],
        compiler_params=_params("parallel", "parallel"),
        name="in_proj",
    )(x, mod, norm_g, pos3, inv, w_p)


def _compress_kernel(xk_ref, xv_ref, kw1_ref, kpe_ref, kw2_ref, vw1_ref, vpe_ref, vw2_ref, ok_ref, ov_ref):
    nb, n_grp, _ = xk_ref.shape
    rows = nb * n_grp
    for x_ref, w1_ref, pe_ref, w2_ref, o_ref in ((xk_ref, kw1_ref, kpe_ref, kw2_ref, ok_ref),
                                                 (xv_ref, vw1_ref, vpe_ref, vw2_ref, ov_ref)):
        w1a = w1_ref[0:GRP_W, :]
        w1b = w1_ref[GRP_W:2 * GRP_W, :]
        x = jnp.concatenate([x_ref[:, :, k * GRP_W:(k + 1) * GRP_W].reshape(rows, GRP_W)
                             for k in range(NSA_KV_HEADS)], axis=0)
        bias = (jnp.dot(pe_ref[:, 0:GRP_W], w1a, preferred_element_type=F32)[0:1]
                + jnp.dot(pe_ref[:, GRP_W:2 * GRP_W], w1b, preferred_element_type=F32)[0:1])
        ua = jnp.dot(x, w1a, preferred_element_type=F32)
        ub = jnp.dot(x, w1b, preferred_element_type=F32)
        hid = ua + pltpu.roll(ub, NSA_KV_HEADS * rows - 1, axis=0) + bias
        act = jax.nn.gelu(hid, approximate=True).astype(BF16)
        y = jnp.dot(act, w2_ref[...], preferred_element_type=F32).astype(o_ref.dtype)
        for k in range(NSA_KV_HEADS):
            o_ref[:, :, k * NSA_DH:(k + 1) * NSA_DH] = y[k * rows:(k + 1) * rows].reshape(nb, n_grp, NSA_DH)


def _compress(xk, xv, kw1, kpe, kw2, vw1, vpe, vw2, nb):
    B, n_grp, gw = xk.shape
    W = NSA_KV_WIDTH
    whole = lambda a: pl.BlockSpec(a.shape, lambda b: (0,) * a.ndim)
    xspec = pl.BlockSpec((nb, n_grp, gw), lambda b: (b, 0, 0))
    ospec = pl.BlockSpec((nb, n_grp, W), lambda b: (b, 0, 0))
    return pl.pallas_call(
        _compress_kernel,
        grid=(B // nb,),
        in_specs=[xspec, xspec, whole(kw1), whole(kpe), whole(kw2), whole(vw1), whole(vpe), whole(vw2)],
        out_specs=[ospec, ospec],
        out_shape=[jax.ShapeDtypeStruct((B, n_grp, W), BF16)] * 2,
        compiler_params=_params("parallel"),
        name="compress",
    )(xk, xv, kw1, kpe, kw2, vw1, vpe, vw2)


def _compress_pe(pe):
    return jnp.zeros((8, 2 * GRP_W), F32).at[0].set(pe.reshape(2 * GRP_W)).astype(BF16)


V_AUG = NSA_DH + 16


def _attend(k_rows, qt, vt_aug, masked):
    s = jnp.dot(k_rows(), qt, preferred_element_type=F32)
    if masked is not None:
        s = jnp.where(masked(), NEG, s)
    yield
    m = jnp.max(s, axis=0, keepdims=True)
    p = jnp.exp2((s - m).astype(BF16))
    yield
    return m, jnp.dot(vt_aug(), p, preferred_element_type=F32)


def _pipelined(gens, depth=2, side=()):
    tasks = [_Staged(g) for g in gens]
    for n in range(len(tasks) + depth):
        for t in tasks[max(0, n - depth):n + 1]:
            t.step()
        for s in side:
            s.step()
    return [t.finish() for t in tasks]


def _merge(parts):
    m_all = parts[0][0]
    for m, _ in parts[1:]:
        m_all = jnp.maximum(m_all, m)
    tot = None
    for m, acc in parts:
        w = acc * jnp.exp2(m - m_all)
        tot = w if tot is None else tot + w
    return tot[0:NSA_DH] / tot[NSA_DH:NSA_DH + 1]


def _nsa_step(c, tq, tk, q_ref, kc_ref, vc_ref, kw_ref, misc_ref, gn_ref, o_ref, kaug_ref, vst_ref, vwt_ref):
    S = kw_ref.shape[0]
    n_cmp = kc_ref.shape[0]
    n_sel = S // SEL_BLOCK
    G = NSA_HEADS // NSA_KV_HEADS
    M = G * tq
    q0 = c * tq

    t_q = q0 + lax.broadcasted_iota(jnp.int32, (1, tq), 1)
    t_m = jnp.concatenate([t_q] * G, axis=1)
    key_off = lax.broadcasted_iota(jnp.int32, (tk, 1), 0)

    n_col = lax.broadcasted_iota(jnp.int32, (n_cmp, 1), 0)
    cmp_end = jnp.where(n_col < n_cmp - 1, n_col * CMP_STRIDE + (CMP_BLOCK - 1), jnp.int32(2 ** 30))
    cmp_valid = cmp_end <= t_m
    any_valid = (t_m >= CMP_BLOCK - 1).astype(F32)
    jj = lax.broadcasted_iota(jnp.int32, (n_sel, n_cmp), 0) * SEL_BLOCK
    nn = lax.broadcasted_iota(jnp.int32, (n_sel, n_cmp), 1) * CMP_STRIDE
    ov_t = jnp.maximum(jnp.minimum(nn + CMP_BLOCK, jj + SEL_BLOCK) - jnp.maximum(nn, jj), 0)
    ov_t = (ov_t.astype(F32) * (1.0 / CMP_BLOCK)).astype(BF16)
    j_row = lax.broadcasted_iota(jnp.int32, (n_sel, tq), 0)
    blk_t = lax.shift_right_logical(q0 + lax.broadcasted_iota(jnp.int32, (n_sel, tq), 1), 6)
    forced = (j_row == 0) | (j_row == blk_t) | (j_row == blk_t - 1)
    in_past = j_row <= blk_t

    heads = range(NSA_KV_HEADS)
    qts = [jnp.concatenate([_t_f32(q_ref[:, (k * G + g) * NSA_DH:(k * G + g + 1) * NSA_DH])
                            for g in range(G)], axis=1).astype(BF16) for k in heads]

    def select(k):
        hs = slice(k * NSA_DH, (k + 1) * NSA_DH)
        s = jnp.dot(kc_ref[:, hs], qts[k], preferred_element_type=F32)
        s = jnp.where(cmp_valid, s, NEG)
        yield
        e = jnp.exp2(s - jnp.max(s, axis=0, keepdims=True))
        p = e / jnp.sum(e, axis=0, keepdims=True) * any_valid
        yield
        o_cmp = jnp.dot(_t_f32(vc_ref[:, hs]).astype(BF16), p.astype(BF16), preferred_element_type=F32)
        p_grp = p[:, 0:tq]
        for g in range(1, G):
            p_grp = p_grp + p[:, g * tq:(g + 1) * tq]
        p_hi, p_lo = _split_bf16(p_grp)
        imp = (jnp.dot(ov_t, p_hi, preferred_element_type=F32)
               + jnp.dot(ov_t, p_lo, preferred_element_type=F32))
        imp = jnp.where(forced, BIG, jnp.where(in_past, imp, NEG))
        yield
        rank = []
        for r0 in range(0, n_sel, SUBLANES):
            blk = imp[r0:r0 + SUBLANES, :]
            cnt = jnp.zeros(blk.shape, F32)
            for j in range(n_sel):
                row = imp[j:j + 1, :]
                if j < r0:
                    cnt = cnt + jnp.where(row >= blk, 1.0, 0.0)
                elif j >= r0 + SUBLANES - 1:
                    cnt = cnt + jnp.where(row > blk, 1.0, 0.0)
                else:
                    below = r0 + lax.broadcasted_iota(jnp.int32, blk.shape, 0) > j
                    cnt = cnt + jnp.where(below, jnp.where(row >= blk, 1.0, 0.0), jnp.where(row > blk, 1.0, 0.0))
            rank.append(cnt)
            yield
        rank = jnp.concatenate(rank, axis=0)
        bias = jnp.where(rank < float(min(SEL_TOPK, n_sel)), 0.0, NEG).astype(BF16)
        return o_cmp, jnp.concatenate([qts[k], jnp.concatenate([bias] * G, axis=1),
                                       jnp.zeros((NSA_DH - n_sel, M), BF16)], axis=0)

    subs = range(tq // tk)
    col_off = jnp.concatenate([lax.broadcasted_iota(jnp.int32, (1, tk), 1)] * G, axis=1)
    after = lambda: key_off > col_off
    too_far = lambda: key_off <= col_off

    def cols(x, j):
        return jnp.concatenate([x[:, g * tq + j * tk:g * tq + (j + 1) * tk] for g in range(G)], axis=1)

    def sel_tasks(k, j, qta):
        first, diag = (0, q0 // tk - 1) if j is None else (q0 // tk, (q0 + j * tk) // tk)
        for kt in range(first, diag + 1):
            yield _attend(lambda kt=kt: kaug_ref[k, kt * tk:(kt + 1) * tk, :], qta,
                          lambda kt=kt: vst_ref[k, kt], after if kt == diag and j is not None else None)

    def win_tasks(k, j, qt):
        hs = slice(k * NSA_DH, (k + 1) * NSA_DH)
        diag = (q0 + j * tk) // tk
        back = WINDOW // tk
        for kt in range(max(diag - back, 0), diag + 1):
            masked = after if kt == diag else too_far if kt == diag - back else None
            yield _attend(lambda kt=kt: kw_ref[kt * tk:(kt + 1) * tk, hs], qt, lambda kt=kt: vwt_ref[k, kt], masked)

    def run(task_lists, side=()):
        flat = [(key, t) for key, tasks in task_lists.items() for t in tasks]
        parts = {key: [] for key in task_lists}
        for (key, _), part in zip(flat, _pipelined([t for _, t in flat], side=side)):
            parts[key].append(part)
        merged = {}
        for k in heads:
            for j in subs:
                shared = [(cols(m, j), cols(acc, j)) for m, acc in parts.get((k, None), [])]
                merged[k, j] = _merge(shared + parts[k, j])
        return [jnp.concatenate([merged[k, j][:, g * tk:(g + 1) * tk] for g in range(G) for j in subs], axis=1)
                for k in heads]

    selects = [_Staged(select(k)) for k in heads]
    o_win = run({(k, j): list(win_tasks(k, j, cols(qts[k], j))) for k in heads for j in subs}, side=selects)
    o_cmp, qtas = zip(*[s.finish() for s in selects])
    o_sel = run({(k, j): list(sel_tasks(k, j, qtas[k] if j is None else cols(qtas[k], j)))
                 for k in heads for j in (None, *subs)})


    gates_t = _t_f32(jax.nn.sigmoid(misc_ref[...]))
    heads = []
    for k in range(NSA_KV_HEADS):
        for g in range(G):
            gs = slice(g * tq, (g + 1) * tq)
            gl = GATE_OFF + (k * G + g) * NSA_BRANCHES
            heads.append(gates_t[gl:gl + 1, :] * o_cmp[k][:, gs] + gates_t[gl + 1:gl + 2, :] * o_sel[k][:, gs]
                         + gates_t[gl + 2:gl + 3, :] * o_win[k][:, gs])
    ssq = heads[0] * heads[0]
    for o in heads[1:]:
        ssq = ssq + o * o
    inv = lax.rsqrt(jnp.sum(ssq, axis=0, keepdims=True) * (1.0 / NSA_WIDTH) + EPS)
    for hq, o in enumerate(heads):
        cs = slice(hq * NSA_DH, (hq + 1) * NSA_DH)
        o_ref[:, cs] = ((o * inv).T * gn_ref[:, cs]).astype(o_ref.dtype)


def _nsa_kernel(q_ref, kc_ref, vc_ref, ks_ref, vs_ref, kw_ref, vw_ref, misc_ref, gn_ref, o_ref,
                kaug_ref, vst_ref, vwt_ref):
    tq = q_ref.shape[0]
    tk = vst_ref.shape[3]
    S = ks_ref.shape[0]
    i = pl.program_id(1)

    @pl.when(i == 0)
    def _():
        pos = lax.broadcasted_iota(jnp.int32, (S, LANES), 0)
        lane = lax.broadcasted_iota(jnp.int32, (S, LANES), 1)
        onehot = jnp.where(lax.shift_right_logical(pos, 6) == lane, 1.0, 0.0).astype(BF16)
        row = lax.broadcasted_iota(jnp.int32, (V_AUG - NSA_DH, tk), 0)
        ones_rows = jnp.where(row == 0, 1.0, 0.0).astype(BF16)
        for k in range(NSA_KV_HEADS):
            hs = slice(k * NSA_DH, (k + 1) * NSA_DH)
            kaug_ref[k, :, 0:NSA_DH] = ks_ref[:, hs]
            kaug_ref[k, :, NSA_DH:2 * NSA_DH] = onehot
            for kt in range(S // tk):
                rows = slice(kt * tk, (kt + 1) * tk)
                for src, dst in ((vs_ref, vst_ref), (vw_ref, vwt_ref)):
                    dst[k, kt, 0:NSA_DH, :] = _t_f32(src[rows, hs]).astype(BF16)
                    dst[k, kt, NSA_DH:V_AUG, :] = ones_rows

    for c in range(S // tq):
        pl.when(i == c)(functools.partial(_nsa_step, c, tq, tk, q_ref, kc_ref, vc_ref, kw_ref, misc_ref,
                                          gn_ref, o_ref, kaug_ref, vst_ref, vwt_ref))


def _nsa(nq, kcmp, vcmp, ks, vs, kw, vw, misc, gn, tq, tk):
    B, S, _ = nq.shape
    n_cmp = kcmp.shape[1]
    assert n_cmp <= LANES and S // SEL_BLOCK <= NSA_DH and SEL_BLOCK == 64
    assert tq % tk == 0 and tq <= WINDOW and WINDOW % tk == 0
    row = lambda w: pl.BlockSpec((None, tq, w), lambda b, i: (b, i, 0))
    seq = lambda n: pl.BlockSpec((None, n, NSA_KV_WIDTH), lambda b, i: (b, 0, 0))
    vt_scratch = pltpu.VMEM((NSA_KV_HEADS, S // tk, V_AUG, tk), BF16)
    return pl.pallas_call(
        _nsa_kernel,
        grid=(B, S // tq),
        in_specs=[row(NSA_WIDTH), seq(n_cmp), seq(n_cmp), seq(S), seq(S), seq(S), seq(S), row(MISC_W),
                  pl.BlockSpec(gn.shape, lambda b, i: (0, 0))],
        out_specs=row(NSA_WIDTH),
        out_shape=jax.ShapeDtypeStruct((B, S, NSA_WIDTH), BF16),
        scratch_shapes=[pltpu.VMEM((NSA_KV_HEADS, S, 2 * NSA_DH), BF16), vt_scratch, vt_scratch],
        compiler_params=_params("parallel", "arbitrary"),
        name="nsa",
    )(nq, kcmp, vcmp, ks, vs, kw, vw, misc, gn)


def _log_sigmoid(z):
    return jnp.minimum(z, 0.0) - jnp.log(1.0 + jnp.exp(-jnp.abs(z)))


class _Staged:
    def __init__(self, gen):
        self.gen, self.value, self.done = gen, None, False

    def step(self):
        if not self.done:
            try:
                next(self.gen)
            except StopIteration as stop:
                self.value, self.done = stop.value, True

    def finish(self):
        while not self.done:
            self.step()
        return self.value


def _gla_tile(q, k, v, g, misc, wa, ba, gn, st):
    C = GLA_CHUNK
    T = q.shape[0]
    n_chunk = T // C
    r = lax.broadcasted_iota(jnp.int32, (T, T), 0)
    c = lax.broadcasted_iota(jnp.int32, (T, T), 1)
    causal = (r >= c) & (r - c <= (r & (C - 1)))
    tri = jnp.where(causal, 1.0, 0.0).astype(BF16)
    sr = lax.broadcasted_iota(jnp.int32, st.shape, 0)
    sc = lax.broadcasted_iota(jnp.int32, st.shape, 1)
    own_head = (lax.shift_right_logical(sr, GLA_DK.bit_length() - 1)
                == lax.shift_right_logical(sc, GLA_DV.bit_length() - 1))

    r_hi, r_lo = _split_bf16(misc)
    w_hi, w_lo = _split_bf16(wa)
    z = (jnp.dot(r_hi, w_hi, preferred_element_type=F32) + jnp.dot(r_lo, w_hi, preferred_element_type=F32)
         + jnp.dot(r_hi, w_lo, preferred_element_type=F32) + ba)
    la_hi, la_lo = _split_bf16(_log_sigmoid(z) * (1.0 / GLA_TAU))
    yield
    b = (jnp.dot(tri, la_hi, preferred_element_type=F32)
         + jnp.dot(tri, la_lo, preferred_element_type=F32))
    b_lasts = [b[(ci + 1) * C - 1:(ci + 1) * C, :] for ci in range(n_chunk)]
    b_last = jnp.concatenate([jnp.broadcast_to(bl, (C, bl.shape[1])) for bl in b_lasts], axis=0)
    qf = q.astype(F32)
    kf = k.astype(F32)
    q_in = (qf * GLA_DK ** -0.5 * jnp.exp(b)).astype(BF16)
    k_in = (kf * jnp.exp(-b)).astype(BF16)
    k_dec = (kf * jnp.exp(b_last - b)).astype(BF16)
    pad = jnp.zeros((LANES - n_chunk, b.shape[1]), F32)
    dec_cols = jnp.exp(jnp.concatenate(b_lasts + [pad], axis=0).T)
    yield

    o_intra = []
    for h in range(GLA_HEADS):
        ks = slice(h * GLA_DK, (h + 1) * GLA_DK)
        att = lax.dot_general(q_in[:, ks], k_in[:, ks], _NT, preferred_element_type=F32)
        att = jnp.where(causal, att, 0.0).astype(BF16)
        o_intra.append(jnp.dot(att, v[:, h * GLA_DV:(h + 1) * GLA_DV], preferred_element_type=F32))
        yield

    o_inter = []
    for ci in range(n_chunk):
        rows = slice(ci * C, (ci + 1) * C)
        o_inter.append(jnp.dot(q_in[rows], st.astype(BF16), preferred_element_type=F32))
        d_st = lax.dot_general(k_dec[rows], v[rows], _TN, preferred_element_type=F32)
        st = st * dec_cols[:, ci:ci + 1] + jnp.where(own_head, d_st, 0.0)
        yield
    o_inter = jnp.concatenate(o_inter, axis=0)
    gate = g.astype(F32)
    gate = gate * jax.nn.sigmoid(gate)
    outs = []
    for h in range(GLA_HEADS):
        vs = slice(h * GLA_DV, (h + 1) * GLA_DV)
        outs.append(_rms(o_intra[h] + o_inter[:, vs]) * gn[:, vs] * gate[:, vs])
    return jnp.concatenate(outs, axis=1), st


def _gla_ffn_kernel(x_ref, on_ref, mod_ref, wo_ref, g2_ref, w1_ref, w2_ref, gf_ref,
                    q0_ref, k0_ref, v0_ref, g0_ref, m0_ref, q1_ref, k1_ref, v1_ref, g1_ref, m1_ref,
                    wa_ref, ba_ref, gn_ref, o_ref, st_ref, og_ref, *, ff_chunk, final):
    def gla(q_ref, k_ref, v_ref, g_ref, m_ref, st):
        return _Staged(_gla_tile(q_ref[...], k_ref[...], v_ref[...], g_ref[...], m_ref[...],
                                 wa_ref[...], ba_ref[...], gn_ref[...], st))

    @pl.when(pl.program_id(1) == 0)
    def _():
        o, st = gla(q0_ref, k0_ref, v0_ref, g0_ref, m0_ref, jnp.zeros(st_ref.shape, F32)).finish()
        og_ref[...] = o.astype(BF16)
        st_ref[...] = st

    ahead = gla(q1_ref, k1_ref, v1_ref, g1_ref, m1_ref, st_ref[...])

    ahead.step()
    mix = (jnp.dot(og_ref[...], wo_ref[0:GLA_WIDTH, :], preferred_element_type=F32)
           + jnp.dot(on_ref[...], wo_ref[GLA_WIDTH:, :], preferred_element_type=F32))
    x1 = x_ref[...] + mod_ref[2:3, :] * mix
    h = (_rms(x1) * g2_ref[...] * (1.0 + mod_ref[4:5, :]) + mod_ref[3:4, :]).astype(BF16)
    acc = jnp.zeros_like(x1)
    ahead.step()
    for c in range(w1_ref.shape[1] // ff_chunk):
        cs = slice(c * ff_chunk, (c + 1) * ff_chunk)
        a = jnp.maximum(jnp.dot(h, w1_ref[:, cs], preferred_element_type=F32), 0.0)
        ahead.step()
        acc = acc + jnp.dot((a * a).astype(BF16), w2_ref[cs, :], preferred_element_type=F32)
        ahead.step()
    x2 = x1 + mod_ref[5:6, :] * acc
    o_ref[...] = _rms(x2) * gf_ref[...] if final else x2

    o, st = ahead.finish()
    og_ref[...] = o.astype(BF16)
    st_ref[...] = st


def _gla_ffn(x, on, mod, wo, g2, w1, w2, gf, gq, gk, gv, gg, misc, wa_p, ba, gn, final):
    B, S, D = x.shape
    tm = GLA_TILE
    n = S // tm
    row = lambda w: pl.BlockSpec((None, tm, w), lambda b, i: (b, i, 0))
    first = lambda w: pl.BlockSpec((None, tm, w), lambda b, i: (b, 0, 0))
    ahead = lambda w: pl.BlockSpec((None, tm, w), lambda b, i: (b, jnp.minimum(i + 1, n - 1), 0))
    whole = lambda a: pl.BlockSpec(a.shape, lambda b, i: (0,) * a.ndim)
    gla_in = (gq, gk, gv, gg, misc)
    return pl.pallas_call(
        functools.partial(_gla_ffn_kernel, ff_chunk=1024, final=final),
        grid=(B, n),
        in_specs=([row(D), row(on.shape[2]), pl.BlockSpec((None, N_MOD, D), lambda b, i: (b, 0, 0)),
                   whole(wo), whole(g2), whole(w1), whole(w2), whole(gf)]
                  + [first(a.shape[2]) for a in gla_in] + [ahead(a.shape[2]) for a in gla_in]
                  + [whole(wa_p), whole(ba), whole(gn)]),
        out_specs=row(D),
        out_shape=jax.ShapeDtypeStruct((B, S, D), F32),
        scratch_shapes=[pltpu.VMEM((GLA_QK, GLA_WIDTH), F32), pltpu.VMEM((tm, GLA_WIDTH), BF16)],
        compiler_params=_params("parallel", "arbitrary"),
        name="gla_ffn",
    )(x, on, mod, wo, g2, w1, w2, gf, *gla_in, *gla_in, wa_p, ba, gn)


def _reorder_w_in(w_in):
    cuts = [int(v) for v in np.cumsum(IN_SPLITS)[:-1]]
    gq, gk, gv, gg, gr, nq, kc, vc, ks, vs, kw, vw, ngate = jnp.split(w_in, cuts, axis=1)
    pad = jnp.zeros((w_in.shape[0], MISC_W - GLA_RANK - N_GATE), w_in.dtype)
    return jnp.concatenate([gq, gk, gv, gg, nq, kc, vc, ks, vs, kw, vw, gr, ngate, pad], axis=1).astype(BF16)


def kernel(x, c, positions, w_ada, b_ada, norm1_g, w_in, gla_w_a2, gla_b_a, gla_norm_g, nsa_pe_k, nsa_pe_v, cmp_k_w1, cmp_k_w2, cmp_v_w1, cmp_v_w2, nsa_norm_g, w_o, norm2_g, w_ff1, w_ff2, final_norm_g):
    B, S, D = x.shape
    depth = w_in.shape[0]
    half = NSA_DH // 2
    inv = ROPE_THETA ** (-jnp.arange(half, dtype=F32) / half)
    inv = jnp.concatenate([inv, inv]).reshape(1, NSA_DH)
    pos3 = positions.reshape(B, S, 1)
    for l in range(depth):
        mod = _adaln(c, w_ada[l], b_ada[l]).reshape(B, N_MOD, D)
        (gq, gk, gv, gg, nq, kc, vc, ks, vs, kw, vw, misc) = _in_proj(
            x, mod, norm1_g[l].reshape(1, D), pos3, inv, _reorder_w_in(w_in[l]), tm=512)

        k_cmp, v_cmp = _compress(kc, vc, cmp_k_w1[l].astype(BF16), _compress_pe(nsa_pe_k[l]),
                                 cmp_k_w2[l].astype(BF16), cmp_v_w1[l].astype(BF16),
                                 _compress_pe(nsa_pe_v[l]), cmp_v_w2[l].astype(BF16), nb=int(np.gcd(B, 4)))
        o_nsa = _nsa(nq, k_cmp, v_cmp, ks, vs, kw, vw, misc, nsa_norm_g[l].reshape(1, NSA_WIDTH),
                     tq=512, tk=256)

        wa_p = jnp.zeros((MISC_W, GLA_QK), F32).at[0:GLA_RANK].set(gla_w_a2[l])
        x = _gla_ffn(x, o_nsa, mod, w_o[l].astype(BF16), norm2_g[l].reshape(1, D),
                     w_ff1[l].astype(BF16), w_ff2[l].astype(BF16), final_norm_g.reshape(1, D),
                     gq, gk, gv, gg, misc, wa_p, gla_b_a[l].reshape(1, GLA_QK),
                     gla_norm_g[l].reshape(1, GLA_WIDTH), final=(l == depth - 1))
    return x
```

```python
import functools

import numpy as np
import jax
import jax.numpy as jnp
from jax import lax
from jax.experimental import pallas as pl
from jax.experimental.pallas import tpu as pltpu

GLA_HEADS = 4
GLA_DK = 64
GLA_DV = 128
GLA_RANK = 16
GLA_TAU = 16.0
GLA_CHUNK = 64
NSA_HEADS = 4
NSA_KV_HEADS = 2
NSA_DH = 128
NSA_BRANCHES = 3
CMP_BLOCK = 32
CMP_STRIDE = 16
CMP_HIDDEN = 256
SEL_BLOCK = 64
SEL_TOPK = 16
WINDOW = 512
N_MOD = 6
ROPE_THETA = 10000.0
EPS = 1e-6
NEG = -1e30
BIG = 1e30

GLA_QK = GLA_HEADS * GLA_DK
GLA_WIDTH = GLA_HEADS * GLA_DV
NSA_WIDTH = NSA_HEADS * NSA_DH
NSA_KV_WIDTH = NSA_KV_HEADS * NSA_DH
N_GATE = NSA_HEADS * NSA_BRANCHES
IN_SPLITS = (GLA_QK, GLA_QK, GLA_WIDTH, GLA_WIDTH, GLA_RANK, NSA_WIDTH) + (NSA_KV_WIDTH,) * 6 + (N_GATE,)

LANES = 128
SUBLANES = 8
MISC_W = LANES
GATE_OFF = GLA_RANK
GLA_TILE = 256
VMEM_LIMIT = 56 * 1024 * 1024

F32 = jnp.float32
BF16 = jnp.bfloat16
HIGHEST = lax.Precision.HIGHEST

_NT = (((1,), (1,)), ((), ()))
_TN = (((0,), (0,)), ((), ()))


def _params(*sem):
    return pltpu.CompilerParams(dimension_semantics=sem, vmem_limit_bytes=VMEM_LIMIT)


def _rms(x):
    return x * lax.rsqrt(jnp.mean(x * x, axis=-1, keepdims=True) + EPS)


def _split_bf16(x):
    hi = x.astype(BF16)
    return hi, (x - hi.astype(F32)).astype(BF16)


def _t_f32(x):
    return x.astype(F32).T


def _adaln_kernel(c_ref, w_ref, b_ref, o_ref):
    c = c_ref[...]
    a = c * jax.nn.sigmoid(c)
    o_ref[...] = jnp.dot(a, w_ref[...], precision=HIGHEST, preferred_element_type=F32) + b_ref[...]


def _adaln(c, w_ada, b_ada):
    B, D = c.shape
    N = w_ada.shape[1]
    tn = D
    return pl.pallas_call(
        _adaln_kernel,
        grid=(N // tn,),
        in_specs=[pl.BlockSpec((B, D), lambda j: (0, 0)),
                  pl.BlockSpec((D, tn), lambda j: (0, j)),
                  pl.BlockSpec((1, tn), lambda j: (0, j))],
        out_specs=pl.BlockSpec((B, tn), lambda j: (0, j)),
        out_shape=jax.ShapeDtypeStruct((B, N), F32),
        compiler_params=_params("arbitrary"),
        name="adaln",
    )(c, w_ada, b_ada.reshape(1, N))


_PROJ_GROUPS = (("gq", GLA_QK, False), ("gk", GLA_QK, False), ("gv", GLA_WIDTH, False),
                ("gg", GLA_WIDTH, False), ("nq", NSA_WIDTH, True), ("kc", NSA_KV_WIDTH, True),
                ("vc", NSA_KV_WIDTH, False), ("ks", NSA_KV_WIDTH, True), ("vs", NSA_KV_WIDTH, False),
                ("kw", NSA_KV_WIDTH, True), ("vw", NSA_KV_WIDTH, False))
_PROJ_W = sum(g[1] for g in _PROJ_GROUPS) + MISC_W


def _w_in_moves():
    names = ("gq", "gk", "gv", "gg", "gr", "nq", "kc", "vc", "ks", "vs", "kw", "vw", "ngate")
    src = dict(zip(names, np.cumsum((0,) + IN_SPLITS[:-1])))
    width = dict(zip(names, IN_SPLITS))
    moves, dst = [], 0
    for name, w, _ in _PROJ_GROUPS:
        moves.append((int(src[name]), dst, w))
        dst += w
    for name in ("gr", "ngate"):
        moves.append((int(src[name]), dst, width[name]))
        dst += width[name]
    return tuple(moves)


_W_IN_MOVES = _w_in_moves()
_GROUPED = ("kc", "vc")
GRP_W = CMP_STRIDE * NSA_DH


def _in_proj_kernel(x_ref, mod_ref, g_ref, pos_ref, inv_ref, win_ref, *refs):
    n_out = len(_PROJ_GROUPS) + 1
    out_refs, stage_refs, w_ref = refs[:n_out], refs[n_out:-1], refs[-1]

    @pl.when((pl.program_id(0) == 0) & (pl.program_id(1) == 0))
    def _():
        for src, dst, width in _W_IN_MOVES:
            w_ref[:, dst:dst + width] = win_ref[:, src:src + width].astype(BF16)
        pad = _PROJ_W - _W_IN_MOVES[-1][1] - _W_IN_MOVES[-1][2]
        w_ref[:, _PROJ_W - pad:] = jnp.zeros((w_ref.shape[0], pad), BF16)

    tm = x_ref.shape[0]
    x = x_ref[...]
    h = _rms(x) * g_ref[...] * (1.0 + mod_ref[1:2, :]) + mod_ref[0:1, :]
    hb = h.astype(BF16)

    qs = NSA_DH ** -0.5 * float(np.log2(np.e))
    cos = sin = None

    off = 0
    for (name, width, rot), o_ref in zip(_PROJ_GROUPS, out_refs[:-1]):
        if rot and cos is None:
            ang = pos_ref[...].astype(F32) * inv_ref[...]
            lane = lax.broadcasted_iota(jnp.int32, (1, LANES), 1)
            cos = jnp.cos(ang)
            sin = jnp.sin(ang) * jnp.where(lane < NSA_DH // 2, -1.0, 1.0)
        y = jnp.dot(hb, w_ref[:, off:off + width], preferred_element_type=F32)
        grouped = name in _GROUPED
        dst = stage_refs[_GROUPED.index(name)] if grouped else o_ref
        for hd in range(width // NSA_DH if (rot or grouped) else 0):
            yh = y[:, hd * NSA_DH:(hd + 1) * NSA_DH]
            if rot:
                c, s = (cos * qs, sin * qs) if name == "nq" else (cos, sin)
                yh = yh * c + pltpu.roll(yh, NSA_DH // 2, axis=1) * s
            if grouped:
                dst[hd] = yh
            else:
                dst[:, hd * NSA_DH:(hd + 1) * NSA_DH] = yh.astype(dst.dtype)
        if not (rot or grouped):
            dst[...] = y.astype(dst.dtype)
        if grouped:
            for k in range(NSA_KV_HEADS):
                for tok in range(CMP_STRIDE):
                    c0 = k * GRP_W + tok * NSA_DH
                    o_ref[:, c0:c0 + NSA_DH] = dst[k, pl.ds(tok, tm // CMP_STRIDE, stride=CMP_STRIDE), :].astype(
                        o_ref.dtype)
        off += width
    out_refs[-1][...] = jnp.dot(hb, w_ref[:, off:off + MISC_W], preferred_element_type=F32)


def _in_proj(x, mod, norm_g, pos3, inv, w_in, tm):
    B, S, D = x.shape
    row = lambda w: pl.BlockSpec((None, tm, w), lambda b, i: (b, i, 0))
    whole = lambda a: pl.BlockSpec(a.shape, lambda b, i: (0,) * a.ndim)
    out_shape, out_specs = [], []
    for name, w, _ in _PROJ_GROUPS:
        if name in _GROUPED:
            out_shape.append(jax.ShapeDtypeStruct((B, S // CMP_STRIDE, CMP_STRIDE * w), BF16))
            out_specs.append(pl.BlockSpec((None, tm // CMP_STRIDE, CMP_STRIDE * w), lambda b, i: (b, i, 0)))
        else:
            out_shape.append(jax.ShapeDtypeStruct((B, S, w), BF16))
            out_specs.append(row(w))
    out_shape.append(jax.ShapeDtypeStruct((B, S, MISC_W), F32))
    out_specs.append(row(MISC_W))
    return pl.pallas_call(
        _in_proj_kernel,
        grid=(B, S // tm),
        in_specs=[row(D), pl.BlockSpec((None, N_MOD, D), lambda b, i: (b, 0, 0)), whole(norm_g),
                  row(1), whole(inv),
                  pl.BlockSpec(w_in.shape, lambda b, i: (0, 0), pipeline_mode=pl.Buffered(1))],
        out_specs=out_specs,
        out_shape=out_shape,
        scratch_shapes=([pltpu.VMEM((NSA_KV_HEADS, tm, NSA_DH), F32) for _ in _GROUPED]
                        + [pltpu.VMEM((D, _PROJ_W), BF16)]),
        compiler_params=_params("arbitrary", "arbitrary"),
        name="in_proj",
    )(x, mod, norm_g, pos3, inv, w_in)


def _compress_kernel(xk_ref, xv_ref, kw1_ref, kpe_ref, kw2_ref, vw1_ref, vpe_ref, vw2_ref, ok_ref, ov_ref):
    nb, n_grp, _ = xk_ref.shape
    rows = nb * n_grp
    for x_ref, w1_ref, pe_ref, w2_ref, o_ref in ((xk_ref, kw1_ref, kpe_ref, kw2_ref, ok_ref),
                                                 (xv_ref, vw1_ref, vpe_ref, vw2_ref, ov_ref)):
        w1a = w1_ref[0:GRP_W, :]
        w1b = w1_ref[GRP_W:2 * GRP_W, :]
        x = jnp.concatenate([x_ref[:, :, k * GRP_W:(k + 1) * GRP_W].reshape(rows, GRP_W)
                             for k in range(NSA_KV_HEADS)], axis=0)
        bias = (jnp.dot(pe_ref[:, 0:GRP_W], w1a, preferred_element_type=F32)[0:1]
                + jnp.dot(pe_ref[:, GRP_W:2 * GRP_W], w1b, preferred_element_type=F32)[0:1])
        ua = jnp.dot(x, w1a, preferred_element_type=F32)
        ub = jnp.dot(x, w1b, preferred_element_type=F32)
        hid = ua + pltpu.roll(ub, NSA_KV_HEADS * rows - 1, axis=0) + bias
        act = jax.nn.gelu(hid, approximate=True).astype(BF16)
        y = jnp.dot(act, w2_ref[...], preferred_element_type=F32).astype(o_ref.dtype)
        for k in range(NSA_KV_HEADS):
            o_ref[:, :, k * NSA_DH:(k + 1) * NSA_DH] = y[k * rows:(k + 1) * rows].reshape(nb, n_grp, NSA_DH)


def _compress(xk, xv, kw1, kpe, kw2, vw1, vpe, vw2, nb):
    B, n_grp, gw = xk.shape
    W = NSA_KV_WIDTH
    whole = lambda a: pl.BlockSpec(a.shape, lambda b: (0,) * a.ndim)
    xspec = pl.BlockSpec((nb, n_grp, gw), lambda b: (b, 0, 0))
    ospec = pl.BlockSpec((nb, n_grp, W), lambda b: (b, 0, 0))
    return pl.pallas_call(
        _compress_kernel,
        grid=(B // nb,),
        in_specs=[xspec, xspec, whole(kw1), whole(kpe), whole(kw2), whole(vw1), whole(vpe), whole(vw2)],
        out_specs=[ospec, ospec],
        out_shape=[jax.ShapeDtypeStruct((B, n_grp, W), BF16)] * 2,
        compiler_params=_params("parallel"),
        name="compress",
    )(xk, xv, kw1, kpe, kw2, vw1, vpe, vw2)


def _compress_pe(pe):
    return jnp.zeros((8, 2 * GRP_W), F32).at[0].set(pe.reshape(2 * GRP_W)).astype(BF16)


V_AUG = NSA_DH + 16


def _attend(k_rows, qt, vt_aug, masked):
    s = jnp.dot(k_rows(), qt, preferred_element_type=F32)
    if masked is not None:
        s = jnp.where(masked(), NEG, s)
    yield
    m = jnp.max(s, axis=0, keepdims=True)
    p = jnp.exp2((s - m).astype(BF16))
    yield
    return m, jnp.dot(vt_aug(), p, preferred_element_type=F32)


def _pipelined(gens, depth=2, side=()):
    tasks = [_Staged(g) for g in gens]
    for n in range(len(tasks) + depth):
        for t in tasks[max(0, n - depth):n + 1]:
            t.step()
        for s in side:
            s.step()
    return [t.finish() for t in tasks]


def _merge(parts):
    m_all = parts[0][0]
    for m, _ in parts[1:]:
        m_all = jnp.maximum(m_all, m)
    tot = None
    for m, acc in parts:
        w = acc * jnp.exp2(m - m_all)
        tot = w if tot is None else tot + w
    return tot[0:NSA_DH] / tot[NSA_DH:NSA_DH + 1]


def _nsa_step(c, tq, tk, q_ref, kc_ref, vc_ref, kw_ref, misc_ref, gn_ref, o_ref, kaug_ref, vst_ref, vwt_ref):
    S = kw_ref.shape[0]
    n_cmp = kc_ref.shape[0]
    n_sel = S // SEL_BLOCK
    G = NSA_HEADS // NSA_KV_HEADS
    M = G * tq
    q0 = c * tq

    t_q = q0 + lax.broadcasted_iota(jnp.int32, (1, tq), 1)
    t_m = jnp.concatenate([t_q] * G, axis=1)
    key_off = lax.broadcasted_iota(jnp.int32, (tk, 1), 0)

    n_col = lax.broadcasted_iota(jnp.int32, (n_cmp, 1), 0)
    cmp_end = jnp.where(n_col < n_cmp - 1, n_col * CMP_STRIDE + (CMP_BLOCK - 1), jnp.int32(2 ** 30))
    cmp_valid = cmp_end <= t_m
    any_valid = (t_m >= CMP_BLOCK - 1).astype(F32)
    jj = lax.broadcasted_iota(jnp.int32, (n_sel, n_cmp), 0) * SEL_BLOCK
    nn = lax.broadcasted_iota(jnp.int32, (n_sel, n_cmp), 1) * CMP_STRIDE
    ov_t = jnp.maximum(jnp.minimum(nn + CMP_BLOCK, jj + SEL_BLOCK) - jnp.maximum(nn, jj), 0)
    ov_t = (ov_t.astype(F32) * (1.0 / CMP_BLOCK)).astype(BF16)
    j_row = lax.broadcasted_iota(jnp.int32, (n_sel, tq), 0)
    blk_t = lax.shift_right_logical(q0 + lax.broadcasted_iota(jnp.int32, (n_sel, tq), 1), 6)
    forced = (j_row == 0) | (j_row == blk_t) | (j_row == blk_t - 1)
    in_past = j_row <= blk_t

    heads = range(NSA_KV_HEADS)
    qts = [jnp.concatenate([_t_f32(q_ref[:, (k * G + g) * NSA_DH:(k * G + g + 1) * NSA_DH])
                            for g in range(G)], axis=1).astype(BF16) for k in heads]

    def select(k):
        hs = slice(k * NSA_DH, (k + 1) * NSA_DH)
        s = jnp.dot(kc_ref[:, hs], qts[k], preferred_element_type=F32)
        s = jnp.where(cmp_valid, s, NEG)
        yield
        e = jnp.exp2(s - jnp.max(s, axis=0, keepdims=True))
        p = e / jnp.sum(e, axis=0, keepdims=True) * any_valid
        yield
        o_cmp = jnp.dot(_t_f32(vc_ref[:, hs]).astype(BF16), p.astype(BF16), preferred_element_type=F32)
        p_grp = p[:, 0:tq]
        for g in range(1, G):
            p_grp = p_grp + p[:, g * tq:(g + 1) * tq]
        p_hi, p_lo = _split_bf16(p_grp)
        imp = (jnp.dot(ov_t, p_hi, preferred_element_type=F32)
               + jnp.dot(ov_t, p_lo, preferred_element_type=F32))
        imp = jnp.where(forced, BIG, jnp.where(in_past, imp, NEG))
        yield
        rank = []
        for r0 in range(0, n_sel, SUBLANES):
            blk = imp[r0:r0 + SUBLANES, :]
            cnt = jnp.zeros(blk.shape, F32)
            for j in range(n_sel):
                row = imp[j:j + 1, :]
                if j < r0:
                    cnt = cnt + jnp.where(row >= blk, 1.0, 0.0)
                elif j >= r0 + SUBLANES - 1:
                    cnt = cnt + jnp.where(row > blk, 1.0, 0.0)
                else:
                    below = r0 + lax.broadcasted_iota(jnp.int32, blk.shape, 0) > j
                    cnt = cnt + jnp.where(below, jnp.where(row >= blk, 1.0, 0.0), jnp.where(row > blk, 1.0, 0.0))
            rank.append(cnt)
            yield
        rank = jnp.concatenate(rank, axis=0)
        bias = jnp.where(rank < float(min(SEL_TOPK, n_sel)), 0.0, NEG).astype(BF16)
        return o_cmp, jnp.concatenate([qts[k], jnp.concatenate([bias] * G, axis=1),
                                       jnp.zeros((NSA_DH - n_sel, M), BF16)], axis=0)

    subs = range(tq // tk)
    col_off = jnp.concatenate([lax.broadcasted_iota(jnp.int32, (1, tk), 1)] * G, axis=1)
    after = lambda: key_off > col_off
    too_far = lambda: key_off <= col_off

    def cols(x, j):
        return jnp.concatenate([x[:, g * tq + j * tk:g * tq + (j + 1) * tk] for g in range(G)], axis=1)

    def sel_tasks(k, j, qta):
        first, diag = (0, q0 // tk - 1) if j is None else (q0 // tk, (q0 + j * tk) // tk)
        for kt in range(first, diag + 1):
            yield _attend(lambda kt=kt: kaug_ref[k, kt * tk:(kt + 1) * tk, :], qta,
                          lambda kt=kt: vst_ref[k, kt], after if kt == diag and j is not None else None)

    def win_tasks(k, j, qt):
        hs = slice(k * NSA_DH, (k + 1) * NSA_DH)
        diag = (q0 + j * tk) // tk
        back = WINDOW // tk
        for kt in range(max(diag - back, 0), diag + 1):
            masked = after if kt == diag else too_far if kt == diag - back else None
            yield _attend(lambda kt=kt: kw_ref[kt * tk:(kt + 1) * tk, hs], qt, lambda kt=kt: vwt_ref[k, kt], masked)

    def run(task_lists, side=()):
        flat = [(key, t) for key, tasks in task_lists.items() for t in tasks]
        parts = {key: [] for key in task_lists}
        for (key, _), part in zip(flat, _pipelined([t for _, t in flat], side=side)):
            parts[key].append(part)
        merged = {}
        for k in heads:
            for j in subs:
                shared = [(cols(m, j), cols(acc, j)) for m, acc in parts.get((k, None), [])]
                merged[k, j] = _merge(shared + parts[k, j])
        return [jnp.concatenate([merged[k, j][:, g * tk:(g + 1) * tk] for g in range(G) for j in subs], axis=1)
                for k in heads]

    selects = [_Staged(select(k)) for k in heads]
    o_win = run({(k, j): list(win_tasks(k, j, cols(qts[k], j))) for k in heads for j in subs}, side=selects)
    o_cmp, qtas = zip(*[s.finish() for s in selects])
    o_sel = run({(k, j): list(sel_tasks(k, j, qtas[k] if j is None else cols(qtas[k], j)))
                 for k in heads for j in (None, *subs)})


    gates_t = _t_f32(jax.nn.sigmoid(misc_ref[...]))
    heads = []
    for k in range(NSA_KV_HEADS):
        for g in range(G):
            gs = slice(g * tq, (g + 1) * tq)
            gl = GATE_OFF + (k * G + g) * NSA_BRANCHES
            heads.append(gates_t[gl:gl + 1, :] * o_cmp[k][:, gs] + gates_t[gl + 1:gl + 2, :] * o_sel[k][:, gs]
                         + gates_t[gl + 2:gl + 3, :] * o_win[k][:, gs])
    ssq = heads[0] * heads[0]
    for o in heads[1:]:
        ssq = ssq + o * o
    inv = lax.rsqrt(jnp.sum(ssq, axis=0, keepdims=True) * (1.0 / NSA_WIDTH) + EPS)
    for hq, o in enumerate(heads):
        cs = slice(hq * NSA_DH, (hq + 1) * NSA_DH)
        o_ref[:, cs] = ((o * inv).T * gn_ref[:, cs]).astype(o_ref.dtype)


def _nsa_kernel(q_ref, kc_ref, vc_ref, ks_ref, vs_ref, kw_ref, vw_ref, misc_ref, gn_ref, o_ref,
                kaug_ref, vst_ref, vwt_ref):
    tq = q_ref.shape[0]
    tk = vst_ref.shape[3]
    S = ks_ref.shape[0]
    i = pl.program_id(1)

    @pl.when(i == 0)
    def _():
        pos = lax.broadcasted_iota(jnp.int32, (S, LANES), 0)
        lane = lax.broadcasted_iota(jnp.int32, (S, LANES), 1)
        onehot = jnp.where(lax.shift_right_logical(pos, 6) == lane, 1.0, 0.0).astype(BF16)
        row = lax.broadcasted_iota(jnp.int32, (V_AUG - NSA_DH, tk), 0)
        ones_rows = jnp.where(row == 0, 1.0, 0.0).astype(BF16)
        for k in range(NSA_KV_HEADS):
            hs = slice(k * NSA_DH, (k + 1) * NSA_DH)
            kaug_ref[k, :, 0:NSA_DH] = ks_ref[:, hs]
            kaug_ref[k, :, NSA_DH:2 * NSA_DH] = onehot
            for kt in range(S // tk):
                rows = slice(kt * tk, (kt + 1) * tk)
                for src, dst in ((vs_ref, vst_ref), (vw_ref, vwt_ref)):
                    dst[k, kt, 0:NSA_DH, :] = _t_f32(src[rows, hs]).astype(BF16)
                    dst[k, kt, NSA_DH:V_AUG, :] = ones_rows

    for c in range(S // tq):
        pl.when(i == c)(functools.partial(_nsa_step, c, tq, tk, q_ref, kc_ref, vc_ref, kw_ref, misc_ref,
                                          gn_ref, o_ref, kaug_ref, vst_ref, vwt_ref))


def _nsa(nq, kcmp, vcmp, ks, vs, kw, vw, misc, gn, tq, tk):
    B, S, _ = nq.shape
    n_cmp = kcmp.shape[1]
    assert n_cmp <= LANES and S // SEL_BLOCK <= NSA_DH and SEL_BLOCK == 64
    assert tq % tk == 0 and tq <= WINDOW and WINDOW % tk == 0
    row = lambda w: pl.BlockSpec((None, tq, w), lambda b, i: (b, i, 0))
    seq = lambda n: pl.BlockSpec((None, n, NSA_KV_WIDTH), lambda b, i: (b, 0, 0))
    vt_scratch = pltpu.VMEM((NSA_KV_HEADS, S // tk, V_AUG, tk), BF16)
    return pl.pallas_call(
        _nsa_kernel,
        grid=(B, S // tq),
        in_specs=[row(NSA_WIDTH), seq(n_cmp), seq(n_cmp), seq(S), seq(S), seq(S), seq(S), row(MISC_W),
                  pl.BlockSpec(gn.shape, lambda b, i: (0, 0))],
        out_specs=row(NSA_WIDTH),
        out_shape=jax.ShapeDtypeStruct((B, S, NSA_WIDTH), BF16),
        scratch_shapes=[pltpu.VMEM((NSA_KV_HEADS, S, 2 * NSA_DH), BF16), vt_scratch, vt_scratch],
        compiler_params=_params("parallel", "arbitrary"),
        name="nsa",
    )(nq, kcmp, vcmp, ks, vs, kw, vw, misc, gn)


def _log_sigmoid(z):
    return jnp.minimum(z, 0.0) - jnp.log(1.0 + jnp.exp(-jnp.abs(z)))


class _Staged:
    def __init__(self, gen):
        self.gen, self.value, self.done = gen, None, False

    def step(self):
        if not self.done:
            try:
                next(self.gen)
            except StopIteration as stop:
                self.value, self.done = stop.value, True

    def finish(self):
        while not self.done:
            self.step()
        return self.value


def _gla_tile(q, k, v, g, misc, wa, ba, gn, st):
    C = GLA_CHUNK
    T = q.shape[0]
    n_chunk = T // C
    r = lax.broadcasted_iota(jnp.int32, (T, T), 0)
    c = lax.broadcasted_iota(jnp.int32, (T, T), 1)
    causal = (r >= c) & (r - c <= (r & (C - 1)))
    tri = jnp.where(causal, 1.0, 0.0).astype(BF16)
    sr = lax.broadcasted_iota(jnp.int32, st.shape, 0)
    sc = lax.broadcasted_iota(jnp.int32, st.shape, 1)
    own_head = (lax.shift_right_logical(sr, GLA_DK.bit_length() - 1)
                == lax.shift_right_logical(sc, GLA_DV.bit_length() - 1))

    r_hi, r_lo = _split_bf16(misc)
    w_hi, w_lo = _split_bf16(wa)
    z = (jnp.dot(r_hi, w_hi, preferred_element_type=F32) + jnp.dot(r_lo, w_hi, preferred_element_type=F32)
         + jnp.dot(r_hi, w_lo, preferred_element_type=F32) + ba)
    la_hi, la_lo = _split_bf16(_log_sigmoid(z) * (1.0 / GLA_TAU))
    yield
    b = (jnp.dot(tri, la_hi, preferred_element_type=F32)
         + jnp.dot(tri, la_lo, preferred_element_type=F32))
    b_lasts = [b[(ci + 1) * C - 1:(ci + 1) * C, :] for ci in range(n_chunk)]
    b_last = jnp.concatenate([jnp.broadcast_to(bl, (C, bl.shape[1])) for bl in b_lasts], axis=0)
    qf = q.astype(F32)
    kf = k.astype(F32)
    q_in = (qf * GLA_DK ** -0.5 * jnp.exp(b)).astype(BF16)
    k_in = (kf * jnp.exp(-b)).astype(BF16)
    k_dec = (kf * jnp.exp(b_last - b)).astype(BF16)
    pad = jnp.zeros((LANES - n_chunk, b.shape[1]), F32)
    dec_cols = jnp.exp(jnp.concatenate(b_lasts + [pad], axis=0).T)
    yield

    o_intra = []
    for h in range(GLA_HEADS):
        ks = slice(h * GLA_DK, (h + 1) * GLA_DK)
        att = lax.dot_general(q_in[:, ks], k_in[:, ks], _NT, preferred_element_type=F32)
        att = jnp.where(causal, att, 0.0).astype(BF16)
        o_intra.append(jnp.dot(att, v[:, h * GLA_DV:(h + 1) * GLA_DV], preferred_element_type=F32))
        yield

    o_inter = []
    for ci in range(n_chunk):
        rows = slice(ci * C, (ci + 1) * C)
        o_inter.append(jnp.dot(q_in[rows], st.astype(BF16), preferred_element_type=F32))
        d_st = lax.dot_general(k_dec[rows], v[rows], _TN, preferred_element_type=F32)
        st = st * dec_cols[:, ci:ci + 1] + jnp.where(own_head, d_st, 0.0)
        yield
    o_inter = jnp.concatenate(o_inter, axis=0)
    gate = g.astype(F32)
    gate = gate * jax.nn.sigmoid(gate)
    outs = []
    for h in range(GLA_HEADS):
        vs = slice(h * GLA_DV, (h + 1) * GLA_DV)
        outs.append(_rms(o_intra[h] + o_inter[:, vs]) * gn[:, vs] * gate[:, vs])
    return jnp.concatenate(outs, axis=1), st


def _gla_ffn_kernel(x_ref, on_ref, mod_ref, wo_ref, g2_ref, w1_ref, w2_ref, gf_ref,
                    q0_ref, k0_ref, v0_ref, g0_ref, m0_ref, q1_ref, k1_ref, v1_ref, g1_ref, m1_ref,
                    wa_ref, ba_ref, gn_ref, o_ref, st_ref, og_ref, *, ff_chunk, final):
    def gla(q_ref, k_ref, v_ref, g_ref, m_ref, st):
        return _Staged(_gla_tile(q_ref[...], k_ref[...], v_ref[...], g_ref[...], m_ref[...],
                                 wa_ref[...], ba_ref[...], gn_ref[...], st))

    @pl.when(pl.program_id(1) == 0)
    def _():
        o, st = gla(q0_ref, k0_ref, v0_ref, g0_ref, m0_ref, jnp.zeros(st_ref.shape, F32)).finish()
        og_ref[...] = o.astype(BF16)
        st_ref[...] = st

    ahead = gla(q1_ref, k1_ref, v1_ref, g1_ref, m1_ref, st_ref[...])

    ahead.step()
    mix = (jnp.dot(og_ref[...], wo_ref[0:GLA_WIDTH, :], preferred_element_type=F32)
           + jnp.dot(on_ref[...], wo_ref[GLA_WIDTH:, :], preferred_element_type=F32))
    x1 = x_ref[...] + mod_ref[2:3, :] * mix
    h = (_rms(x1) * g2_ref[...] * (1.0 + mod_ref[4:5, :]) + mod_ref[3:4, :]).astype(BF16)
    acc = jnp.zeros_like(x1)
    ahead.step()
    for c in range(w1_ref.shape[1] // ff_chunk):
        cs = slice(c * ff_chunk, (c + 1) * ff_chunk)
        a = jnp.maximum(jnp.dot(h, w1_ref[:, cs], preferred_element_type=F32), 0.0)
        ahead.step()
        acc = acc + jnp.dot((a * a).astype(BF16), w2_ref[cs, :], preferred_element_type=F32)
        ahead.step()
    x2 = x1 + mod_ref[5:6, :] * acc
    o_ref[...] = _rms(x2) * gf_ref[...] if final else x2

    o, st = ahead.finish()
    og_ref[...] = o.astype(BF16)
    st_ref[...] = st


def _gla_ffn(x, on, mod, wo, g2, w1, w2, gf, gq, gk, gv, gg, misc, wa_p, ba, gn, final):
    B, S, D = x.shape
    tm = GLA_TILE
    n = S // tm
    row = lambda w: pl.BlockSpec((None, tm, w), lambda b, i: (b, i, 0))
    first = lambda w: pl.BlockSpec((None, tm, w), lambda b, i: (b, 0, 0))
    ahead = lambda w: pl.BlockSpec((None, tm, w), lambda b, i: (b, jnp.minimum(i + 1, n - 1), 0))
    whole = lambda a: pl.BlockSpec(a.shape, lambda b, i: (0,) * a.ndim)
    gla_in = (gq, gk, gv, gg, misc)
    return pl.pallas_call(
        functools.partial(_gla_ffn_kernel, ff_chunk=1024, final=final),
        grid=(B, n),
        in_specs=([row(D), row(on.shape[2]), pl.BlockSpec((None, N_MOD, D), lambda b, i: (b, 0, 0)),
                   whole(wo), whole(g2), whole(w1), whole(w2), whole(gf)]
                  + [first(a.shape[2]) for a in gla_in] + [ahead(a.shape[2]) for a in gla_in]
                  + [whole(wa_p), whole(ba), whole(gn)]),
        out_specs=row(D),
        out_shape=jax.ShapeDtypeStruct((B, S, D), F32),
        scratch_shapes=[pltpu.VMEM((GLA_QK, GLA_WIDTH), F32), pltpu.VMEM((tm, GLA_WIDTH), BF16)],
        compiler_params=_params("parallel", "arbitrary"),
        name="gla_ffn",
    )(x, on, mod, wo, g2, w1, w2, gf, *gla_in, *gla_in, wa_p, ba, gn)


def kernel(x, c, positions, w_ada, b_ada, norm1_g, w_in, gla_w_a2, gla_b_a, gla_norm_g, nsa_pe_k, nsa_pe_v, cmp_k_w1, cmp_k_w2, cmp_v_w1, cmp_v_w2, nsa_norm_g, w_o, norm2_g, w_ff1, w_ff2, final_norm_g):
    B, S, D = x.shape
    depth = w_in.shape[0]
    half = NSA_DH // 2
    inv = ROPE_THETA ** (-jnp.arange(half, dtype=F32) / half)
    inv = jnp.concatenate([inv, inv]).reshape(1, NSA_DH)
    pos3 = positions.reshape(B, S, 1)
    for l in range(depth):
        mod = _adaln(c, w_ada[l], b_ada[l]).reshape(B, N_MOD, D)
        (gq, gk, gv, gg, nq, kc, vc, ks, vs, kw, vw, misc) = _in_proj(
            x, mod, norm1_g[l].reshape(1, D), pos3, inv, w_in[l], tm=512)

        k_cmp, v_cmp = _compress(kc, vc, cmp_k_w1[l].astype(BF16), _compress_pe(nsa_pe_k[l]),
                                 cmp_k_w2[l].astype(BF16), cmp_v_w1[l].astype(BF16),
                                 _compress_pe(nsa_pe_v[l]), cmp_v_w2[l].astype(BF16), nb=int(np.gcd(B, 4)))
        o_nsa = _nsa(nq, k_cmp, v_cmp, ks, vs, kw, vw, misc, nsa_norm_g[l].reshape(1, NSA_WIDTH),
                     tq=512, tk=256)

        wa_p = jnp.zeros((MISC_W, GLA_QK), F32).at[0:GLA_RANK].set(gla_w_a2[l])
        x = _gla_ffn(x, o_nsa, mod, w_o[l].astype(BF16), norm2_g[l].reshape(1, D),
                     w_ff1[l].astype(BF16), w_ff2[l].astype(BF16), final_norm_g.reshape(1, D),
                     gq, gk, gv, gg, misc, wa_p, gla_b_a[l].reshape(1, GLA_QK),
                     gla_norm_g[l].reshape(1, GLA_WIDTH), final=(l == depth - 1))
    return x
```

```python
import functools

import numpy as np
import jax
import jax.numpy as jnp
from jax import lax
from jax.experimental import pallas as pl
from jax.experimental.pallas import tpu as pltpu

GLA_HEADS = 4
GLA_DK = 64
GLA_DV = 128
GLA_RANK = 16
GLA_TAU = 16.0
GLA_CHUNK = 64
NSA_HEADS = 4
NSA_KV_HEADS = 2
NSA_DH = 128
NSA_BRANCHES = 3
CMP_BLOCK = 32
CMP_STRIDE = 16
CMP_HIDDEN = 256
SEL_BLOCK = 64
SEL_TOPK = 16
WINDOW = 512
N_MOD = 6
ROPE_THETA = 10000.0
EPS = 1e-6
NEG = -1e30
BIG = 1e30

GLA_QK = GLA_HEADS * GLA_DK
GLA_WIDTH = GLA_HEADS * GLA_DV
NSA_WIDTH = NSA_HEADS * NSA_DH
NSA_KV_WIDTH = NSA_KV_HEADS * NSA_DH
N_GATE = NSA_HEADS * NSA_BRANCHES
IN_SPLITS = (GLA_QK, GLA_QK, GLA_WIDTH, GLA_WIDTH, GLA_RANK, NSA_WIDTH) + (NSA_KV_WIDTH,) * 6 + (N_GATE,)

LANES = 128
SUBLANES = 8
MISC_W = LANES
GATE_OFF = GLA_RANK
GLA_TILE = 256
VMEM_LIMIT = 56 * 1024 * 1024

F32 = jnp.float32
BF16 = jnp.bfloat16

_NT = (((1,), (1,)), ((), ()))
_TN = (((0,), (0,)), ((), ()))


def _params(*sem):
    return pltpu.CompilerParams(dimension_semantics=sem, vmem_limit_bytes=VMEM_LIMIT)


def _rms(x):
    return x * lax.rsqrt(jnp.mean(x * x, axis=-1, keepdims=True) + EPS)


def _split_bf16(x):
    hi = x.astype(BF16)
    return hi, (x - hi.astype(F32)).astype(BF16)


def _t_f32(x):
    return x.astype(F32).T


def _adaln_kernel(c_ref, w_ref, b_ref, o_ref):
    c = c_ref[...]
    n = c.shape[0]
    a_hi, a_lo = _split_bf16(c * jax.nn.sigmoid(c))
    w_hi, w_lo = _split_bf16(w_ref[...])
    y = jnp.dot(jnp.concatenate([a_hi, a_lo], axis=0), w_hi, preferred_element_type=F32)
    o_ref[...] = y[0:n] + y[n:2 * n] + jnp.dot(a_hi, w_lo, preferred_element_type=F32) + b_ref[...]


def _adaln(c, w_ada, b_ada):
    B, D = c.shape
    N = w_ada.shape[1]
    tn = D
    return pl.pallas_call(
        _adaln_kernel,
        grid=(N // tn,),
        in_specs=[pl.BlockSpec((B, D), lambda j: (0, 0)),
                  pl.BlockSpec((D, tn), lambda j: (0, j)),
                  pl.BlockSpec((1, tn), lambda j: (0, j))],
        out_specs=pl.BlockSpec((B, tn), lambda j: (0, j)),
        out_shape=jax.ShapeDtypeStruct((B, N), F32),
        compiler_params=_params("arbitrary"),
        name="adaln",
    )(c, w_ada, b_ada.reshape(1, N))


_PROJ_GROUPS = (("gq", GLA_QK, False), ("gk", GLA_QK, False), ("gv", GLA_WIDTH, False),
                ("gg", GLA_WIDTH, False), ("nq", NSA_WIDTH, True), ("kc", NSA_KV_WIDTH, True),
                ("vc", NSA_KV_WIDTH, False), ("ks", NSA_KV_WIDTH, True), ("vs", NSA_KV_WIDTH, False),
                ("kw", NSA_KV_WIDTH, True), ("vw", NSA_KV_WIDTH, False))
_PROJ_W = sum(g[1] for g in _PROJ_GROUPS) + MISC_W


def _w_in_moves():
    names = ("gq", "gk", "gv", "gg", "gr", "nq", "kc", "vc", "ks", "vs", "kw", "vw", "ngate")
    src = dict(zip(names, np.cumsum((0,) + IN_SPLITS[:-1])))
    width = dict(zip(names, IN_SPLITS))
    moves, dst = [], 0
    for name, w, _ in _PROJ_GROUPS:
        moves.append((int(src[name]), dst, w))
        dst += w
    for name in ("gr", "ngate"):
        moves.append((int(src[name]), dst, width[name]))
        dst += width[name]
    return tuple(moves)


_W_IN_MOVES = _w_in_moves()
_GROUPED = ("kc", "vc")
GRP_W = CMP_STRIDE * NSA_DH


def _in_proj_kernel(x_ref, mod_ref, g_ref, pos_ref, inv_ref, win_ref, *refs):
    n_out = len(_PROJ_GROUPS) + 1
    out_refs, stage_refs, w_ref = refs[:n_out], refs[n_out:-1], refs[-1]

    @pl.when((pl.program_id(0) == 0) & (pl.program_id(1) == 0))
    def _():
        for src, dst, width in _W_IN_MOVES:
            w_ref[:, dst:dst + width] = win_ref[:, src:src + width].astype(BF16)
        pad = _PROJ_W - _W_IN_MOVES[-1][1] - _W_IN_MOVES[-1][2]
        w_ref[:, _PROJ_W - pad:] = jnp.zeros((w_ref.shape[0], pad), BF16)

    tm = x_ref.shape[0]
    x = x_ref[...]
    h = _rms(x) * g_ref[...] * (1.0 + mod_ref[1:2, :]) + mod_ref[0:1, :]
    hb = h.astype(BF16)

    qs = NSA_DH ** -0.5 * float(np.log2(np.e))
    cos = sin = None

    off = 0
    for (name, width, rot), o_ref in zip(_PROJ_GROUPS, out_refs[:-1]):
        if rot and cos is None:
            pos = pos_ref[...].astype(F32)
            pos = jnp.concatenate([jnp.broadcast_to(pos[r:r + 1, :], (LANES, LANES)).T
                                   for r in range(pos.shape[0])], axis=0)
            ang = pos * inv_ref[...]
            lane = lax.broadcasted_iota(jnp.int32, (1, LANES), 1)
            cos = jnp.cos(ang)
            sin = jnp.sin(ang) * jnp.where(lane < NSA_DH // 2, -1.0, 1.0)
        y = jnp.dot(hb, w_ref[:, off:off + width], preferred_element_type=F32)
        grouped = name in _GROUPED
        dst = stage_refs[_GROUPED.index(name)] if grouped else o_ref
        for hd in range(width // NSA_DH if (rot or grouped) else 0):
            yh = y[:, hd * NSA_DH:(hd + 1) * NSA_DH]
            if rot:
                c, s = (cos * qs, sin * qs) if name == "nq" else (cos, sin)
                yh = yh * c + pltpu.roll(yh, NSA_DH // 2, axis=1) * s
            if grouped:
                dst[hd] = yh
            else:
                dst[:, hd * NSA_DH:(hd + 1) * NSA_DH] = yh.astype(dst.dtype)
        if not (rot or grouped):
            dst[...] = y.astype(dst.dtype)
        if grouped:
            for k in range(NSA_KV_HEADS):
                for tok in range(CMP_STRIDE):
                    c0 = k * GRP_W + tok * NSA_DH
                    o_ref[:, c0:c0 + NSA_DH] = dst[k, pl.ds(tok, tm // CMP_STRIDE, stride=CMP_STRIDE), :].astype(
                        o_ref.dtype)
        off += width
    out_refs[-1][...] = jnp.dot(hb, w_ref[:, off:off + MISC_W], preferred_element_type=F32)


def _in_proj(x, mod, norm_g, positions, inv, w_in, layer, tm):
    B, S, D = x.shape
    pos4 = positions.reshape(B, S // tm, tm // LANES, LANES)
    row = lambda w: pl.BlockSpec((None, tm, w), lambda b, i: (b, i, 0))
    whole = lambda a: pl.BlockSpec(a.shape, lambda b, i: (0,) * a.ndim)
    out_shape, out_specs = [], []
    for name, w, _ in _PROJ_GROUPS:
        if name in _GROUPED:
            out_shape.append(jax.ShapeDtypeStruct((B, S // CMP_STRIDE, CMP_STRIDE * w), BF16))
            out_specs.append(pl.BlockSpec((None, tm // CMP_STRIDE, CMP_STRIDE * w), lambda b, i: (b, i, 0)))
        else:
            out_shape.append(jax.ShapeDtypeStruct((B, S, w), BF16))
            out_specs.append(row(w))
    out_shape.append(jax.ShapeDtypeStruct((B, S, MISC_W), F32))
    out_specs.append(row(MISC_W))
    return pl.pallas_call(
        _in_proj_kernel,
        grid=(B, S // tm),
        in_specs=[row(D), pl.BlockSpec((None, N_MOD, D), lambda b, i: (b, 0, 0)), whole(norm_g),
                  pl.BlockSpec((None, None, tm // LANES, LANES), lambda b, i: (b, i, 0, 0)), whole(inv),
                  pl.BlockSpec((None,) + w_in.shape[1:], lambda b, i: (layer, 0, 0),
                               pipeline_mode=pl.Buffered(1))],
        out_specs=out_specs,
        out_shape=out_shape,
        scratch_shapes=([pltpu.VMEM((NSA_KV_HEADS, tm, NSA_DH), F32) for _ in _GROUPED]
                        + [pltpu.VMEM((D, _PROJ_W), BF16)]),
        compiler_params=_params("arbitrary", "arbitrary"),
        name="in_proj",
    )(x, mod, norm_g, pos4, inv, w_in)


def _compress_kernel(xk_ref, xv_ref, kw1_ref, kpe_ref, kw2_ref, vw1_ref, vpe_ref, vw2_ref, ok_ref, ov_ref):
    nb, n_grp, _ = xk_ref.shape
    rows = nb * n_grp
    for x_ref, w1_ref, pe_ref, w2_ref, o_ref in ((xk_ref, kw1_ref, kpe_ref, kw2_ref, ok_ref),
                                                 (xv_ref, vw1_ref, vpe_ref, vw2_ref, ov_ref)):
        w1a = w1_ref[0:GRP_W, :]
        w1b = w1_ref[GRP_W:2 * GRP_W, :]
        x = jnp.concatenate([x_ref[:, :, k * GRP_W:(k + 1) * GRP_W].reshape(rows, GRP_W)
                             for k in range(NSA_KV_HEADS)], axis=0)
        bias = (jnp.dot(pe_ref[:, 0:GRP_W], w1a, preferred_element_type=F32)[0:1]
                + jnp.dot(pe_ref[:, GRP_W:2 * GRP_W], w1b, preferred_element_type=F32)[0:1])
        ua = jnp.dot(x, w1a, preferred_element_type=F32)
        ub = jnp.dot(x, w1b, preferred_element_type=F32)
        hid = ua + pltpu.roll(ub, NSA_KV_HEADS * rows - 1, axis=0) + bias
        act = jax.nn.gelu(hid, approximate=True).astype(BF16)
        y = jnp.dot(act, w2_ref[...], preferred_element_type=F32).astype(o_ref.dtype)
        for k in range(NSA_KV_HEADS):
            o_ref[:, :, k * NSA_DH:(k + 1) * NSA_DH] = y[k * rows:(k + 1) * rows].reshape(nb, n_grp, NSA_DH)


def _compress(xk, xv, kw1, kpe, kw2, vw1, vpe, vw2, nb):
    B, n_grp, gw = xk.shape
    W = NSA_KV_WIDTH
    whole = lambda a: pl.BlockSpec(a.shape, lambda b: (0,) * a.ndim)
    xspec = pl.BlockSpec((nb, n_grp, gw), lambda b: (b, 0, 0))
    ospec = pl.BlockSpec((nb, n_grp, W), lambda b: (b, 0, 0))
    return pl.pallas_call(
        _compress_kernel,
        grid=(B // nb,),
        in_specs=[xspec, xspec, whole(kw1), whole(kpe), whole(kw2), whole(vw1), whole(vpe), whole(vw2)],
        out_specs=[ospec, ospec],
        out_shape=[jax.ShapeDtypeStruct((B, n_grp, W), BF16)] * 2,
        compiler_params=_params("parallel"),
        name="compress",
    )(xk, xv, kw1, kpe, kw2, vw1, vpe, vw2)


def _compress_pe(pe):
    return jnp.zeros((8, 2 * GRP_W), F32).at[0].set(pe.reshape(2 * GRP_W)).astype(BF16)


V_AUG = NSA_DH + 16


def _attend(k_rows, qt, vt_aug, masked):
    s = jnp.dot(k_rows(), qt, preferred_element_type=F32)
    if masked is not None:
        s = jnp.where(masked(), NEG, s)
    yield
    m = jnp.max(s, axis=0, keepdims=True)
    p = jnp.exp2((s - m).astype(BF16))
    yield
    return m, jnp.dot(vt_aug(), p, preferred_element_type=F32)


def _pipelined(gens, depth=2, side=()):
    tasks = [_Staged(g) for g in gens]
    for n in range(len(tasks) + depth):
        for t in tasks[max(0, n - depth):n + 1]:
            t.step()
        for s in side:
            s.step()
    return [t.finish() for t in tasks]


def _merge(parts):
    m_all = parts[0][0]
    for m, _ in parts[1:]:
        m_all = jnp.maximum(m_all, m)
    tot = None
    for m, acc in parts:
        w = acc * jnp.exp2(m - m_all)
        tot = w if tot is None else tot + w
    return tot[0:NSA_DH] / tot[NSA_DH:NSA_DH + 1]


def _nsa_step(c, tq, tk, q_ref, kc_ref, vc_ref, kw_ref, misc_ref, gn_ref, o_ref, kaug_ref, vst_ref, vwt_ref):
    S = kw_ref.shape[0]
    n_cmp = kc_ref.shape[0]
    n_sel = S // SEL_BLOCK
    G = NSA_HEADS // NSA_KV_HEADS
    M = G * tq
    q0 = c * tq

    t_q = q0 + lax.broadcasted_iota(jnp.int32, (1, tq), 1)
    t_m = jnp.concatenate([t_q] * G, axis=1)
    key_off = lax.broadcasted_iota(jnp.int32, (tk, 1), 0)

    n_col = lax.broadcasted_iota(jnp.int32, (n_cmp, 1), 0)
    cmp_end = jnp.where(n_col < n_cmp - 1, n_col * CMP_STRIDE + (CMP_BLOCK - 1), jnp.int32(2 ** 30))
    cmp_valid = cmp_end <= t_m
    any_valid = (t_m >= CMP_BLOCK - 1).astype(F32)
    jj = lax.broadcasted_iota(jnp.int32, (n_sel, n_cmp), 0) * SEL_BLOCK
    nn = lax.broadcasted_iota(jnp.int32, (n_sel, n_cmp), 1) * CMP_STRIDE
    ov_t = jnp.maximum(jnp.minimum(nn + CMP_BLOCK, jj + SEL_BLOCK) - jnp.maximum(nn, jj), 0)
    ov_t = (ov_t.astype(F32) * (1.0 / CMP_BLOCK)).astype(BF16)
    j_row = lax.broadcasted_iota(jnp.int32, (n_sel, tq), 0)
    blk_t = lax.shift_right_logical(q0 + lax.broadcasted_iota(jnp.int32, (n_sel, tq), 1), 6)
    forced = (j_row == 0) | (j_row == blk_t) | (j_row == blk_t - 1)
    in_past = j_row <= blk_t

    heads = range(NSA_KV_HEADS)
    qts = [jnp.concatenate([_t_f32(q_ref[:, (k * G + g) * NSA_DH:(k * G + g + 1) * NSA_DH])
                            for g in range(G)], axis=1).astype(BF16) for k in heads]

    def select(k):
        hs = slice(k * NSA_DH, (k + 1) * NSA_DH)
        s = jnp.dot(kc_ref[:, hs], qts[k], preferred_element_type=F32)
        s = jnp.where(cmp_valid, s, NEG)
        yield
        e = jnp.exp2(s - jnp.max(s, axis=0, keepdims=True))
        p = e / jnp.sum(e, axis=0, keepdims=True) * any_valid
        yield
        o_cmp = jnp.dot(_t_f32(vc_ref[:, hs]).astype(BF16), p.astype(BF16), preferred_element_type=F32)
        p_grp = p[:, 0:tq]
        for g in range(1, G):
            p_grp = p_grp + p[:, g * tq:(g + 1) * tq]
        p_hi, p_lo = _split_bf16(p_grp)
        imp = (jnp.dot(ov_t, p_hi, preferred_element_type=F32)
               + jnp.dot(ov_t, p_lo, preferred_element_type=F32))
        imp = jnp.where(forced, BIG, jnp.where(in_past, imp, NEG))
        yield
        rank = []
        for r0 in range(0, n_sel, SUBLANES):
            blk = imp[r0:r0 + SUBLANES, :]
            cnt = jnp.zeros(blk.shape, F32)
            for j in range(n_sel):
                row = imp[j:j + 1, :]
                if j < r0:
                    cnt = cnt + jnp.where(row >= blk, 1.0, 0.0)
                elif j >= r0 + SUBLANES - 1:
                    cnt = cnt + jnp.where(row > blk, 1.0, 0.0)
                else:
                    below = r0 + lax.broadcasted_iota(jnp.int32, blk.shape, 0) > j
                    cnt = cnt + jnp.where(below, jnp.where(row >= blk, 1.0, 0.0), jnp.where(row > blk, 1.0, 0.0))
            rank.append(cnt)
            yield
        rank = jnp.concatenate(rank, axis=0)
        bias = jnp.where(rank < float(min(SEL_TOPK, n_sel)), 0.0, NEG).astype(BF16)
        return o_cmp, jnp.concatenate([qts[k], jnp.concatenate([bias] * G, axis=1),
                                       jnp.zeros((NSA_DH - n_sel, M), BF16)], axis=0)

    subs = range(tq // tk)
    col_off = jnp.concatenate([lax.broadcasted_iota(jnp.int32, (1, tk), 1)] * G, axis=1)
    after = lambda: key_off > col_off
    too_far = lambda: key_off <= col_off

    def cols(x, j):
        return jnp.concatenate([x[:, g * tq + j * tk:g * tq + (j + 1) * tk] for g in range(G)], axis=1)

    def sel_tasks(k, j, qta):
        first, diag = (0, q0 // tk - 1) if j is None else (q0 // tk, (q0 + j * tk) // tk)
        for kt in range(first, diag + 1):
            yield _attend(lambda kt=kt: kaug_ref[k, kt * tk:(kt + 1) * tk, :], qta,
                          lambda kt=kt: vst_ref[k, kt], after if kt == diag and j is not None else None)

    def win_tasks(k, j, qt):
        hs = slice(k * NSA_DH, (k + 1) * NSA_DH)
        diag = (q0 + j * tk) // tk
        back = WINDOW // tk
        for kt in range(max(diag - back, 0), diag + 1):
            masked = after if kt == diag else too_far if kt == diag - back else None
            yield _attend(lambda kt=kt: kw_ref[kt * tk:(kt + 1) * tk, hs], qt, lambda kt=kt: vwt_ref[k, kt], masked)

    def run(task_lists, side=()):
        flat = [(key, t) for key, tasks in task_lists.items() for t in tasks]
        parts = {key: [] for key in task_lists}
        for (key, _), part in zip(flat, _pipelined([t for _, t in flat], side=side)):
            parts[key].append(part)
        merged = {}
        for k in heads:
            for j in subs:
                shared = [(cols(m, j), cols(acc, j)) for m, acc in parts.get((k, None), [])]
                merged[k, j] = _merge(shared + parts[k, j])
        return [jnp.concatenate([merged[k, j][:, g * tk:(g + 1) * tk] for g in range(G) for j in subs], axis=1)
                for k in heads]

    selects = [_Staged(select(k)) for k in heads]
    o_win = run({(k, j): list(win_tasks(k, j, cols(qts[k], j))) for k in heads for j in subs}, side=selects)
    o_cmp, qtas = zip(*[s.finish() for s in selects])
    o_sel = run({(k, j): list(sel_tasks(k, j, qtas[k] if j is None else cols(qtas[k], j)))
                 for k in heads for j in (None, *subs)})


    gates_t = _t_f32(jax.nn.sigmoid(misc_ref[...]))
    heads = []
    for k in range(NSA_KV_HEADS):
        for g in range(G):
            gs = slice(g * tq, (g + 1) * tq)
            gl = GATE_OFF + (k * G + g) * NSA_BRANCHES
            heads.append(gates_t[gl:gl + 1, :] * o_cmp[k][:, gs] + gates_t[gl + 1:gl + 2, :] * o_sel[k][:, gs]
                         + gates_t[gl + 2:gl + 3, :] * o_win[k][:, gs])
    ssq = heads[0] * heads[0]
    for o in heads[1:]:
        ssq = ssq + o * o
    inv = lax.rsqrt(jnp.sum(ssq, axis=0, keepdims=True) * (1.0 / NSA_WIDTH) + EPS)
    for hq, o in enumerate(heads):
        cs = slice(hq * NSA_DH, (hq + 1) * NSA_DH)
        o_ref[:, cs] = ((o * inv).T * gn_ref[:, cs]).astype(o_ref.dtype)


def _nsa_kernel(q_ref, kc_ref, vc_ref, ks_ref, vs_ref, kw_ref, vw_ref, misc_ref, gn_ref, o_ref,
                kaug_ref, vst_ref, vwt_ref):
    tq = q_ref.shape[0]
    tk = vst_ref.shape[3]
    S = ks_ref.shape[0]
    i = pl.program_id(1)

    @pl.when(i == 0)
    def _():
        pos = lax.broadcasted_iota(jnp.int32, (S, LANES), 0)
        lane = lax.broadcasted_iota(jnp.int32, (S, LANES), 1)
        onehot = jnp.where(lax.shift_right_logical(pos, 6) == lane, 1.0, 0.0).astype(BF16)
        row = lax.broadcasted_iota(jnp.int32, (V_AUG - NSA_DH, tk), 0)
        ones_rows = jnp.where(row == 0, 1.0, 0.0).astype(BF16)
        for k in range(NSA_KV_HEADS):
            hs = slice(k * NSA_DH, (k + 1) * NSA_DH)
            kaug_ref[k, :, 0:NSA_DH] = ks_ref[:, hs]
            kaug_ref[k, :, NSA_DH:2 * NSA_DH] = onehot
            for kt in range(S // tk):
                rows = slice(kt * tk, (kt + 1) * tk)
                for src, dst in ((vs_ref, vst_ref), (vw_ref, vwt_ref)):
                    dst[k, kt, 0:NSA_DH, :] = _t_f32(src[rows, hs]).astype(BF16)
                    dst[k, kt, NSA_DH:V_AUG, :] = ones_rows

    for c in range(S // tq):
        pl.when(i == c)(functools.partial(_nsa_step, c, tq, tk, q_ref, kc_ref, vc_ref, kw_ref, misc_ref,
                                          gn_ref, o_ref, kaug_ref, vst_ref, vwt_ref))


def _nsa(nq, kcmp, vcmp, ks, vs, kw, vw, misc, gn, tq, tk):
    B, S, _ = nq.shape
    n_cmp = kcmp.shape[1]
    assert n_cmp <= LANES and S // SEL_BLOCK <= NSA_DH and SEL_BLOCK == 64
    assert tq % tk == 0 and tq <= WINDOW and WINDOW % tk == 0
    row = lambda w: pl.BlockSpec((None, tq, w), lambda b, i: (b, i, 0))
    seq = lambda n: pl.BlockSpec((None, n, NSA_KV_WIDTH), lambda b, i: (b, 0, 0))
    vt_scratch = pltpu.VMEM((NSA_KV_HEADS, S // tk, V_AUG, tk), BF16)
    return pl.pallas_call(
        _nsa_kernel,
        grid=(B, S // tq),
        in_specs=[row(NSA_WIDTH), seq(n_cmp), seq(n_cmp), seq(S), seq(S), seq(S), seq(S), row(MISC_W),
                  pl.BlockSpec(gn.shape, lambda b, i: (0, 0))],
        out_specs=row(NSA_WIDTH),
        out_shape=jax.ShapeDtypeStruct((B, S, NSA_WIDTH), BF16),
        scratch_shapes=[pltpu.VMEM((NSA_KV_HEADS, S, 2 * NSA_DH), BF16), vt_scratch, vt_scratch],
        compiler_params=_params("parallel", "arbitrary"),
        name="nsa",
    )(nq, kcmp, vcmp, ks, vs, kw, vw, misc, gn)


def _log_sigmoid(z):
    return jnp.minimum(z, 0.0) - jnp.log(1.0 + jnp.exp(-jnp.abs(z)))


class _Staged:
    def __init__(self, gen):
        self.gen, self.value, self.done = gen, None, False

    def step(self):
        if not self.done:
            try:
                next(self.gen)
            except StopIteration as stop:
                self.value, self.done = stop.value, True

    def finish(self):
        while not self.done:
            self.step()
        return self.value


def _gla_tile(q, k, v, g, misc, wa, ba, gn, st):
    C = GLA_CHUNK
    T = q.shape[0]
    n_chunk = T // C
    r = lax.broadcasted_iota(jnp.int32, (T, T), 0)
    c = lax.broadcasted_iota(jnp.int32, (T, T), 1)
    causal = (r >= c) & (r - c <= (r & (C - 1)))
    tri = jnp.where(causal, 1.0, 0.0).astype(BF16)
    sr = lax.broadcasted_iota(jnp.int32, st.shape, 0)
    sc = lax.broadcasted_iota(jnp.int32, st.shape, 1)
    own_head = (lax.shift_right_logical(sr, GLA_DK.bit_length() - 1)
                == lax.shift_right_logical(sc, GLA_DV.bit_length() - 1))

    r_hi, r_lo = _split_bf16(misc)
    w_hi, w_lo = _split_bf16(wa)
    z = (jnp.dot(r_hi, w_hi, preferred_element_type=F32) + jnp.dot(r_lo, w_hi, preferred_element_type=F32)
         + jnp.dot(r_hi, w_lo, preferred_element_type=F32) + ba)
    la_hi, la_lo = _split_bf16(_log_sigmoid(z) * (1.0 / GLA_TAU))
    yield
    b = (jnp.dot(tri, la_hi, preferred_element_type=F32)
         + jnp.dot(tri, la_lo, preferred_element_type=F32))
    b_lasts = [b[(ci + 1) * C - 1:(ci + 1) * C, :] for ci in range(n_chunk)]
    b_last = jnp.concatenate([jnp.broadcast_to(bl, (C, bl.shape[1])) for bl in b_lasts], axis=0)
    qf = q.astype(F32)
    kf = k.astype(F32)
    q_in = (qf * GLA_DK ** -0.5 * jnp.exp(b)).astype(BF16)
    k_in = (kf * jnp.exp(-b)).astype(BF16)
    k_dec = (kf * jnp.exp(b_last - b)).astype(BF16)
    pad = jnp.zeros((LANES - n_chunk, b.shape[1]), F32)
    dec_cols = jnp.exp(jnp.concatenate(b_lasts + [pad], axis=0).T)
    yield

    o_intra = []
    for h in range(GLA_HEADS):
        ks = slice(h * GLA_DK, (h + 1) * GLA_DK)
        att = lax.dot_general(q_in[:, ks], k_in[:, ks], _NT, preferred_element_type=F32)
        att = jnp.where(causal, att, 0.0).astype(BF16)
        o_intra.append(jnp.dot(att, v[:, h * GLA_DV:(h + 1) * GLA_DV], preferred_element_type=F32))
        yield

    o_inter = []
    for ci in range(n_chunk):
        rows = slice(ci * C, (ci + 1) * C)
        o_inter.append(jnp.dot(q_in[rows], st.astype(BF16), preferred_element_type=F32))
        d_st = lax.dot_general(k_dec[rows], v[rows], _TN, preferred_element_type=F32)
        st = st * dec_cols[:, ci:ci + 1] + jnp.where(own_head, d_st, 0.0)
        yield
    o_inter = jnp.concatenate(o_inter, axis=0)
    gate = g.astype(F32)
    gate = gate * jax.nn.sigmoid(gate)
    outs = []
    for h in range(GLA_HEADS):
        vs = slice(h * GLA_DV, (h + 1) * GLA_DV)
        outs.append(_rms(o_intra[h] + o_inter[:, vs]) * gn[:, vs] * gate[:, vs])
    return jnp.concatenate(outs, axis=1), st


def _gla_ffn_kernel(x_ref, on_ref, mod_ref, wo_ref, g2_ref, w1_ref, w2_ref, gf_ref,
                    q0_ref, k0_ref, v0_ref, g0_ref, m0_ref, q1_ref, k1_ref, v1_ref, g1_ref, m1_ref,
                    wa_ref, ba_ref, gn_ref, o_ref, st_ref, og_ref, *, ff_chunk, final):
    def gla(q_ref, k_ref, v_ref, g_ref, m_ref, st):
        return _Staged(_gla_tile(q_ref[...], k_ref[...], v_ref[...], g_ref[...], m_ref[...],
                                 wa_ref[...], ba_ref[...], gn_ref[...], st))

    @pl.when((pl.program_id(0) == 0) & (pl.program_id(1) == 0))
    def _():
        o, st = gla(q0_ref, k0_ref, v0_ref, g0_ref, m0_ref, jnp.zeros(st_ref.shape, F32)).finish()
        og_ref[...] = o.astype(BF16)
        st_ref[...] = st

    last = pl.program_id(1) == pl.num_programs(1) - 1
    ahead = gla(q1_ref, k1_ref, v1_ref, g1_ref, m1_ref, jnp.where(last, 0.0, st_ref[...]))

    ahead.step()
    mix = (jnp.dot(og_ref[...], wo_ref[0:GLA_WIDTH, :], preferred_element_type=F32)
           + jnp.dot(on_ref[...], wo_ref[GLA_WIDTH:, :], preferred_element_type=F32))
    x1 = x_ref[...] + mod_ref[2:3, :] * mix
    h = (_rms(x1) * g2_ref[...] * (1.0 + mod_ref[4:5, :]) + mod_ref[3:4, :]).astype(BF16)
    acc = jnp.zeros_like(x1)
    ahead.step()
    for c in range(w1_ref.shape[1] // ff_chunk):
        cs = slice(c * ff_chunk, (c + 1) * ff_chunk)
        a = jnp.maximum(jnp.dot(h, w1_ref[:, cs], preferred_element_type=F32), 0.0)
        ahead.step()
        acc = acc + jnp.dot((a * a).astype(BF16), w2_ref[cs, :], preferred_element_type=F32)
        ahead.step()
    x2 = x1 + mod_ref[5:6, :] * acc
    o_ref[...] = _rms(x2) * gf_ref[...] if final else x2

    o, st = ahead.finish()
    og_ref[...] = o.astype(BF16)
    st_ref[...] = st


def _gla_ffn(x, on, mod, wo, g2, w1, w2, gf, gq, gk, gv, gg, misc, wa_p, ba, gn, final):
    B, S, D = x.shape
    tm = GLA_TILE
    n = S // tm
    row = lambda w: pl.BlockSpec((None, tm, w), lambda b, i: (b, i, 0))
    first = lambda w: pl.BlockSpec((None, tm, w), lambda b, i: (0, 0, 0))
    ahead = lambda w: pl.BlockSpec((None, tm, w), lambda b, i: (
        jnp.where(i + 1 < n, b, jnp.minimum(b + 1, B - 1)), jnp.where(i + 1 < n, i + 1, 0), 0))
    whole = lambda a: pl.BlockSpec(a.shape, lambda b, i: (0,) * a.ndim)
    gla_in = (gq, gk, gv, gg, misc)
    return pl.pallas_call(
        functools.partial(_gla_ffn_kernel, ff_chunk=1024, final=final),
        grid=(B, n),
        in_specs=([row(D), row(on.shape[2]), pl.BlockSpec((None, N_MOD, D), lambda b, i: (b, 0, 0)),
                   whole(wo), whole(g2), whole(w1), whole(w2), whole(gf)]
                  + [first(a.shape[2]) for a in gla_in] + [ahead(a.shape[2]) for a in gla_in]
                  + [whole(wa_p), whole(ba), whole(gn)]),
        out_specs=row(D),
        out_shape=jax.ShapeDtypeStruct((B, S, D), F32),
        scratch_shapes=[pltpu.VMEM((GLA_QK, GLA_WIDTH), F32), pltpu.VMEM((tm, GLA_WIDTH), BF16)],
        compiler_params=_params("arbitrary", "arbitrary"),
        name="gla_ffn",
    )(x, on, mod, wo, g2, w1, w2, gf, *gla_in, *gla_in, wa_p, ba, gn)


def kernel(x, c, positions, w_ada, b_ada, norm1_g, w_in, gla_w_a2, gla_b_a, gla_norm_g, nsa_pe_k, nsa_pe_v, cmp_k_w1, cmp_k_w2, cmp_v_w1, cmp_v_w2, nsa_norm_g, w_o, norm2_g, w_ff1, w_ff2, final_norm_g):
    B, S, D = x.shape
    depth = w_in.shape[0]
    half = NSA_DH // 2
    inv = ROPE_THETA ** (-jnp.arange(half, dtype=F32) / half)
    inv = jnp.concatenate([inv, inv]).reshape(1, NSA_DH)
    for l in range(depth):
        mod = _adaln(c, w_ada[l], b_ada[l]).reshape(B, N_MOD, D)
        (gq, gk, gv, gg, nq, kc, vc, ks, vs, kw, vw, misc) = _in_proj(
            x, mod, norm1_g[l].reshape(1, D), positions, inv, w_in, l, tm=512)

        k_cmp, v_cmp = _compress(kc, vc, cmp_k_w1[l].astype(BF16), _compress_pe(nsa_pe_k[l]),
                                 cmp_k_w2[l].astype(BF16), cmp_v_w1[l].astype(BF16),
                                 _compress_pe(nsa_pe_v[l]), cmp_v_w2[l].astype(BF16), nb=int(np.gcd(B, 4)))
        o_nsa = _nsa(nq, k_cmp, v_cmp, ks, vs, kw, vw, misc, nsa_norm_g[l].reshape(1, NSA_WIDTH),
                     tq=512, tk=256)

        wa_p = jnp.zeros((MISC_W, GLA_QK), F32).at[0:GLA_RANK].set(gla_w_a2[l])
        x = _gla_ffn(x, o_nsa, mod, w_o[l].astype(BF16), norm2_g[l].reshape(1, D),
                     w_ff1[l].astype(BF16), w_ff2[l].astype(BF16), final_norm_g.reshape(1, D),
                     gq, gk, gv, gg, misc, wa_p, gla_b_a[l].reshape(1, GLA_QK),
                     gla_norm_g[l].reshape(1, GLA_WIDTH), final=(l == depth - 1))
    return x
```

```python
import functools

import numpy as np
import jax
import jax.numpy as jnp
from jax import lax
from jax.experimental import pallas as pl
from jax.experimental.pallas import tpu as pltpu

GLA_HEADS = 4
GLA_DK = 64
GLA_DV = 128
GLA_RANK = 16
GLA_TAU = 16.0
GLA_CHUNK = 64
NSA_HEADS = 4
NSA_KV_HEADS = 2
NSA_DH = 128
NSA_BRANCHES = 3
CMP_BLOCK = 32
CMP_STRIDE = 16
CMP_HIDDEN = 256
SEL_BLOCK = 64
SEL_TOPK = 16
WINDOW = 512
N_MOD = 6
ROPE_THETA = 10000.0
EPS = 1e-6
NEG = -1e30
BIG = 1e30

GLA_QK = GLA_HEADS * GLA_DK
GLA_WIDTH = GLA_HEADS * GLA_DV
NSA_WIDTH = NSA_HEADS * NSA_DH
NSA_KV_WIDTH = NSA_KV_HEADS * NSA_DH
N_GATE = NSA_HEADS * NSA_BRANCHES
IN_SPLITS = (GLA_QK, GLA_QK, GLA_WIDTH, GLA_WIDTH, GLA_RANK, NSA_WIDTH) + (NSA_KV_WIDTH,) * 6 + (N_GATE,)

LANES = 128
SUBLANES = 8
MISC_W = LANES
GATE_OFF = GLA_RANK
GLA_TILE = 256
VMEM_LIMIT = 56 * 1024 * 1024

F32 = jnp.float32
BF16 = jnp.bfloat16

_NT = (((1,), (1,)), ((), ()))
_TN = (((0,), (0,)), ((), ()))


def _params(*sem):
    return pltpu.CompilerParams(dimension_semantics=sem, vmem_limit_bytes=VMEM_LIMIT)


def _rms(x):
    return x * lax.rsqrt(jnp.mean(x * x, axis=-1, keepdims=True) + EPS)


def _split_bf16(x):
    hi = x.astype(BF16)
    return hi, (x - hi.astype(F32)).astype(BF16)


def _t_f32(x):
    return x.astype(F32).T


def _adaln_kernel(c_ref, w_ref, b_ref, o_ref):
    c = c_ref[...]
    n = c.shape[0]
    a_hi, a_lo = _split_bf16(c * jax.nn.sigmoid(c))
    w_hi, w_lo = _split_bf16(w_ref[...])
    y = jnp.dot(jnp.concatenate([a_hi, a_lo], axis=0), w_hi, preferred_element_type=F32)
    o_ref[...] = y[0:n] + y[n:2 * n] + jnp.dot(a_hi, w_lo, preferred_element_type=F32) + b_ref[...]


def _adaln(c, w_ada, b_ada):
    B, D = c.shape
    N = w_ada.shape[1]
    tn = D
    return pl.pallas_call(
        _adaln_kernel,
        grid=(N // tn,),
        in_specs=[pl.BlockSpec((B, D), lambda j: (0, 0)),
                  pl.BlockSpec((D, tn), lambda j: (0, j)),
                  pl.BlockSpec((1, tn), lambda j: (0, j))],
        out_specs=pl.BlockSpec((B, tn), lambda j: (0, j)),
        out_shape=jax.ShapeDtypeStruct((B, N), F32),
        compiler_params=_params("arbitrary"),
        name="adaln",
    )(c, w_ada, b_ada.reshape(1, N))


_PROJ_GROUPS = (("gq", GLA_QK, False), ("gk", GLA_QK, False), ("gv", GLA_WIDTH, False),
                ("gg", GLA_WIDTH, False), ("nq", NSA_WIDTH, True), ("kc", NSA_KV_WIDTH, True),
                ("vc", NSA_KV_WIDTH, False), ("ks", NSA_KV_WIDTH, True), ("vs", NSA_KV_WIDTH, False),
                ("kw", NSA_KV_WIDTH, True), ("vw", NSA_KV_WIDTH, False))
_PROJ_W = sum(g[1] for g in _PROJ_GROUPS) + MISC_W


def _w_in_moves():
    names = ("gq", "gk", "gv", "gg", "gr", "nq", "kc", "vc", "ks", "vs", "kw", "vw", "ngate")
    src = dict(zip(names, np.cumsum((0,) + IN_SPLITS[:-1])))
    width = dict(zip(names, IN_SPLITS))
    moves, dst = [], 0
    for name, w, _ in _PROJ_GROUPS:
        moves.append((int(src[name]), dst, w))
        dst += w
    for name in ("gr", "ngate"):
        moves.append((int(src[name]), dst, width[name]))
        dst += width[name]
    return tuple(moves)


_W_IN_MOVES = _w_in_moves()
_GROUPED = ("kc", "vc")
GRP_W = CMP_STRIDE * NSA_DH


def _in_proj_kernel(x_ref, mod_ref, g_ref, pos_ref, inv_ref, win_ref, *refs):
    n_out = len(_PROJ_GROUPS) + 1
    out_refs, stage_refs, w_ref = refs[:n_out], refs[n_out:-1], refs[-1]

    @pl.when((pl.program_id(0) == 0) & (pl.program_id(1) == 0))
    def _():
        for src, dst, width in _W_IN_MOVES:
            w_ref[:, dst:dst + width] = win_ref[:, src:src + width].astype(BF16)
        pad = _PROJ_W - _W_IN_MOVES[-1][1] - _W_IN_MOVES[-1][2]
        w_ref[:, _PROJ_W - pad:] = jnp.zeros((w_ref.shape[0], pad), BF16)

    tm = x_ref.shape[0]
    x = x_ref[...]
    h = _rms(x) * g_ref[...] * (1.0 + mod_ref[1:2, :]) + mod_ref[0:1, :]
    hb = h.astype(BF16)

    qs = NSA_DH ** -0.5 * float(np.log2(np.e))
    cos = sin = None

    off = 0
    for (name, width, rot), o_ref in zip(_PROJ_GROUPS, out_refs[:-1]):
        if rot and cos is None:
            pos = pos_ref[...].astype(F32)
            pos = jnp.concatenate([jnp.broadcast_to(pos[r:r + 1, :], (LANES, LANES)).T
                                   for r in range(pos.shape[0])], axis=0)
            ang = pos * inv_ref[...]
            lane = lax.broadcasted_iota(jnp.int32, (1, LANES), 1)
            cos = jnp.cos(ang)
            sin = jnp.sin(ang) * jnp.where(lane < NSA_DH // 2, -1.0, 1.0)
        y = jnp.dot(hb, w_ref[:, off:off + width], preferred_element_type=F32)
        grouped = name in _GROUPED
        dst = stage_refs[_GROUPED.index(name)] if grouped else o_ref
        for hd in range(width // NSA_DH if (rot or grouped) else 0):
            yh = y[:, hd * NSA_DH:(hd + 1) * NSA_DH]
            if rot:
                c, s = (cos * qs, sin * qs) if name == "nq" else (cos, sin)
                yh = yh * c + pltpu.roll(yh, NSA_DH // 2, axis=1) * s
            if grouped:
                dst[hd] = yh
            else:
                dst[:, hd * NSA_DH:(hd + 1) * NSA_DH] = yh.astype(dst.dtype)
        if not (rot or grouped):
            dst[...] = y.astype(dst.dtype)
        if grouped:
            for k in range(NSA_KV_HEADS):
                for tok in range(CMP_STRIDE):
                    c0 = k * GRP_W + tok * NSA_DH
                    o_ref[:, c0:c0 + NSA_DH] = dst[k, pl.ds(tok, tm // CMP_STRIDE, stride=CMP_STRIDE), :].astype(
                        o_ref.dtype)
        off += width
    out_refs[-1][...] = jnp.dot(hb, w_ref[:, off:off + MISC_W], preferred_element_type=F32)


def _in_proj(x, mod, norm_g, positions, inv, w_in, layer, tm):
    B, S, D = x.shape
    pos4 = positions.reshape(B, S // tm, tm // LANES, LANES)
    row = lambda w: pl.BlockSpec((None, tm, w), lambda b, i: (b, i, 0))
    whole = lambda a: pl.BlockSpec(a.shape, lambda b, i: (0,) * a.ndim)
    out_shape, out_specs = [], []
    for name, w, _ in _PROJ_GROUPS:
        if name in _GROUPED:
            out_shape.append(jax.ShapeDtypeStruct((B, S // CMP_STRIDE, CMP_STRIDE * w), BF16))
            out_specs.append(pl.BlockSpec((None, tm // CMP_STRIDE, CMP_STRIDE * w), lambda b, i: (b, i, 0)))
        else:
            out_shape.append(jax.ShapeDtypeStruct((B, S, w), BF16))
            out_specs.append(row(w))
    out_shape.append(jax.ShapeDtypeStruct((B, S, MISC_W), F32))
    out_specs.append(row(MISC_W))
    return pl.pallas_call(
        _in_proj_kernel,
        grid=(B, S // tm),
        in_specs=[row(D), pl.BlockSpec((None, N_MOD, D), lambda b, i: (b, 0, 0)), whole(norm_g),
                  pl.BlockSpec((None, None, tm // LANES, LANES), lambda b, i: (b, i, 0, 0)), whole(inv),
                  pl.BlockSpec((None,) + w_in.shape[1:], lambda b, i: (layer, 0, 0),
                               pipeline_mode=pl.Buffered(1))],
        out_specs=out_specs,
        out_shape=out_shape,
        scratch_shapes=([pltpu.VMEM((NSA_KV_HEADS, tm, NSA_DH), F32) for _ in _GROUPED]
                        + [pltpu.VMEM((D, _PROJ_W), BF16)]),
        compiler_params=_params("arbitrary", "arbitrary"),
        name="in_proj",
    )(x, mod, norm_g, pos4, inv, w_in)


def _compress_kernel(xk_ref, xv_ref, kw1_ref, kpe_ref, kw2_ref, vw1_ref, vpe_ref, vw2_ref, ok_ref, ov_ref):
    nb, n_grp, _ = xk_ref.shape
    rows = nb * n_grp
    for x_ref, w1_ref, pe_ref, w2_ref, o_ref in ((xk_ref, kw1_ref, kpe_ref, kw2_ref, ok_ref),
                                                 (xv_ref, vw1_ref, vpe_ref, vw2_ref, ov_ref)):
        w1a = w1_ref[0:GRP_W, :]
        w1b = w1_ref[GRP_W:2 * GRP_W, :]
        x = jnp.concatenate([x_ref[:, :, k * GRP_W:(k + 1) * GRP_W].reshape(rows, GRP_W)
                             for k in range(NSA_KV_HEADS)], axis=0)
        bias = (jnp.dot(pe_ref[:, 0:GRP_W], w1a, preferred_element_type=F32)[0:1]
                + jnp.dot(pe_ref[:, GRP_W:2 * GRP_W], w1b, preferred_element_type=F32)[0:1])
        ua = jnp.dot(x, w1a, preferred_element_type=F32)
        ub = jnp.dot(x, w1b, preferred_element_type=F32)
        hid = ua + pltpu.roll(ub, NSA_KV_HEADS * rows - 1, axis=0) + bias
        act = jax.nn.gelu(hid, approximate=True).astype(BF16)
        y = jnp.dot(act, w2_ref[...], preferred_element_type=F32).astype(o_ref.dtype)
        for k in range(NSA_KV_HEADS):
            o_ref[:, :, k * NSA_DH:(k + 1) * NSA_DH] = y[k * rows:(k + 1) * rows].reshape(nb, n_grp, NSA_DH)


def _compress(xk, xv, kw1, kpe, kw2, vw1, vpe, vw2, nb):
    B, n_grp, gw = xk.shape
    W = NSA_KV_WIDTH
    whole = lambda a: pl.BlockSpec(a.shape, lambda b: (0,) * a.ndim)
    xspec = pl.BlockSpec((nb, n_grp, gw), lambda b: (b, 0, 0))
    ospec = pl.BlockSpec((nb, n_grp, W), lambda b: (b, 0, 0))
    return pl.pallas_call(
        _compress_kernel,
        grid=(B // nb,),
        in_specs=[xspec, xspec, whole(kw1), whole(kpe), whole(kw2), whole(vw1), whole(vpe), whole(vw2)],
        out_specs=[ospec, ospec],
        out_shape=[jax.ShapeDtypeStruct((B, n_grp, W), BF16)] * 2,
        compiler_params=_params("parallel"),
        name="compress",
    )(xk, xv, kw1, kpe, kw2, vw1, vpe, vw2)


def _compress_pe(pe):
    return jnp.zeros((8, 2 * GRP_W), F32).at[0].set(pe.reshape(2 * GRP_W)).astype(BF16)


V_AUG = NSA_DH + 16


def _attend(k_rows, qt, vt_aug, masked):
    s = jnp.dot(k_rows(), qt, preferred_element_type=F32)
    if masked is not None:
        s = jnp.where(masked(), NEG, s)
    yield
    m = jnp.max(s, axis=0, keepdims=True)
    p = jnp.exp2((s - m).astype(BF16))
    yield
    return m, jnp.dot(vt_aug(), p, preferred_element_type=F32)


def _pipelined(gens, depth=1, side=()):
    tasks = [_Staged(g) for g in gens]
    for n in range(len(tasks) + depth):
        for t in tasks[max(0, n - depth):n + 1]:
            t.step()
        for s in side:
            s.step()
    return [t.finish() for t in tasks]


def _merge(parts):
    m_all = parts[0][0]
    for m, _ in parts[1:]:
        m_all = jnp.maximum(m_all, m)
    tot = None
    for m, acc in parts:
        w = acc * jnp.exp2(m - m_all)
        tot = w if tot is None else tot + w
    return tot[0:NSA_DH] / tot[NSA_DH:NSA_DH + 1]


def _nsa_step(c, tq, tk, q_ref, kc_ref, vc_ref, kw_ref, misc_ref, gn_ref, o_ref, kaug_ref, vst_ref, vwt_ref):
    S = kw_ref.shape[0]
    n_cmp = kc_ref.shape[0]
    n_sel = S // SEL_BLOCK
    G = NSA_HEADS // NSA_KV_HEADS
    M = G * tq
    q0 = c * tq

    t_q = q0 + lax.broadcasted_iota(jnp.int32, (1, tq), 1)
    t_m = jnp.concatenate([t_q] * G, axis=1)
    key_off = lax.broadcasted_iota(jnp.int32, (tk, 1), 0)

    n_col = lax.broadcasted_iota(jnp.int32, (n_cmp, 1), 0)
    cmp_end = jnp.where(n_col < n_cmp - 1, n_col * CMP_STRIDE + (CMP_BLOCK - 1), jnp.int32(2 ** 30))
    cmp_valid = cmp_end <= t_m
    any_valid = (t_m >= CMP_BLOCK - 1).astype(F32)
    jj = lax.broadcasted_iota(jnp.int32, (n_sel, n_cmp), 0) * SEL_BLOCK
    nn = lax.broadcasted_iota(jnp.int32, (n_sel, n_cmp), 1) * CMP_STRIDE
    ov_t = jnp.maximum(jnp.minimum(nn + CMP_BLOCK, jj + SEL_BLOCK) - jnp.maximum(nn, jj), 0)
    ov_t = (ov_t.astype(F32) * (1.0 / CMP_BLOCK)).astype(BF16)
    j_row = lax.broadcasted_iota(jnp.int32, (n_sel, tq), 0)
    blk_t = lax.shift_right_logical(q0 + lax.broadcasted_iota(jnp.int32, (n_sel, tq), 1), 6)
    forced = (j_row == 0) | (j_row == blk_t) | (j_row == blk_t - 1)
    in_past = j_row <= blk_t

    heads = range(NSA_KV_HEADS)
    qts = [jnp.concatenate([_t_f32(q_ref[:, (k * G + g) * NSA_DH:(k * G + g + 1) * NSA_DH])
                            for g in range(G)], axis=1).astype(BF16) for k in heads]

    def select(k):
        hs = slice(k * NSA_DH, (k + 1) * NSA_DH)
        s = jnp.dot(kc_ref[:, hs], qts[k], preferred_element_type=F32)
        s = jnp.where(cmp_valid, s, NEG)
        yield
        e = jnp.exp2(s - jnp.max(s, axis=0, keepdims=True))
        p = e / jnp.sum(e, axis=0, keepdims=True) * any_valid
        yield
        o_cmp = jnp.dot(_t_f32(vc_ref[:, hs]).astype(BF16), p.astype(BF16), preferred_element_type=F32)
        p_grp = p[:, 0:tq]
        for g in range(1, G):
            p_grp = p_grp + p[:, g * tq:(g + 1) * tq]
        p_hi, p_lo = _split_bf16(p_grp)
        imp = (jnp.dot(ov_t, p_hi, preferred_element_type=F32)
               + jnp.dot(ov_t, p_lo, preferred_element_type=F32))
        imp = jnp.where(forced, BIG, jnp.where(in_past, imp, NEG))
        yield
        rank = []
        for r0 in range(0, n_sel, SUBLANES):
            blk = imp[r0:r0 + SUBLANES, :]
            cnt = jnp.zeros(blk.shape, F32)
            for j in range(n_sel):
                row = imp[j:j + 1, :]
                if j < r0:
                    cnt = cnt + jnp.where(row >= blk, 1.0, 0.0)
                elif j >= r0 + SUBLANES - 1:
                    cnt = cnt + jnp.where(row > blk, 1.0, 0.0)
                else:
                    below = r0 + lax.broadcasted_iota(jnp.int32, blk.shape, 0) > j
                    cnt = cnt + jnp.where(below, jnp.where(row >= blk, 1.0, 0.0), jnp.where(row > blk, 1.0, 0.0))
            rank.append(cnt)
            yield
        rank = jnp.concatenate(rank, axis=0)
        bias = jnp.where(rank < float(min(SEL_TOPK, n_sel)), 0.0, NEG).astype(BF16)
        return o_cmp, jnp.concatenate([qts[k], jnp.concatenate([bias] * G, axis=1),
                                       jnp.zeros((NSA_DH - n_sel, M), BF16)], axis=0)

    subs = range(tq // tk)
    col_off = jnp.concatenate([lax.broadcasted_iota(jnp.int32, (1, tk), 1)] * G, axis=1)
    after = lambda: key_off > col_off
    too_far = lambda: key_off <= col_off

    def cols(x, j):
        return jnp.concatenate([x[:, g * tq + j * tk:g * tq + (j + 1) * tk] for g in range(G)], axis=1)

    def sel_tasks(k, j, qta):
        first, diag = (0, q0 // tk - 1) if j is None else (q0 // tk, (q0 + j * tk) // tk)
        for kt in range(first, diag + 1):
            yield _attend(lambda kt=kt: kaug_ref[k, kt * tk:(kt + 1) * tk, :], qta,
                          lambda kt=kt: vst_ref[k, kt], after if kt == diag and j is not None else None)

    def win_tasks(k, j, qt):
        hs = slice(k * NSA_DH, (k + 1) * NSA_DH)
        diag = (q0 + j * tk) // tk
        back = WINDOW // tk
        for kt in range(max(diag - back, 0), diag + 1):
            masked = after if kt == diag else too_far if kt == diag - back else None
            yield _attend(lambda kt=kt: kw_ref[kt * tk:(kt + 1) * tk, hs], qt, lambda kt=kt: vwt_ref[k, kt], masked)

    def run(task_lists, side=()):
        flat = [(key, t) for key, tasks in task_lists.items() for t in tasks]
        parts = {key: [] for key in task_lists}
        for (key, _), part in zip(flat, _pipelined([t for _, t in flat], side=side)):
            parts[key].append(part)
        merged = {}
        for k in heads:
            for j in subs:
                shared = [(cols(m, j), cols(acc, j)) for m, acc in parts.get((k, None), [])]
                merged[k, j] = _merge(shared + parts[k, j])
        return [jnp.concatenate([merged[k, j][:, g * tk:(g + 1) * tk] for g in range(G) for j in subs], axis=1)
                for k in heads]

    selects = [_Staged(select(k)) for k in heads]
    o_win = run({(k, j): list(win_tasks(k, j, cols(qts[k], j))) for k in heads for j in subs}, side=selects)
    o_cmp, qtas = zip(*[s.finish() for s in selects])
    o_sel = run({(k, j): list(sel_tasks(k, j, qtas[k] if j is None else cols(qtas[k], j)))
                 for k in heads for j in (None, *subs)})


    gates_t = _t_f32(jax.nn.sigmoid(misc_ref[...]))
    heads = []
    for k in range(NSA_KV_HEADS):
        for g in range(G):
            gs = slice(g * tq, (g + 1) * tq)
            gl = GATE_OFF + (k * G + g) * NSA_BRANCHES
            heads.append(gates_t[gl:gl + 1, :] * o_cmp[k][:, gs] + gates_t[gl + 1:gl + 2, :] * o_sel[k][:, gs]
                         + gates_t[gl + 2:gl + 3, :] * o_win[k][:, gs])
    ssq = heads[0] * heads[0]
    for o in heads[1:]:
        ssq = ssq + o * o
    inv = lax.rsqrt(jnp.sum(ssq, axis=0, keepdims=True) * (1.0 / NSA_WIDTH) + EPS)
    for hq, o in enumerate(heads):
        cs = slice(hq * NSA_DH, (hq + 1) * NSA_DH)
        o_ref[:, cs] = ((o * inv).T * gn_ref[:, cs]).astype(o_ref.dtype)


def _nsa_kernel(q_ref, kc_ref, vc_ref, ks_ref, vs_ref, kw_ref, vw_ref, misc_ref, gn_ref, o_ref,
                kaug_ref, vst_ref, vwt_ref):
    tq = q_ref.shape[0]
    tk = vst_ref.shape[3]
    S = ks_ref.shape[0]
    i = pl.program_id(1)

    @pl.when(i == 0)
    def _():
        pos = lax.broadcasted_iota(jnp.int32, (S, LANES), 0)
        lane = lax.broadcasted_iota(jnp.int32, (S, LANES), 1)
        onehot = jnp.where(lax.shift_right_logical(pos, 6) == lane, 1.0, 0.0).astype(BF16)
        row = lax.broadcasted_iota(jnp.int32, (V_AUG - NSA_DH, tk), 0)
        ones_rows = jnp.where(row == 0, 1.0, 0.0).astype(BF16)
        for k in range(NSA_KV_HEADS):
            hs = slice(k * NSA_DH, (k + 1) * NSA_DH)
            kaug_ref[k, :, 0:NSA_DH] = ks_ref[:, hs]
            kaug_ref[k, :, NSA_DH:2 * NSA_DH] = onehot
            for kt in range(S // tk):
                rows = slice(kt * tk, (kt + 1) * tk)
                for src, dst in ((vs_ref, vst_ref), (vw_ref, vwt_ref)):
                    dst[k, kt, 0:NSA_DH, :] = _t_f32(src[rows, hs]).astype(BF16)
                    dst[k, kt, NSA_DH:V_AUG, :] = ones_rows

    for c in range(S // tq):
        pl.when(i == c)(functools.partial(_nsa_step, c, tq, tk, q_ref, kc_ref, vc_ref, kw_ref, misc_ref,
                                          gn_ref, o_ref, kaug_ref, vst_ref, vwt_ref))


def _nsa(nq, kcmp, vcmp, ks, vs, kw, vw, misc, gn, tq, tk):
    B, S, _ = nq.shape
    n_cmp = kcmp.shape[1]
    assert n_cmp <= LANES and S // SEL_BLOCK <= NSA_DH and SEL_BLOCK == 64
    assert tq % tk == 0 and tq <= WINDOW and WINDOW % tk == 0
    row = lambda w: pl.BlockSpec((None, tq, w), lambda b, i: (b, i, 0))
    seq = lambda n: pl.BlockSpec((None, n, NSA_KV_WIDTH), lambda b, i: (b, 0, 0))
    vt_scratch = pltpu.VMEM((NSA_KV_HEADS, S // tk, V_AUG, tk), BF16)
    return pl.pallas_call(
        _nsa_kernel,
        grid=(B, S // tq),
        in_specs=[row(NSA_WIDTH), seq(n_cmp), seq(n_cmp), seq(S), seq(S), seq(S), seq(S), row(MISC_W),
                  pl.BlockSpec(gn.shape, lambda b, i: (0, 0))],
        out_specs=row(NSA_WIDTH),
        out_shape=jax.ShapeDtypeStruct((B, S, NSA_WIDTH), BF16),
        scratch_shapes=[pltpu.VMEM((NSA_KV_HEADS, S, 2 * NSA_DH), BF16), vt_scratch, vt_scratch],
        compiler_params=_params("parallel", "arbitrary"),
        name="nsa",
    )(nq, kcmp, vcmp, ks, vs, kw, vw, misc, gn)


def _log_sigmoid(z):
    return jnp.minimum(z, 0.0) - jnp.log(1.0 + jnp.exp(-jnp.abs(z)))


class _Staged:
    def __init__(self, gen):
        self.gen, self.value, self.done = gen, None, False

    def step(self):
        if not self.done:
            try:
                next(self.gen)
            except StopIteration as stop:
                self.value, self.done = stop.value, True

    def finish(self):
        while not self.done:
            self.step()
        return self.value


def _gla_tile(q, k, v, g, misc, wa, ba, gn, st):
    C = GLA_CHUNK
    T = q.shape[0]
    n_chunk = T // C
    r = lax.broadcasted_iota(jnp.int32, (T, T), 0)
    c = lax.broadcasted_iota(jnp.int32, (T, T), 1)
    causal = (r >= c) & (r - c <= (r & (C - 1)))
    tri = jnp.where(causal, 1.0, 0.0).astype(BF16)
    sr = lax.broadcasted_iota(jnp.int32, st.shape, 0)
    sc = lax.broadcasted_iota(jnp.int32, st.shape, 1)
    own_head = (lax.shift_right_logical(sr, GLA_DK.bit_length() - 1)
                == lax.shift_right_logical(sc, GLA_DV.bit_length() - 1))

    r_hi, r_lo = _split_bf16(misc)
    w_hi, w_lo = _split_bf16(wa)
    z = (jnp.dot(r_hi, w_hi, preferred_element_type=F32) + jnp.dot(r_lo, w_hi, preferred_element_type=F32)
         + jnp.dot(r_hi, w_lo, preferred_element_type=F32) + ba)
    la_hi, la_lo = _split_bf16(_log_sigmoid(z) * (1.0 / GLA_TAU))
    yield
    b = (jnp.dot(tri, la_hi, preferred_element_type=F32)
         + jnp.dot(tri, la_lo, preferred_element_type=F32))
    b_lasts = [b[(ci + 1) * C - 1:(ci + 1) * C, :] for ci in range(n_chunk)]
    b_last = jnp.concatenate([jnp.broadcast_to(bl, (C, bl.shape[1])) for bl in b_lasts], axis=0)
    qf = q.astype(F32)
    kf = k.astype(F32)
    q_in = (qf * GLA_DK ** -0.5 * jnp.exp(b)).astype(BF16)
    k_in = (kf * jnp.exp(-b)).astype(BF16)
    k_dec = (kf * jnp.exp(b_last - b)).astype(BF16)
    pad = jnp.zeros((LANES - n_chunk, b.shape[1]), F32)
    dec_cols = jnp.exp(jnp.concatenate(b_lasts + [pad], axis=0).T)
    yield

    o_intra = []
    for h in range(GLA_HEADS):
        ks = slice(h * GLA_DK, (h + 1) * GLA_DK)
        att = lax.dot_general(q_in[:, ks], k_in[:, ks], _NT, preferred_element_type=F32)
        att = jnp.where(causal, att, 0.0).astype(BF16)
        o_intra.append(jnp.dot(att, v[:, h * GLA_DV:(h + 1) * GLA_DV], preferred_element_type=F32))
        yield

    o_inter = []
    for ci in range(n_chunk):
        rows = slice(ci * C, (ci + 1) * C)
        o_inter.append(jnp.dot(q_in[rows], st.astype(BF16), preferred_element_type=F32))
        d_st = lax.dot_general(k_dec[rows], v[rows], _TN, preferred_element_type=F32)
        st = st * dec_cols[:, ci:ci + 1] + jnp.where(own_head, d_st, 0.0)
        yield
    o_inter = jnp.concatenate(o_inter, axis=0)
    gate = g.astype(F32)
    gate = gate * jax.nn.sigmoid(gate)
    outs = []
    for h in range(GLA_HEADS):
        vs = slice(h * GLA_DV, (h + 1) * GLA_DV)
        outs.append(_rms(o_intra[h] + o_inter[:, vs]) * gn[:, vs] * gate[:, vs])
    return jnp.concatenate(outs, axis=1), st


def _gla_ffn_kernel(x_ref, on_ref, mod_ref, wo_ref, g2_ref, w1_ref, w2_ref, gf_ref,
                    q0_ref, k0_ref, v0_ref, g0_ref, m0_ref, q1_ref, k1_ref, v1_ref, g1_ref, m1_ref,
                    wa_ref, ba_ref, gn_ref, o_ref, st_ref, og_ref, *, ff_chunk, final):
    def gla(q_ref, k_ref, v_ref, g_ref, m_ref, st):
        return _Staged(_gla_tile(q_ref[...], k_ref[...], v_ref[...], g_ref[...], m_ref[...],
                                 wa_ref[...], ba_ref[...], gn_ref[...], st))

    @pl.when((pl.program_id(0) == 0) & (pl.program_id(1) == 0))
    def _():
        o, st = gla(q0_ref, k0_ref, v0_ref, g0_ref, m0_ref, jnp.zeros(st_ref.shape, F32)).finish()
        og_ref[...] = o.astype(BF16)
        st_ref[...] = st

    last = pl.program_id(1) == pl.num_programs(1) - 1
    ahead = gla(q1_ref, k1_ref, v1_ref, g1_ref, m1_ref, jnp.where(last, 0.0, st_ref[...]))

    ahead.step()
    mix = (jnp.dot(og_ref[...], wo_ref[0:GLA_WIDTH, :], preferred_element_type=F32)
           + jnp.dot(on_ref[...], wo_ref[GLA_WIDTH:, :], preferred_element_type=F32))
    x1 = x_ref[...] + mod_ref[2:3, :] * mix
    h = (_rms(x1) * g2_ref[...] * (1.0 + mod_ref[4:5, :]) + mod_ref[3:4, :]).astype(BF16)
    acc = jnp.zeros_like(x1)
    ahead.step()
    for c in range(w1_ref.shape[1] // ff_chunk):
        cs = slice(c * ff_chunk, (c + 1) * ff_chunk)
        a = jnp.maximum(jnp.dot(h, w1_ref[:, cs], preferred_element_type=F32), 0.0)
        ahead.step()
        acc = acc + jnp.dot((a * a).astype(BF16), w2_ref[cs, :], preferred_element_type=F32)
        ahead.step()
    x2 = x1 + mod_ref[5:6, :] * acc
    o_ref[...] = _rms(x2) * gf_ref[...] if final else x2

    o, st = ahead.finish()
    og_ref[...] = o.astype(BF16)
    st_ref[...] = st


def _gla_ffn(x, on, mod, wo, g2, w1, w2, gf, gq, gk, gv, gg, misc, wa_p, ba, gn, final):
    B, S, D = x.shape
    tm = GLA_TILE
    n = S // tm
    row = lambda w: pl.BlockSpec((None, tm, w), lambda b, i: (b, i, 0))
    first = lambda w: pl.BlockSpec((None, tm, w), lambda b, i: (0, 0, 0))
    ahead = lambda w: pl.BlockSpec((None, tm, w), lambda b, i: (
        jnp.where(i + 1 < n, b, jnp.minimum(b + 1, B - 1)), jnp.where(i + 1 < n, i + 1, 0), 0))
    whole = lambda a: pl.BlockSpec(a.shape, lambda b, i: (0,) * a.ndim)
    gla_in = (gq, gk, gv, gg, misc)
    return pl.pallas_call(
        functools.partial(_gla_ffn_kernel, ff_chunk=1024, final=final),
        grid=(B, n),
        in_specs=([row(D), row(on.shape[2]), pl.BlockSpec((None, N_MOD, D), lambda b, i: (b, 0, 0)),
                   whole(wo), whole(g2), whole(w1), whole(w2), whole(gf)]
                  + [first(a.shape[2]) for a in gla_in] + [ahead(a.shape[2]) for a in gla_in]
                  + [whole(wa_p), whole(ba), whole(gn)]),
        out_specs=row(D),
        out_shape=jax.ShapeDtypeStruct((B, S, D), F32),
        scratch_shapes=[pltpu.VMEM((GLA_QK, GLA_WIDTH), F32), pltpu.VMEM((tm, GLA_WIDTH), BF16)],
        compiler_params=_params("arbitrary", "arbitrary"),
        name="gla_ffn",
    )(x, on, mod, wo, g2, w1, w2, gf, *gla_in, *gla_in, wa_p, ba, gn)


def kernel(x, c, positions, w_ada, b_ada, norm1_g, w_in, gla_w_a2, gla_b_a, gla_norm_g, nsa_pe_k, nsa_pe_v, cmp_k_w1, cmp_k_w2, cmp_v_w1, cmp_v_w2, nsa_norm_g, w_o, norm2_g, w_ff1, w_ff2, final_norm_g):
    B, S, D = x.shape
    depth = w_in.shape[0]
    half = NSA_DH // 2
    inv = ROPE_THETA ** (-jnp.arange(half, dtype=F32) / half)
    inv = jnp.concatenate([inv, inv]).reshape(1, NSA_DH)
    for l in range(depth):
        mod = _adaln(c, w_ada[l], b_ada[l]).reshape(B, N_MOD, D)
        (gq, gk, gv, gg, nq, kc, vc, ks, vs, kw, vw, misc) = _in_proj(
            x, mod, norm1_g[l].reshape(1, D), positions, inv, w_in, l, tm=512)

        k_cmp, v_cmp = _compress(kc, vc, cmp_k_w1[l].astype(BF16), _compress_pe(nsa_pe_k[l]),
                                 cmp_k_w2[l].astype(BF16), cmp_v_w1[l].astype(BF16),
                                 _compress_pe(nsa_pe_v[l]), cmp_v_w2[l].astype(BF16), nb=int(np.gcd(B, 4)))
        o_nsa = _nsa(nq, k_cmp, v_cmp, ks, vs, kw, vw, misc, nsa_norm_g[l].reshape(1, NSA_WIDTH),
                     tq=512, tk=256)

        wa_p = jnp.zeros((MISC_W, GLA_QK), F32).at[0:GLA_RANK].set(gla_w_a2[l])
        x = _gla_ffn(x, o_nsa, mod, w_o[l].astype(BF16), norm2_g[l].reshape(1, D),
                     w_ff1[l].astype(BF16), w_ff2[l].astype(BF16), final_norm_g.reshape(1, D),
                     gq, gk, gv, gg, misc, wa_p, gla_b_a[l].reshape(1, GLA_QK),
                     gla_norm_g[l].reshape(1, GLA_WIDTH), final=(l == depth - 1))
    return x
```

```python
import functools

import numpy as np
import jax
import jax.numpy as jnp
from jax import lax
from jax.experimental import pallas as pl
from jax.experimental.pallas import tpu as pltpu

GLA_HEADS = 4
GLA_DK = 64
GLA_DV = 128
GLA_RANK = 16
GLA_TAU = 16.0
GLA_CHUNK = 64
NSA_HEADS = 4
NSA_KV_HEADS = 2
NSA_DH = 128
NSA_BRANCHES = 3
CMP_BLOCK = 32
CMP_STRIDE = 16
CMP_HIDDEN = 256
SEL_BLOCK = 64
SEL_TOPK = 16
WINDOW = 512
N_MOD = 6
ROPE_THETA = 10000.0
EPS = 1e-6
NEG = -1e30
BIG = 1e30

GLA_QK = GLA_HEADS * GLA_DK
GLA_WIDTH = GLA_HEADS * GLA_DV
NSA_WIDTH = NSA_HEADS * NSA_DH
NSA_KV_WIDTH = NSA_KV_HEADS * NSA_DH
N_GATE = NSA_HEADS * NSA_BRANCHES
IN_SPLITS = (GLA_QK, GLA_QK, GLA_WIDTH, GLA_WIDTH, GLA_RANK, NSA_WIDTH) + (NSA_KV_WIDTH,) * 6 + (N_GATE,)

LANES = 128
SUBLANES = 8
MISC_W = LANES
GATE_OFF = GLA_RANK
GLA_TILE = 256
VMEM_LIMIT = 56 * 1024 * 1024

F32 = jnp.float32
BF16 = jnp.bfloat16

_NT = (((1,), (1,)), ((), ()))
_TN = (((0,), (0,)), ((), ()))


def _params(*sem):
    return pltpu.CompilerParams(dimension_semantics=sem, vmem_limit_bytes=VMEM_LIMIT)


def _rms(x):
    return x * lax.rsqrt(jnp.mean(x * x, axis=-1, keepdims=True) + EPS)


def _split_bf16(x):
    hi = x.astype(BF16)
    return hi, (x - hi.astype(F32)).astype(BF16)


def _t_f32(x):
    return x.astype(F32).T


def _adaln_kernel(c_ref, w_ref, b_ref, o_ref):
    c = c_ref[...]
    n = c.shape[0]
    a_hi, a_lo = _split_bf16(c * jax.nn.sigmoid(c))
    w_hi, w_lo = _split_bf16(w_ref[...])
    y = jnp.dot(jnp.concatenate([a_hi, a_lo], axis=0), w_hi, preferred_element_type=F32)
    o_ref[...] = y[0:n] + y[n:2 * n] + jnp.dot(a_hi, w_lo, preferred_element_type=F32) + b_ref[...]


def _adaln(c, w_ada, b_ada):
    B, D = c.shape
    N = w_ada.shape[1]
    tn = D
    return pl.pallas_call(
        _adaln_kernel,
        grid=(N // tn,),
        in_specs=[pl.BlockSpec((B, D), lambda j: (0, 0)),
                  pl.BlockSpec((D, tn), lambda j: (0, j)),
                  pl.BlockSpec((1, tn), lambda j: (0, j))],
        out_specs=pl.BlockSpec((B, tn), lambda j: (0, j)),
        out_shape=jax.ShapeDtypeStruct((B, N), F32),
        compiler_params=_params("arbitrary"),
        name="adaln",
    )(c, w_ada, b_ada.reshape(1, N))


_PROJ_GROUPS = (("gq", GLA_QK, False), ("gk", GLA_QK, False), ("gv", GLA_WIDTH, False),
                ("gg", GLA_WIDTH, False), ("nq", NSA_WIDTH, True), ("kc", NSA_KV_WIDTH, True),
                ("vc", NSA_KV_WIDTH, False), ("ks", NSA_KV_WIDTH, True), ("vs", NSA_KV_WIDTH, False),
                ("kw", NSA_KV_WIDTH, True), ("vw", NSA_KV_WIDTH, False))
_PROJ_W = sum(g[1] for g in _PROJ_GROUPS) + MISC_W


def _w_in_moves():
    names = ("gq", "gk", "gv", "gg", "gr", "nq", "kc", "vc", "ks", "vs", "kw", "vw", "ngate")
    src = dict(zip(names, np.cumsum((0,) + IN_SPLITS[:-1])))
    width = dict(zip(names, IN_SPLITS))
    moves, dst = [], 0
    for name, w, _ in _PROJ_GROUPS:
        moves.append((int(src[name]), dst, w))
        dst += w
    for name in ("gr", "ngate"):
        moves.append((int(src[name]), dst, width[name]))
        dst += width[name]
    return tuple(moves)


_W_IN_MOVES = _w_in_moves()
_GROUPED = ("kc", "vc")
GRP_W = CMP_STRIDE * NSA_DH


def _in_proj_kernel(x_ref, mod_ref, g_ref, pos_ref, inv_ref, win_ref, *refs):
    n_out = len(_PROJ_GROUPS) + 1
    out_refs, stage_refs, w_ref = refs[:n_out], refs[n_out:-1], refs[-1]

    @pl.when((pl.program_id(0) == 0) & (pl.program_id(1) == 0))
    def _():
        for src, dst, width in _W_IN_MOVES:
            w_ref[:, dst:dst + width] = win_ref[:, src:src + width].astype(BF16)
        pad = _PROJ_W - _W_IN_MOVES[-1][1] - _W_IN_MOVES[-1][2]
        w_ref[:, _PROJ_W - pad:] = jnp.zeros((w_ref.shape[0], pad), BF16)

    tm = x_ref.shape[0]
    x = x_ref[...]
    h = _rms(x) * g_ref[...] * (1.0 + mod_ref[1:2, :]) + mod_ref[0:1, :]
    hb = h.astype(BF16)

    qs = NSA_DH ** -0.5 * float(np.log2(np.e))
    cos = sin = None

    off = 0
    for (name, width, rot), o_ref in zip(_PROJ_GROUPS, out_refs[:-1]):
        if rot and cos is None:
            pos = pos_ref[...].astype(F32)
            pos = jnp.concatenate([jnp.broadcast_to(pos[r:r + 1, :], (LANES, LANES)).T
                                   for r in range(pos.shape[0])], axis=0)
            ang = pos * inv_ref[...]
            lane = lax.broadcasted_iota(jnp.int32, (1, LANES), 1)
            cos = jnp.cos(ang)
            sin = jnp.sin(ang) * jnp.where(lane < NSA_DH // 2, -1.0, 1.0)
        y = jnp.dot(hb, w_ref[:, off:off + width], preferred_element_type=F32)
        grouped = name in _GROUPED
        dst = stage_refs[_GROUPED.index(name)] if grouped else o_ref
        for hd in range(width // NSA_DH if (rot or grouped) else 0):
            yh = y[:, hd * NSA_DH:(hd + 1) * NSA_DH]
            if rot:
                c, s = (cos * qs, sin * qs) if name == "nq" else (cos, sin)
                yh = yh * c + pltpu.roll(yh, NSA_DH // 2, axis=1) * s
            if grouped:
                dst[hd] = yh
            else:
                dst[:, hd * NSA_DH:(hd + 1) * NSA_DH] = yh.astype(dst.dtype)
        if not (rot or grouped):
            dst[...] = y.astype(dst.dtype)
        if grouped:
            for k in range(NSA_KV_HEADS):
                for tok in range(CMP_STRIDE):
                    c0 = k * GRP_W + tok * NSA_DH
                    o_ref[:, c0:c0 + NSA_DH] = dst[k, pl.ds(tok, tm // CMP_STRIDE, stride=CMP_STRIDE), :].astype(
                        o_ref.dtype)
        off += width
    out_refs[-1][...] = jnp.dot(hb, w_ref[:, off:off + MISC_W], preferred_element_type=F32)


def _in_proj(x, mod, norm_g, positions, inv, w_in, layer, tm):
    B, S, D = x.shape
    pos4 = positions.reshape(B, S // tm, tm // LANES, LANES)
    row = lambda w: pl.BlockSpec((None, tm, w), lambda b, i: (b, i, 0))
    whole = lambda a: pl.BlockSpec(a.shape, lambda b, i: (0,) * a.ndim)
    out_shape, out_specs = [], []
    for name, w, _ in _PROJ_GROUPS:
        if name in _GROUPED:
            out_shape.append(jax.ShapeDtypeStruct((B, S // CMP_STRIDE, CMP_STRIDE * w), BF16))
            out_specs.append(pl.BlockSpec((None, tm // CMP_STRIDE, CMP_STRIDE * w), lambda b, i: (b, i, 0)))
        else:
            out_shape.append(jax.ShapeDtypeStruct((B, S, w), BF16))
            out_specs.append(row(w))
    out_shape.append(jax.ShapeDtypeStruct((B, S, MISC_W), F32))
    out_specs.append(row(MISC_W))
    return pl.pallas_call(
        _in_proj_kernel,
        grid=(B, S // tm),
        in_specs=[row(D), pl.BlockSpec((None, N_MOD, D), lambda b, i: (b, 0, 0)), whole(norm_g),
                  pl.BlockSpec((None, None, tm // LANES, LANES), lambda b, i: (b, i, 0, 0)), whole(inv),
                  pl.BlockSpec((None,) + w_in.shape[1:], lambda b, i: (layer, 0, 0),
                               pipeline_mode=pl.Buffered(1))],
        out_specs=out_specs,
        out_shape=out_shape,
        scratch_shapes=([pltpu.VMEM((NSA_KV_HEADS, tm, NSA_DH), F32) for _ in _GROUPED]
                        + [pltpu.VMEM((D, _PROJ_W), BF16)]),
        compiler_params=_params("arbitrary", "arbitrary"),
        name="in_proj",
    )(x, mod, norm_g, pos4, inv, w_in)


def _compress_kernel(xk_ref, xv_ref, kw1_ref, kpe_ref, kw2_ref, vw1_ref, vpe_ref, vw2_ref, ok_ref, ov_ref):
    nb, n_grp, _ = xk_ref.shape
    rows = nb * n_grp
    for x_ref, w1_ref, pe_ref, w2_ref, o_ref in ((xk_ref, kw1_ref, kpe_ref, kw2_ref, ok_ref),
                                                 (xv_ref, vw1_ref, vpe_ref, vw2_ref, ov_ref)):
        w1a = w1_ref[0:GRP_W, :]
        w1b = w1_ref[GRP_W:2 * GRP_W, :]
        x = jnp.concatenate([x_ref[:, :, k * GRP_W:(k + 1) * GRP_W].reshape(rows, GRP_W)
                             for k in range(NSA_KV_HEADS)], axis=0)
        bias = (jnp.dot(pe_ref[:, 0:GRP_W], w1a, preferred_element_type=F32)[0:1]
                + jnp.dot(pe_ref[:, GRP_W:2 * GRP_W], w1b, preferred_element_type=F32)[0:1])
        ua = jnp.dot(x, w1a, preferred_element_type=F32)
        ub = jnp.dot(x, w1b, preferred_element_type=F32)
        hid = ua + pltpu.roll(ub, NSA_KV_HEADS * rows - 1, axis=0) + bias
        act = jax.nn.gelu(hid, approximate=True).astype(BF16)
        y = jnp.dot(act, w2_ref[...], preferred_element_type=F32).astype(o_ref.dtype)
        for k in range(NSA_KV_HEADS):
            o_ref[:, :, k * NSA_DH:(k + 1) * NSA_DH] = y[k * rows:(k + 1) * rows].reshape(nb, n_grp, NSA_DH)


def _compress(xk, xv, kw1, kpe, kw2, vw1, vpe, vw2, nb):
    B, n_grp, gw = xk.shape
    W = NSA_KV_WIDTH
    whole = lambda a: pl.BlockSpec(a.shape, lambda b: (0,) * a.ndim)
    xspec = pl.BlockSpec((nb, n_grp, gw), lambda b: (b, 0, 0))
    ospec = pl.BlockSpec((nb, n_grp, W), lambda b: (b, 0, 0))
    return pl.pallas_call(
        _compress_kernel,
        grid=(B // nb,),
        in_specs=[xspec, xspec, whole(kw1), whole(kpe), whole(kw2), whole(vw1), whole(vpe), whole(vw2)],
        out_specs=[ospec, ospec],
        out_shape=[jax.ShapeDtypeStruct((B, n_grp, W), BF16)] * 2,
        compiler_params=_params("parallel"),
        name="compress",
    )(xk, xv, kw1, kpe, kw2, vw1, vpe, vw2)


def _compress_pe(pe):
    return jnp.zeros((8, 2 * GRP_W), F32).at[0].set(pe.reshape(2 * GRP_W)).astype(BF16)


V_AUG = NSA_DH + 16


def _attend(k_rows, qt, vt_aug, masked):
    s = jnp.dot(k_rows(), qt, preferred_element_type=F32)
    if masked is not None:
        s = s + masked()
    yield
    m = jnp.max(s, axis=0, keepdims=True)
    p = jnp.exp2((s - m).astype(BF16))
    yield
    return m, jnp.dot(vt_aug(), p, preferred_element_type=F32)


def _pipelined(gens, depth=1, side=()):
    tasks = [_Staged(g) for g in gens]
    for n in range(len(tasks) + depth):
        for t in tasks[max(0, n - depth):n + 1]:
            t.step()
        for s in side:
            s.step()
    return [t.finish() for t in tasks]


def _merge(parts):
    m_all = parts[0][0]
    for m, _ in parts[1:]:
        m_all = jnp.maximum(m_all, m)
    tot = None
    for m, acc in parts:
        w = acc * jnp.exp2(m - m_all)
        tot = w if tot is None else tot + w
    return tot[0:NSA_DH] / tot[NSA_DH:NSA_DH + 1]


def _nsa_step(c, tq, tk, q_ref, kc_ref, vc_ref, kw_ref, misc_ref, gn_ref, o_ref, kaug_ref, vst_ref, vwt_ref):
    S = kw_ref.shape[0]
    n_cmp = kc_ref.shape[0]
    n_sel = S // SEL_BLOCK
    G = NSA_HEADS // NSA_KV_HEADS
    M = G * tq
    q0 = c * tq

    t_q = q0 + lax.broadcasted_iota(jnp.int32, (1, tq), 1)
    t_m = jnp.concatenate([t_q] * G, axis=1)
    key_off = lax.broadcasted_iota(jnp.int32, (tk, 1), 0)

    n_col = lax.broadcasted_iota(jnp.int32, (n_cmp, 1), 0)
    cmp_end = jnp.where(n_col < n_cmp - 1, n_col * CMP_STRIDE + (CMP_BLOCK - 1), jnp.int32(2 ** 30))
    cmp_valid = cmp_end <= t_m
    any_valid = (t_m >= CMP_BLOCK - 1).astype(F32)
    jj = lax.broadcasted_iota(jnp.int32, (n_sel, n_cmp), 0) * SEL_BLOCK
    nn = lax.broadcasted_iota(jnp.int32, (n_sel, n_cmp), 1) * CMP_STRIDE
    ov_t = jnp.maximum(jnp.minimum(nn + CMP_BLOCK, jj + SEL_BLOCK) - jnp.maximum(nn, jj), 0)
    ov_t = (ov_t.astype(F32) * (1.0 / CMP_BLOCK)).astype(BF16)
    j_row = lax.broadcasted_iota(jnp.int32, (n_sel, tq), 0)
    blk_t = lax.shift_right_logical(q0 + lax.broadcasted_iota(jnp.int32, (n_sel, tq), 1), 6)
    forced = (j_row == 0) | (j_row == blk_t) | (j_row == blk_t - 1)
    in_past = j_row <= blk_t

    heads = range(NSA_KV_HEADS)
    qts = [jnp.concatenate([_t_f32(q_ref[:, (k * G + g) * NSA_DH:(k * G + g + 1) * NSA_DH])
                            for g in range(G)], axis=1).astype(BF16) for k in heads]

    def select(k):
        hs = slice(k * NSA_DH, (k + 1) * NSA_DH)
        s = jnp.dot(kc_ref[:, hs], qts[k], preferred_element_type=F32)
        s = jnp.where(cmp_valid, s, NEG)
        yield
        e = jnp.exp2(s - jnp.max(s, axis=0, keepdims=True))
        p = e / jnp.sum(e, axis=0, keepdims=True) * any_valid
        yield
        o_cmp = jnp.dot(_t_f32(vc_ref[:, hs]).astype(BF16), p.astype(BF16), preferred_element_type=F32)
        p_grp = p[:, 0:tq]
        for g in range(1, G):
            p_grp = p_grp + p[:, g * tq:(g + 1) * tq]
        p_hi, p_lo = _split_bf16(p_grp)
        imp = (jnp.dot(ov_t, p_hi, preferred_element_type=F32)
               + jnp.dot(ov_t, p_lo, preferred_element_type=F32))
        imp = jnp.where(forced, BIG, jnp.where(in_past, imp, NEG))
        yield
        rank = []
        for r0 in range(0, n_sel, SUBLANES):
            blk = imp[r0:r0 + SUBLANES, :]
            cnt = jnp.zeros(blk.shape, F32)
            for j in range(n_sel):
                row = imp[j:j + 1, :]
                if j < r0:
                    cnt = cnt + jnp.where(row >= blk, 1.0, 0.0)
                elif j >= r0 + SUBLANES - 1:
                    cnt = cnt + jnp.where(row > blk, 1.0, 0.0)
                else:
                    below = r0 + lax.broadcasted_iota(jnp.int32, blk.shape, 0) > j
                    cnt = cnt + jnp.where(below, jnp.where(row >= blk, 1.0, 0.0), jnp.where(row > blk, 1.0, 0.0))
            rank.append(cnt)
            yield
        rank = jnp.concatenate(rank, axis=0)
        bias = jnp.where(rank < float(min(SEL_TOPK, n_sel)), 0.0, NEG).astype(BF16)
        return o_cmp, jnp.concatenate([qts[k], jnp.concatenate([bias] * G, axis=1),
                                       jnp.zeros((NSA_DH - n_sel, M), BF16)], axis=0)

    subs = range(tq // tk)
    col_off = jnp.concatenate([lax.broadcasted_iota(jnp.int32, (1, tk), 1)] * G, axis=1)
    after_bias = jnp.where(key_off > col_off, NEG, 0.0)
    far_bias = jnp.where(key_off <= col_off, NEG, 0.0)
    after = lambda: after_bias
    too_far = lambda: far_bias

    def cols(x, j):
        return jnp.concatenate([x[:, g * tq + j * tk:g * tq + (j + 1) * tk] for g in range(G)], axis=1)

    def sel_tasks(k, j, qta):
        first, diag = (0, q0 // tk - 1) if j is None else (q0 // tk, (q0 + j * tk) // tk)
        for kt in range(first, diag + 1):
            yield _attend(lambda kt=kt: kaug_ref[k, kt * tk:(kt + 1) * tk, :], qta,
                          lambda kt=kt: vst_ref[k, kt], after if kt == diag and j is not None else None)

    def win_tasks(k, j, qt):
        hs = slice(k * NSA_DH, (k + 1) * NSA_DH)
        diag = (q0 + j * tk) // tk
        back = WINDOW // tk
        for kt in range(max(diag - back, 0), diag + 1):
            masked = after if kt == diag else too_far if kt == diag - back else None
            yield _attend(lambda kt=kt: kw_ref[kt * tk:(kt + 1) * tk, hs], qt, lambda kt=kt: vwt_ref[k, kt], masked)

    def run(task_lists, side=()):
        flat = [(key, t) for key, tasks in task_lists.items() for t in tasks]
        parts = {key: [] for key in task_lists}
        for (key, _), part in zip(flat, _pipelined([t for _, t in flat], side=side)):
            parts[key].append(part)
        merged = {}
        for k in heads:
            for j in subs:
                shared = [(cols(m, j), cols(acc, j)) for m, acc in parts.get((k, None), [])]
                merged[k, j] = _merge(shared + parts[k, j])
        return [jnp.concatenate([merged[k, j][:, g * tk:(g + 1) * tk] for g in range(G) for j in subs], axis=1)
                for k in heads]

    selects = [_Staged(select(k)) for k in heads]
    o_win = run({(k, j): list(win_tasks(k, j, cols(qts[k], j))) for k in heads for j in subs}, side=selects)
    o_cmp, qtas = zip(*[s.finish() for s in selects])
    o_sel = run({(k, j): list(sel_tasks(k, j, qtas[k] if j is None else cols(qtas[k], j)))
                 for k in heads for j in (None, *subs)})


    gates_t = _t_f32(jax.nn.sigmoid(misc_ref[...]))
    heads = []
    for k in range(NSA_KV_HEADS):
        for g in range(G):
            gs = slice(g * tq, (g + 1) * tq)
            gl = GATE_OFF + (k * G + g) * NSA_BRANCHES
            heads.append(gates_t[gl:gl + 1, :] * o_cmp[k][:, gs] + gates_t[gl + 1:gl + 2, :] * o_sel[k][:, gs]
                         + gates_t[gl + 2:gl + 3, :] * o_win[k][:, gs])
    ssq = heads[0] * heads[0]
    for o in heads[1:]:
        ssq = ssq + o * o
    inv = lax.rsqrt(jnp.sum(ssq, axis=0, keepdims=True) * (1.0 / NSA_WIDTH) + EPS)
    for hq, o in enumerate(heads):
        cs = slice(hq * NSA_DH, (hq + 1) * NSA_DH)
        o_ref[:, cs] = ((o * inv).T * gn_ref[:, cs]).astype(o_ref.dtype)


def _nsa_kernel(q_ref, kc_ref, vc_ref, ks_ref, vs_ref, kw_ref, vw_ref, misc_ref, gn_ref, o_ref,
                kaug_ref, vst_ref, vwt_ref):
    tq = q_ref.shape[0]
    tk = vst_ref.shape[3]
    S = ks_ref.shape[0]
    i = pl.program_id(1)

    @pl.when(i == 0)
    def _():
        pos = lax.broadcasted_iota(jnp.int32, (S, LANES), 0)
        lane = lax.broadcasted_iota(jnp.int32, (S, LANES), 1)
        onehot = jnp.where(lax.shift_right_logical(pos, 6) == lane, 1.0, 0.0).astype(BF16)
        row = lax.broadcasted_iota(jnp.int32, (V_AUG - NSA_DH, tk), 0)
        ones_rows = jnp.where(row == 0, 1.0, 0.0).astype(BF16)
        for k in range(NSA_KV_HEADS):
            hs = slice(k * NSA_DH, (k + 1) * NSA_DH)
            kaug_ref[k, :, 0:NSA_DH] = ks_ref[:, hs]
            kaug_ref[k, :, NSA_DH:2 * NSA_DH] = onehot
            for kt in range(S // tk):
                rows = slice(kt * tk, (kt + 1) * tk)
                for src, dst in ((vs_ref, vst_ref), (vw_ref, vwt_ref)):
                    dst[k, kt, 0:NSA_DH, :] = _t_f32(src[rows, hs]).astype(BF16)
                    dst[k, kt, NSA_DH:V_AUG, :] = ones_rows

    for c in range(S // tq):
        pl.when(i == c)(functools.partial(_nsa_step, c, tq, tk, q_ref, kc_ref, vc_ref, kw_ref, misc_ref,
                                          gn_ref, o_ref, kaug_ref, vst_ref, vwt_ref))


def _nsa(nq, kcmp, vcmp, ks, vs, kw, vw, misc, gn, tq, tk):
    B, S, _ = nq.shape
    n_cmp = kcmp.shape[1]
    assert n_cmp <= LANES and S // SEL_BLOCK <= NSA_DH and SEL_BLOCK == 64
    assert tq % tk == 0 and tq <= WINDOW and WINDOW % tk == 0
    row = lambda w: pl.BlockSpec((None, tq, w), lambda b, i: (b, i, 0))
    seq = lambda n: pl.BlockSpec((None, n, NSA_KV_WIDTH), lambda b, i: (b, 0, 0))
    vt_scratch = pltpu.VMEM((NSA_KV_HEADS, S // tk, V_AUG, tk), BF16)
    return pl.pallas_call(
        _nsa_kernel,
        grid=(B, S // tq),
        in_specs=[row(NSA_WIDTH), seq(n_cmp), seq(n_cmp), seq(S), seq(S), seq(S), seq(S), row(MISC_W),
                  pl.BlockSpec(gn.shape, lambda b, i: (0, 0))],
        out_specs=row(NSA_WIDTH),
        out_shape=jax.ShapeDtypeStruct((B, S, NSA_WIDTH), BF16),
        scratch_shapes=[pltpu.VMEM((NSA_KV_HEADS, S, 2 * NSA_DH), BF16), vt_scratch, vt_scratch],
        compiler_params=_params("parallel", "arbitrary"),
        name="nsa",
    )(nq, kcmp, vcmp, ks, vs, kw, vw, misc, gn)


def _log_sigmoid(z):
    return jnp.minimum(z, 0.0) - jnp.log(1.0 + jnp.exp(-jnp.abs(z)))


class _Staged:
    def __init__(self, gen):
        self.gen, self.value, self.done = gen, None, False

    def step(self):
        if not self.done:
            try:
                next(self.gen)
            except StopIteration as stop:
                self.value, self.done = stop.value, True

    def finish(self):
        while not self.done:
            self.step()
        return self.value


def _gla_tile(q, k, v, g, misc, wa, ba, gn, st):
    C = GLA_CHUNK
    T = q.shape[0]
    n_chunk = T // C
    r = lax.broadcasted_iota(jnp.int32, (T, T), 0)
    c = lax.broadcasted_iota(jnp.int32, (T, T), 1)
    causal = (r >= c) & (r - c <= (r & (C - 1)))
    tri = jnp.where(causal, 1.0, 0.0).astype(BF16)
    sr = lax.broadcasted_iota(jnp.int32, st.shape, 0)
    sc = lax.broadcasted_iota(jnp.int32, st.shape, 1)
    own_head = (lax.shift_right_logical(sr, GLA_DK.bit_length() - 1)
                == lax.shift_right_logical(sc, GLA_DV.bit_length() - 1))

    z = jnp.dot(misc.astype(BF16), wa.astype(BF16), preferred_element_type=F32) + ba
    la = _log_sigmoid(z) * (1.0 / GLA_TAU)
    yield
    b = jnp.dot(tri, la.astype(BF16), preferred_element_type=F32)
    b_lasts = [b[(ci + 1) * C - 1:(ci + 1) * C, :] for ci in range(n_chunk)]
    b_last = jnp.concatenate([jnp.broadcast_to(bl, (C, bl.shape[1])) for bl in b_lasts], axis=0)
    qf = q.astype(F32)
    kf = k.astype(F32)
    q_in = (qf * GLA_DK ** -0.5 * jnp.exp(b)).astype(BF16)
    k_in = (kf * jnp.exp(-b)).astype(BF16)
    k_dec = (kf * jnp.exp(b_last - b)).astype(BF16)
    pad = jnp.zeros((LANES - n_chunk, b.shape[1]), F32)
    dec_cols = jnp.exp(jnp.concatenate(b_lasts + [pad], axis=0).T)
    yield

    o_intra = []
    for h in range(GLA_HEADS):
        ks = slice(h * GLA_DK, (h + 1) * GLA_DK)
        att = lax.dot_general(q_in[:, ks], k_in[:, ks], _NT, preferred_element_type=F32)
        att = jnp.where(causal, att, 0.0).astype(BF16)
        o_intra.append(jnp.dot(att, v[:, h * GLA_DV:(h + 1) * GLA_DV], preferred_element_type=F32))
        yield

    o_inter = []
    for ci in range(n_chunk):
        rows = slice(ci * C, (ci + 1) * C)
        o_inter.append(jnp.dot(q_in[rows], st.astype(BF16), preferred_element_type=F32))
        d_st = lax.dot_general(k_dec[rows], v[rows], _TN, preferred_element_type=F32)
        st = st * dec_cols[:, ci:ci + 1] + jnp.where(own_head, d_st, 0.0)
        yield
    o_inter = jnp.concatenate(o_inter, axis=0)
    gate = g.astype(F32)
    gate = gate * jax.nn.sigmoid(gate)
    outs = []
    for h in range(GLA_HEADS):
        vs = slice(h * GLA_DV, (h + 1) * GLA_DV)
        outs.append(_rms(o_intra[h] + o_inter[:, vs]) * gn[:, vs] * gate[:, vs])
    return jnp.concatenate(outs, axis=1), st


def _gla_ffn_kernel(x_ref, on_ref, mod_ref, wo_ref, g2_ref, w1_ref, w2_ref, gf_ref,
                    q0_ref, k0_ref, v0_ref, g0_ref, m0_ref, q1_ref, k1_ref, v1_ref, g1_ref, m1_ref,
                    wa_ref, ba_ref, gn_ref, o_ref, st_ref, og_ref, *, ff_chunk, final):
    def gla(q_ref, k_ref, v_ref, g_ref, m_ref, st):
        return _Staged(_gla_tile(q_ref[...], k_ref[...], v_ref[...], g_ref[...], m_ref[...],
                                 wa_ref[...], ba_ref[...], gn_ref[...], st))

    @pl.when((pl.program_id(0) == 0) & (pl.program_id(1) == 0))
    def _():
        o, st = gla(q0_ref, k0_ref, v0_ref, g0_ref, m0_ref, jnp.zeros(st_ref.shape, F32)).finish()
        og_ref[...] = o.astype(BF16)
        st_ref[...] = st

    last = pl.program_id(1) == pl.num_programs(1) - 1
    ahead = gla(q1_ref, k1_ref, v1_ref, g1_ref, m1_ref, jnp.where(last, 0.0, st_ref[...]))

    ahead.step()
    mix = (jnp.dot(og_ref[...], wo_ref[0:GLA_WIDTH, :], preferred_element_type=F32)
           + jnp.dot(on_ref[...], wo_ref[GLA_WIDTH:, :], preferred_element_type=F32))
    x1 = x_ref[...] + mod_ref[2:3, :] * mix
    h = (_rms(x1) * g2_ref[...] * (1.0 + mod_ref[4:5, :]) + mod_ref[3:4, :]).astype(BF16)
    acc = jnp.zeros_like(x1)
    ahead.step()
    for c in range(w1_ref.shape[1] // ff_chunk):
        cs = slice(c * ff_chunk, (c + 1) * ff_chunk)
        a = jnp.maximum(jnp.dot(h, w1_ref[:, cs], preferred_element_type=F32), 0.0)
        ahead.step()
        acc = acc + jnp.dot((a * a).astype(BF16), w2_ref[cs, :], preferred_element_type=F32)
        ahead.step()
    x2 = x1 + mod_ref[5:6, :] * acc
    o_ref[...] = _rms(x2) * gf_ref[...] if final else x2

    o, st = ahead.finish()
    og_ref[...] = o.astype(BF16)
    st_ref[...] = st


def _gla_ffn(x, on, mod, wo, g2, w1, w2, gf, gq, gk, gv, gg, misc, wa_p, ba, gn, final):
    B, S, D = x.shape
    tm = GLA_TILE
    n = S // tm
    row = lambda w: pl.BlockSpec((None, tm, w), lambda b, i: (b, i, 0))
    first = lambda w: pl.BlockSpec((None, tm, w), lambda b, i: (0, 0, 0))
    ahead = lambda w: pl.BlockSpec((None, tm, w), lambda b, i: (
        jnp.where(i + 1 < n, b, jnp.minimum(b + 1, B - 1)), jnp.where(i + 1 < n, i + 1, 0), 0))
    whole = lambda a: pl.BlockSpec(a.shape, lambda b, i: (0,) * a.ndim)
    gla_in = (gq, gk, gv, gg, misc)
    return pl.pallas_call(
        functools.partial(_gla_ffn_kernel, ff_chunk=1024, final=final),
        grid=(B, n),
        in_specs=([row(D), row(on.shape[2]), pl.BlockSpec((None, N_MOD, D), lambda b, i: (b, 0, 0)),
                   whole(wo), whole(g2), whole(w1), whole(w2), whole(gf)]
                  + [first(a.shape[2]) for a in gla_in] + [ahead(a.shape[2]) for a in gla_in]
                  + [whole(wa_p), whole(ba), whole(gn)]),
        out_specs=row(D),
        out_shape=jax.ShapeDtypeStruct((B, S, D), F32),
        scratch_shapes=[pltpu.VMEM((GLA_QK, GLA_WIDTH), F32), pltpu.VMEM((tm, GLA_WIDTH), BF16)],
        compiler_params=_params("arbitrary", "arbitrary"),
        name="gla_ffn",
    )(x, on, mod, wo, g2, w1, w2, gf, *gla_in, *gla_in, wa_p, ba, gn)


def kernel(x, c, positions, w_ada, b_ada, norm1_g, w_in, gla_w_a2, gla_b_a, gla_norm_g, nsa_pe_k, nsa_pe_v, cmp_k_w1, cmp_k_w2, cmp_v_w1, cmp_v_w2, nsa_norm_g, w_o, norm2_g, w_ff1, w_ff2, final_norm_g):
    B, S, D = x.shape
    depth = w_in.shape[0]
    half = NSA_DH // 2
    inv = ROPE_THETA ** (-jnp.arange(half, dtype=F32) / half)
    inv = jnp.concatenate([inv, inv]).reshape(1, NSA_DH)
    for l in range(depth):
        mod = _adaln(c, w_ada[l], b_ada[l]).reshape(B, N_MOD, D)
        (gq, gk, gv, gg, nq, kc, vc, ks, vs, kw, vw, misc) = _in_proj(
            x, mod, norm1_g[l].reshape(1, D), positions, inv, w_in, l, tm=512)

        k_cmp, v_cmp = _compress(kc, vc, cmp_k_w1[l].astype(BF16), _compress_pe(nsa_pe_k[l]),
                                 cmp_k_w2[l].astype(BF16), cmp_v_w1[l].astype(BF16),
                                 _compress_pe(nsa_pe_v[l]), cmp_v_w2[l].astype(BF16), nb=int(np.gcd(B, 4)))
        o_nsa = _nsa(nq, k_cmp, v_cmp, ks, vs, kw, vw, misc, nsa_norm_g[l].reshape(1, NSA_WIDTH),
                     tq=512, tk=256)

        wa_p = jnp.zeros((MISC_W, GLA_QK), F32).at[0:GLA_RANK].set(gla_w_a2[l])
        x = _gla_ffn(x, o_nsa, mod, w_o[l].astype(BF16), norm2_g[l].reshape(1, D),
                     w_ff1[l].astype(BF16), w_ff2[l].astype(BF16), final_norm_g.reshape(1, D),
                     gq, gk, gv, gg, misc, wa_p, gla_b_a[l].reshape(1, GLA_QK),
                     gla_norm_g[l].reshape(1, GLA_WIDTH), final=(l == depth - 1))
    return x
```

```python
import functools

import numpy as np
import jax
import jax.numpy as jnp
from jax import lax
from jax.experimental import pallas as pl
from jax.experimental.pallas import tpu as pltpu

GLA_HEADS = 4
GLA_DK = 64
GLA_DV = 128
GLA_RANK = 16
GLA_TAU = 16.0
GLA_CHUNK = 64
NSA_HEADS = 4
NSA_KV_HEADS = 2
NSA_DH = 128
NSA_BRANCHES = 3
CMP_BLOCK = 32
CMP_STRIDE = 16
CMP_HIDDEN = 256
SEL_BLOCK = 64
SEL_TOPK = 16
WINDOW = 512
N_MOD = 6
ROPE_THETA = 10000.0
EPS = 1e-6
NEG = -1e30
BIG = 1e30

GLA_QK = GLA_HEADS * GLA_DK
GLA_WIDTH = GLA_HEADS * GLA_DV
NSA_WIDTH = NSA_HEADS * NSA_DH
NSA_KV_WIDTH = NSA_KV_HEADS * NSA_DH
N_GATE = NSA_HEADS * NSA_BRANCHES
IN_SPLITS = (GLA_QK, GLA_QK, GLA_WIDTH, GLA_WIDTH, GLA_RANK, NSA_WIDTH) + (NSA_KV_WIDTH,) * 6 + (N_GATE,)

LANES = 128
SUBLANES = 8
MISC_W = LANES
GATE_OFF = GLA_RANK
GLA_TILE = 256
VMEM_LIMIT = 56 * 1024 * 1024

F32 = jnp.float32
BF16 = jnp.bfloat16

_NT = (((1,), (1,)), ((), ()))
_TN = (((0,), (0,)), ((), ()))


def _params(*sem):
    return pltpu.CompilerParams(dimension_semantics=sem, vmem_limit_bytes=VMEM_LIMIT)


def _rms(x):
    return x * lax.rsqrt(jnp.mean(x * x, axis=-1, keepdims=True) + EPS)


def _split_bf16(x):
    hi = x.astype(BF16)
    return hi, (x - hi.astype(F32)).astype(BF16)


def _t_f32(x):
    return x.astype(F32).T


def _adaln_kernel(c_ref, w_ref, b_ref, o_ref):
    c = c_ref[...]
    n = c.shape[0]
    a_hi, a_lo = _split_bf16(c * jax.nn.sigmoid(c))
    w_hi, w_lo = _split_bf16(w_ref[...])
    y = jnp.dot(jnp.concatenate([a_hi, a_lo], axis=0), w_hi, preferred_element_type=F32)
    o_ref[...] = y[0:n] + y[n:2 * n] + jnp.dot(a_hi, w_lo, preferred_element_type=F32) + b_ref[...]


def _adaln(c, w_ada, b_ada):
    B, D = c.shape
    N = w_ada.shape[1]
    tn = D
    return pl.pallas_call(
        _adaln_kernel,
        grid=(N // tn,),
        in_specs=[pl.BlockSpec((B, D), lambda j: (0, 0)),
                  pl.BlockSpec((D, tn), lambda j: (0, j)),
                  pl.BlockSpec((1, tn), lambda j: (0, j))],
        out_specs=pl.BlockSpec((B, tn), lambda j: (0, j)),
        out_shape=jax.ShapeDtypeStruct((B, N), F32),
        compiler_params=_params("arbitrary"),
        name="adaln",
    )(c, w_ada, b_ada.reshape(1, N))


_PROJ_GROUPS = (("gq", GLA_QK, False), ("gk", GLA_QK, False), ("gv", GLA_WIDTH, False),
                ("gg", GLA_WIDTH, False), ("nq", NSA_WIDTH, True), ("kc", NSA_KV_WIDTH, True),
                ("vc", NSA_KV_WIDTH, False), ("ks", NSA_KV_WIDTH, True), ("vs", NSA_KV_WIDTH, False),
                ("kw", NSA_KV_WIDTH, True), ("vw", NSA_KV_WIDTH, False))
_PROJ_W = sum(g[1] for g in _PROJ_GROUPS) + MISC_W


def _w_in_moves():
    names = ("gq", "gk", "gv", "gg", "gr", "nq", "kc", "vc", "ks", "vs", "kw", "vw", "ngate")
    src = dict(zip(names, np.cumsum((0,) + IN_SPLITS[:-1])))
    width = dict(zip(names, IN_SPLITS))
    moves, dst = [], 0
    for name, w, _ in _PROJ_GROUPS:
        moves.append((int(src[name]), dst, w))
        dst += w
    for name in ("gr", "ngate"):
        moves.append((int(src[name]), dst, width[name]))
        dst += width[name]
    return tuple(moves)


_W_IN_MOVES = _w_in_moves()
_GROUPED = ("kc", "vc")
GRP_W = CMP_STRIDE * NSA_DH


def _in_proj_kernel(x_ref, mod_ref, g_ref, pos_ref, inv_ref, win_ref, *refs):
    n_out = len(_PROJ_GROUPS) + 1
    out_refs, stage_refs, w_ref = refs[:n_out], refs[n_out:-1], refs[-1]

    @pl.when((pl.program_id(0) == 0) & (pl.program_id(1) == 0))
    def _():
        for src, dst, width in _W_IN_MOVES:
            w_ref[:, dst:dst + width] = win_ref[:, src:src + width].astype(BF16)
        pad = _PROJ_W - _W_IN_MOVES[-1][1] - _W_IN_MOVES[-1][2]
        w_ref[:, _PROJ_W - pad:] = jnp.zeros((w_ref.shape[0], pad), BF16)

    tm = x_ref.shape[0]
    x = x_ref[...]
    h = _rms(x) * g_ref[...] * (1.0 + mod_ref[1:2, :]) + mod_ref[0:1, :]
    hb = h.astype(BF16)

    qs = NSA_DH ** -0.5 * float(np.log2(np.e))
    cos = sin = None

    off = 0
    for (name, width, rot), o_ref in zip(_PROJ_GROUPS, out_refs[:-1]):
        if rot and cos is None:
            pos = pos_ref[...].astype(F32)
            pos = jnp.concatenate([jnp.broadcast_to(pos[r:r + 1, :], (LANES, LANES)).T
                                   for r in range(pos.shape[0])], axis=0)
            ang = pos * inv_ref[...]
            lane = lax.broadcasted_iota(jnp.int32, (1, LANES), 1)
            cos = jnp.cos(ang)
            sin = jnp.sin(ang) * jnp.where(lane < NSA_DH // 2, -1.0, 1.0)
        y = jnp.dot(hb, w_ref[:, off:off + width], preferred_element_type=F32)
        grouped = name in _GROUPED
        dst = stage_refs[_GROUPED.index(name)] if grouped else o_ref
        for hd in range(width // NSA_DH if (rot or grouped) else 0):
            yh = y[:, hd * NSA_DH:(hd + 1) * NSA_DH]
            if rot:
                c, s = (cos * qs, sin * qs) if name == "nq" else (cos, sin)
                yh = yh * c + pltpu.roll(yh, NSA_DH // 2, axis=1) * s
            if grouped:
                dst[hd] = yh
            else:
                dst[:, hd * NSA_DH:(hd + 1) * NSA_DH] = yh.astype(dst.dtype)
        if not (rot or grouped):
            dst[...] = y.astype(dst.dtype)
        if grouped:
            for k in range(NSA_KV_HEADS):
                for tok in range(CMP_STRIDE):
                    c0 = k * GRP_W + tok * NSA_DH
                    o_ref[:, c0:c0 + NSA_DH] = dst[k, pl.ds(tok, tm // CMP_STRIDE, stride=CMP_STRIDE), :].astype(
                        o_ref.dtype)
        off += width
    out_refs[-1][...] = jnp.dot(hb, w_ref[:, off:off + MISC_W], preferred_element_type=F32)


def _in_proj(x, mod, norm_g, positions, inv, w_in, layer, tm):
    B, S, D = x.shape
    pos4 = positions.reshape(B, S // tm, tm // LANES, LANES)
    row = lambda w: pl.BlockSpec((None, tm, w), lambda b, i: (b, i, 0))
    whole = lambda a: pl.BlockSpec(a.shape, lambda b, i: (0,) * a.ndim)
    out_shape, out_specs = [], []
    for name, w, _ in _PROJ_GROUPS:
        if name in _GROUPED:
            out_shape.append(jax.ShapeDtypeStruct((B, S // CMP_STRIDE, CMP_STRIDE * w), BF16))
            out_specs.append(pl.BlockSpec((None, tm // CMP_STRIDE, CMP_STRIDE * w), lambda b, i: (b, i, 0)))
        else:
            out_shape.append(jax.ShapeDtypeStruct((B, S, w), BF16))
            out_specs.append(row(w))
    out_shape.append(jax.ShapeDtypeStruct((B, S, MISC_W), F32))
    out_specs.append(row(MISC_W))
    return pl.pallas_call(
        _in_proj_kernel,
        grid=(B, S // tm),
        in_specs=[row(D), pl.BlockSpec((None, N_MOD, D), lambda b, i: (b, 0, 0)), whole(norm_g),
                  pl.BlockSpec((None, None, tm // LANES, LANES), lambda b, i: (b, i, 0, 0)), whole(inv),
                  pl.BlockSpec((None,) + w_in.shape[1:], lambda b, i: (layer, 0, 0),
                               pipeline_mode=pl.Buffered(1))],
        out_specs=out_specs,
        out_shape=out_shape,
        scratch_shapes=([pltpu.VMEM((NSA_KV_HEADS, tm, NSA_DH), F32) for _ in _GROUPED]
                        + [pltpu.VMEM((D, _PROJ_W), BF16)]),
        compiler_params=_params("arbitrary", "arbitrary"),
        name="in_proj",
    )(x, mod, norm_g, pos4, inv, w_in)


def _compress_kernel(xk_ref, xv_ref, kw1_ref, kpe_ref, kw2_ref, vw1_ref, vpe_ref, vw2_ref, ok_ref, ov_ref):
    nb, n_grp, _ = xk_ref.shape
    rows = nb * n_grp
    for x_ref, w1_ref, pe_ref, w2_ref, o_ref in ((xk_ref, kw1_ref, kpe_ref, kw2_ref, ok_ref),
                                                 (xv_ref, vw1_ref, vpe_ref, vw2_ref, ov_ref)):
        w1a = w1_ref[0:GRP_W, :]
        w1b = w1_ref[GRP_W:2 * GRP_W, :]
        x = jnp.concatenate([x_ref[:, :, k * GRP_W:(k + 1) * GRP_W].reshape(rows, GRP_W)
                             for k in range(NSA_KV_HEADS)], axis=0)
        bias = (jnp.dot(pe_ref[:, 0:GRP_W], w1a, preferred_element_type=F32)[0:1]
                + jnp.dot(pe_ref[:, GRP_W:2 * GRP_W], w1b, preferred_element_type=F32)[0:1])
        ua = jnp.dot(x, w1a, preferred_element_type=F32)
        ub = jnp.dot(x, w1b, preferred_element_type=F32)
        hid = ua + pltpu.roll(ub, NSA_KV_HEADS * rows - 1, axis=0) + bias
        act = jax.nn.gelu(hid, approximate=True).astype(BF16)
        y = jnp.dot(act, w2_ref[...], preferred_element_type=F32).astype(o_ref.dtype)
        for k in range(NSA_KV_HEADS):
            o_ref[:, :, k * NSA_DH:(k + 1) * NSA_DH] = y[k * rows:(k + 1) * rows].reshape(nb, n_grp, NSA_DH)


def _compress(xk, xv, kw1, kpe, kw2, vw1, vpe, vw2, nb):
    B, n_grp, gw = xk.shape
    W = NSA_KV_WIDTH
    whole = lambda a: pl.BlockSpec(a.shape, lambda b: (0,) * a.ndim)
    xspec = pl.BlockSpec((nb, n_grp, gw), lambda b: (b, 0, 0))
    ospec = pl.BlockSpec((nb, n_grp, W), lambda b: (b, 0, 0))
    return pl.pallas_call(
        _compress_kernel,
        grid=(B // nb,),
        in_specs=[xspec, xspec, whole(kw1), whole(kpe), whole(kw2), whole(vw1), whole(vpe), whole(vw2)],
        out_specs=[ospec, ospec],
        out_shape=[jax.ShapeDtypeStruct((B, n_grp, W), BF16)] * 2,
        compiler_params=_params("parallel"),
        name="compress",
    )(xk, xv, kw1, kpe, kw2, vw1, vpe, vw2)


def _compress_pe(pe):
    return jnp.zeros((8, 2 * GRP_W), F32).at[0].set(pe.reshape(2 * GRP_W)).astype(BF16)


V_AUG = NSA_DH + 16


def _attend(k_rows, qt, vt_aug, masked):
    s = jnp.dot(k_rows(), qt, preferred_element_type=F32)
    if masked is not None:
        s = s + masked()
    yield
    m = jnp.max(s, axis=0, keepdims=True)
    p = jnp.exp2((s - m).astype(BF16))
    yield
    return m, jnp.dot(vt_aug(), p, preferred_element_type=F32)


def _pipelined(gens, depth=1, side=()):
    tasks = [_Staged(g) for g in gens]
    for n in range(len(tasks) + depth):
        for t in tasks[max(0, n - depth):n + 1]:
            t.step()
        for s in side:
            s.step()
    return [t.finish() for t in tasks]


def _merge(parts):
    m_all = parts[0][0]
    for m, _ in parts[1:]:
        m_all = jnp.maximum(m_all, m)
    tot = None
    for m, acc in parts:
        w = acc * jnp.exp2(m - m_all)
        tot = w if tot is None else tot + w
    return tot[0:NSA_DH] / tot[NSA_DH:NSA_DH + 1]


def _nsa_step(c, tq, tk, q_ref, kc_ref, vc_ref, kw_ref, misc_ref, gn_ref, o_ref, kaug_ref, vst_ref, vwt_ref):
    S = kw_ref.shape[0]
    n_cmp = kc_ref.shape[0]
    n_sel = S // SEL_BLOCK
    G = NSA_HEADS // NSA_KV_HEADS
    M = G * tq
    q0 = c * tq

    t_q = q0 + lax.broadcasted_iota(jnp.int32, (1, tq), 1)
    t_m = jnp.concatenate([t_q] * G, axis=1)
    key_off = lax.broadcasted_iota(jnp.int32, (tk, 1), 0)

    n_col = lax.broadcasted_iota(jnp.int32, (n_cmp, 1), 0)
    cmp_end = jnp.where(n_col < n_cmp - 1, n_col * CMP_STRIDE + (CMP_BLOCK - 1), jnp.int32(2 ** 30))
    cmp_valid = cmp_end <= t_m
    any_valid = (t_m >= CMP_BLOCK - 1).astype(F32)
    jj = lax.broadcasted_iota(jnp.int32, (n_sel, n_cmp), 0) * SEL_BLOCK
    nn = lax.broadcasted_iota(jnp.int32, (n_sel, n_cmp), 1) * CMP_STRIDE
    ov_t = jnp.maximum(jnp.minimum(nn + CMP_BLOCK, jj + SEL_BLOCK) - jnp.maximum(nn, jj), 0)
    ov_t = (ov_t.astype(F32) * (1.0 / CMP_BLOCK)).astype(BF16)
    j_row = lax.broadcasted_iota(jnp.int32, (n_sel, tq), 0)
    blk_t = lax.shift_right_logical(q0 + lax.broadcasted_iota(jnp.int32, (n_sel, tq), 1), 6)
    forced = (j_row == 0) | (j_row == blk_t) | (j_row == blk_t - 1)
    in_past = j_row <= blk_t

    heads = range(NSA_KV_HEADS)
    qts = [jnp.concatenate([_t_f32(q_ref[:, (k * G + g) * NSA_DH:(k * G + g + 1) * NSA_DH])
                            for g in range(G)], axis=1).astype(BF16) for k in heads]

    def select(k):
        hs = slice(k * NSA_DH, (k + 1) * NSA_DH)
        s = jnp.dot(kc_ref[:, hs], qts[k], preferred_element_type=F32)
        s = jnp.where(cmp_valid, s, NEG)
        yield
        e = jnp.exp2(s - jnp.max(s, axis=0, keepdims=True))
        p = e / jnp.sum(e, axis=0, keepdims=True) * any_valid
        yield
        o_cmp = jnp.dot(_t_f32(vc_ref[:, hs]).astype(BF16), p.astype(BF16), preferred_element_type=F32)
        p_grp = p[:, 0:tq]
        for g in range(1, G):
            p_grp = p_grp + p[:, g * tq:(g + 1) * tq]
        p_hi, p_lo = _split_bf16(p_grp)
        imp = (jnp.dot(ov_t, p_hi, preferred_element_type=F32)
               + jnp.dot(ov_t, p_lo, preferred_element_type=F32))
        imp = jnp.where(forced, BIG, jnp.where(in_past, imp, NEG))
        yield
        rank = []
        for r0 in range(0, n_sel, SUBLANES):
            blk = imp[r0:r0 + SUBLANES, :]
            cnt = jnp.zeros(blk.shape, F32)
            for j in range(n_sel):
                row = imp[j:j + 1, :]
                if j < r0:
                    cnt = cnt + jnp.where(row >= blk, 1.0, 0.0)
                elif j >= r0 + SUBLANES - 1:
                    cnt = cnt + jnp.where(row > blk, 1.0, 0.0)
                else:
                    below = r0 + lax.broadcasted_iota(jnp.int32, blk.shape, 0) > j
                    cnt = cnt + jnp.where(below, jnp.where(row >= blk, 1.0, 0.0), jnp.where(row > blk, 1.0, 0.0))
            rank.append(cnt)
            yield
        rank = jnp.concatenate(rank, axis=0)
        bias = jnp.where(rank < float(min(SEL_TOPK, n_sel)), 0.0, NEG).astype(BF16)
        return o_cmp, jnp.concatenate([qts[k], jnp.concatenate([bias] * G, axis=1),
                                       jnp.zeros((NSA_DH - n_sel, M), BF16)], axis=0)

    subs = range(tq // tk)
    col_off = jnp.concatenate([lax.broadcasted_iota(jnp.int32, (1, tk), 1)] * G, axis=1)
    after_bias = jnp.where(key_off > col_off, NEG, 0.0)
    far_bias = jnp.where(key_off <= col_off, NEG, 0.0)
    after = lambda: after_bias
    too_far = lambda: far_bias

    def cols(x, j):
        return jnp.concatenate([x[:, g * tq + j * tk:g * tq + (j + 1) * tk] for g in range(G)], axis=1)

    def sel_tasks(k, j, qta):
        first, diag = (0, q0 // tk - 1) if j is None else (q0 // tk, (q0 + j * tk) // tk)
        for kt in range(first, diag + 1):
            yield _attend(lambda kt=kt: kaug_ref[k, kt * tk:(kt + 1) * tk, :], qta,
                          lambda kt=kt: vst_ref[k, kt], after if kt == diag and j is not None else None)

    def win_tasks(k, j, qt):
        hs = slice(k * NSA_DH, (k + 1) * NSA_DH)
        diag = (q0 + j * tk) // tk
        back = WINDOW // tk
        for kt in range(max(diag - back, 0), diag + 1):
            masked = after if kt == diag else too_far if kt == diag - back else None
            yield _attend(lambda kt=kt: kw_ref[kt * tk:(kt + 1) * tk, hs], qt, lambda kt=kt: vwt_ref[k, kt], masked)

    def run(task_lists, side=()):
        flat = [(key, t) for key, tasks in task_lists.items() for t in tasks]
        parts = {key: [] for key in task_lists}
        for (key, _), part in zip(flat, _pipelined([t for _, t in flat], side=side)):
            parts[key].append(part)
        merged = {}
        for k in heads:
            for j in subs:
                shared = [(cols(m, j), cols(acc, j)) for m, acc in parts.get((k, None), [])]
                merged[k, j] = _merge(shared + parts[k, j])
        return [jnp.concatenate([merged[k, j][:, g * tk:(g + 1) * tk] for g in range(G) for j in subs], axis=1)
                for k in heads]

    selects = [_Staged(select(k)) for k in heads]
    o_win = run({(k, j): list(win_tasks(k, j, cols(qts[k], j))) for k in heads for j in subs}, side=selects)
    o_cmp, qtas = zip(*[s.finish() for s in selects])
    o_sel = run({(k, j): list(sel_tasks(k, j, qtas[k] if j is None else cols(qtas[k], j)))
                 for k in heads for j in (None, *subs)})


    gates_t = _t_f32(jax.nn.sigmoid(misc_ref[...]))
    heads = []
    for k in range(NSA_KV_HEADS):
        for g in range(G):
            gs = slice(g * tq, (g + 1) * tq)
            gl = GATE_OFF + (k * G + g) * NSA_BRANCHES
            heads.append(gates_t[gl:gl + 1, :] * o_cmp[k][:, gs] + gates_t[gl + 1:gl + 2, :] * o_sel[k][:, gs]
                         + gates_t[gl + 2:gl + 3, :] * o_win[k][:, gs])
    ssq = heads[0] * heads[0]
    for o in heads[1:]:
        ssq = ssq + o * o
    inv = lax.rsqrt(jnp.sum(ssq, axis=0, keepdims=True) * (1.0 / NSA_WIDTH) + EPS)
    for hq, o in enumerate(heads):
        cs = slice(hq * NSA_DH, (hq + 1) * NSA_DH)
        o_ref[:, cs] = ((o * inv).T * gn_ref[:, cs]).astype(o_ref.dtype)


def _nsa_kernel(q_ref, kc_ref, vc_ref, ks_ref, vs_ref, kw_ref, vw_ref, misc_ref, gn_ref, o_ref,
                kaug_ref, vst_ref, vwt_ref):
    tq = q_ref.shape[0]
    tk = vst_ref.shape[3]
    S = ks_ref.shape[0]
    i = pl.program_id(1)

    @pl.when(i == 0)
    def _():
        pos = lax.broadcasted_iota(jnp.int32, (S, LANES), 0)
        lane = lax.broadcasted_iota(jnp.int32, (S, LANES), 1)
        onehot = jnp.where(lax.shift_right_logical(pos, 6) == lane, 1.0, 0.0).astype(BF16)
        row = lax.broadcasted_iota(jnp.int32, (V_AUG - NSA_DH, tk), 0)
        ones_rows = jnp.where(row == 0, 1.0, 0.0).astype(BF16)
        for k in range(NSA_KV_HEADS):
            hs = slice(k * NSA_DH, (k + 1) * NSA_DH)
            kaug_ref[k, :, 0:NSA_DH] = ks_ref[:, hs]
            kaug_ref[k, :, NSA_DH:2 * NSA_DH] = onehot
            for kt in range(S // tk):
                rows = slice(kt * tk, (kt + 1) * tk)
                for src, dst in ((vs_ref, vst_ref), (vw_ref, vwt_ref)):
                    dst[k, kt, 0:NSA_DH, :] = _t_f32(src[rows, hs]).astype(BF16)
                    dst[k, kt, NSA_DH:V_AUG, :] = ones_rows

    for c in range(S // tq):
        pl.when(i == c)(functools.partial(_nsa_step, c, tq, tk, q_ref, kc_ref, vc_ref, kw_ref, misc_ref,
                                          gn_ref, o_ref, kaug_ref, vst_ref, vwt_ref))


def _nsa(nq, kcmp, vcmp, ks, vs, kw, vw, misc, gn, tq, tk):
    B, S, _ = nq.shape
    n_cmp = kcmp.shape[1]
    assert n_cmp <= LANES and S // SEL_BLOCK <= NSA_DH and SEL_BLOCK == 64
    assert tq % tk == 0 and tq <= WINDOW and WINDOW % tk == 0
    row = lambda w: pl.BlockSpec((None, tq, w), lambda b, i: (b, i, 0))
    seq = lambda n: pl.BlockSpec((None, n, NSA_KV_WIDTH), lambda b, i: (b, 0, 0))
    vt_scratch = pltpu.VMEM((NSA_KV_HEADS, S // tk, V_AUG, tk), BF16)
    return pl.pallas_call(
        _nsa_kernel,
        grid=(B, S // tq),
        in_specs=[row(NSA_WIDTH), seq(n_cmp), seq(n_cmp), seq(S), seq(S), seq(S), seq(S), row(MISC_W),
                  pl.BlockSpec(gn.shape, lambda b, i: (0, 0))],
        out_specs=row(NSA_WIDTH),
        out_shape=jax.ShapeDtypeStruct((B, S, NSA_WIDTH), BF16),
        scratch_shapes=[pltpu.VMEM((NSA_KV_HEADS, S, 2 * NSA_DH), BF16), vt_scratch, vt_scratch],
        compiler_params=_params("parallel", "arbitrary"),
        name="nsa",
    )(nq, kcmp, vcmp, ks, vs, kw, vw, misc, gn)


def _log_sigmoid(z):
    return jnp.minimum(z, 0.0) - jnp.log(1.0 + jnp.exp(-jnp.abs(z)))


class _Staged:
    def __init__(self, gen):
        self.gen, self.value, self.done = gen, None, False

    def step(self):
        if not self.done:
            try:
                next(self.gen)
            except StopIteration as stop:
                self.value, self.done = stop.value, True

    def finish(self):
        while not self.done:
            self.step()
        return self.value


def _gla_tile(q, k, v, g, misc, wa, ba, gn, st):
    C = GLA_CHUNK
    T = q.shape[0]
    n_chunk = T // C
    r = lax.broadcasted_iota(jnp.int32, (T, T), 0)
    c = lax.broadcasted_iota(jnp.int32, (T, T), 1)
    causal = (r >= c) & (r - c <= (r & (C - 1)))
    tri = jnp.where(causal, 1.0, 0.0).astype(BF16)

    z = jnp.dot(misc.astype(BF16), wa.astype(BF16), preferred_element_type=F32) + ba
    la = _log_sigmoid(z) * (1.0 / GLA_TAU)
    yield
    b = jnp.dot(tri, la.astype(BF16), preferred_element_type=F32)
    b_lasts = [b[(ci + 1) * C - 1:(ci + 1) * C, :] for ci in range(n_chunk)]
    b_last = jnp.concatenate([jnp.broadcast_to(bl, (C, bl.shape[1])) for bl in b_lasts], axis=0)
    qf = q.astype(F32)
    kf = k.astype(F32)
    q_in = (qf * GLA_DK ** -0.5 * jnp.exp(b)).astype(BF16)
    k_in = (kf * jnp.exp(-b)).astype(BF16)
    k_dec = (kf * jnp.exp(b_last - b)).astype(BF16)
    pad = jnp.zeros((LANES - n_chunk, b.shape[1]), F32)
    dec_cols = jnp.exp(jnp.concatenate(b_lasts + [pad], axis=0).T)
    yield

    o_intra = []
    for h in range(GLA_HEADS):
        ks = slice(h * GLA_DK, (h + 1) * GLA_DK)
        att = lax.dot_general(q_in[:, ks], k_in[:, ks], _NT, preferred_element_type=F32)
        att = jnp.where(causal, att, 0.0).astype(BF16)
        o_intra.append(jnp.dot(att, v[:, h * GLA_DV:(h + 1) * GLA_DV], preferred_element_type=F32))
        yield

    st = [st[h] for h in range(GLA_HEADS)]
    o_inter = []
    for ci in range(n_chunk):
        rows = slice(ci * C, (ci + 1) * C)
        out = []
        for h in range(GLA_HEADS):
            ks = slice(h * GLA_DK, (h + 1) * GLA_DK)
            vs = slice(h * GLA_DV, (h + 1) * GLA_DV)
            out.append(jnp.dot(q_in[rows, ks], st[h].astype(BF16), preferred_element_type=F32))
            d_st = lax.dot_general(k_dec[rows, ks], v[rows, vs], _TN, preferred_element_type=F32)
            st[h] = st[h] * dec_cols[ks, ci:ci + 1] + d_st
        o_inter.append(jnp.concatenate(out, axis=1))
        yield
    st = jnp.stack(st)
    o_inter = jnp.concatenate(o_inter, axis=0)
    gate = g.astype(F32)
    gate = gate * jax.nn.sigmoid(gate)
    outs = []
    for h in range(GLA_HEADS):
        vs = slice(h * GLA_DV, (h + 1) * GLA_DV)
        outs.append(_rms(o_intra[h] + o_inter[:, vs]) * gn[:, vs] * gate[:, vs])
    return jnp.concatenate(outs, axis=1), st


def _gla_ffn_kernel(x_ref, on_ref, mod_ref, wo_ref, g2_ref, w1_ref, w2_ref, gf_ref,
                    q0_ref, k0_ref, v0_ref, g0_ref, m0_ref, q1_ref, k1_ref, v1_ref, g1_ref, m1_ref,
                    wa_ref, ba_ref, gn_ref, o_ref, st_ref, og_ref, *, ff_chunk, final):
    def gla(q_ref, k_ref, v_ref, g_ref, m_ref, st):
        return _Staged(_gla_tile(q_ref[...], k_ref[...], v_ref[...], g_ref[...], m_ref[...],
                                 wa_ref[...], ba_ref[...], gn_ref[...], st))

    @pl.when((pl.program_id(0) == 0) & (pl.program_id(1) == 0))
    def _():
        o, st = gla(q0_ref, k0_ref, v0_ref, g0_ref, m0_ref, jnp.zeros(st_ref.shape, F32)).finish()
        og_ref[...] = o.astype(BF16)
        st_ref[...] = st

    last = pl.program_id(1) == pl.num_programs(1) - 1
    ahead = gla(q1_ref, k1_ref, v1_ref, g1_ref, m1_ref, jnp.where(last, 0.0, st_ref[...]))

    ahead.step()
    mix = (jnp.dot(og_ref[...], wo_ref[0:GLA_WIDTH, :], preferred_element_type=F32)
           + jnp.dot(on_ref[...], wo_ref[GLA_WIDTH:, :], preferred_element_type=F32))
    x1 = x_ref[...] + mod_ref[2:3, :] * mix
    h = (_rms(x1) * g2_ref[...] * (1.0 + mod_ref[4:5, :]) + mod_ref[3:4, :]).astype(BF16)
    acc = jnp.zeros_like(x1)
    ahead.step()
    for c in range(w1_ref.shape[1] // ff_chunk):
        cs = slice(c * ff_chunk, (c + 1) * ff_chunk)
        a = jnp.maximum(jnp.dot(h, w1_ref[:, cs], preferred_element_type=F32), 0.0)
        ahead.step()
        acc = acc + jnp.dot((a * a).astype(BF16), w2_ref[cs, :], preferred_element_type=F32)
        ahead.step()
    x2 = x1 + mod_ref[5:6, :] * acc
    o_ref[...] = _rms(x2) * gf_ref[...] if final else x2

    o, st = ahead.finish()
    og_ref[...] = o.astype(BF16)
    st_ref[...] = st


def _gla_ffn(x, on, mod, wo, g2, w1, w2, gf, gq, gk, gv, gg, misc, wa_p, ba, gn, final):
    B, S, D = x.shape
    tm = GLA_TILE
    n = S // tm
    row = lambda w: pl.BlockSpec((None, tm, w), lambda b, i: (b, i, 0))
    first = lambda w: pl.BlockSpec((None, tm, w), lambda b, i: (0, 0, 0))
    ahead = lambda w: pl.BlockSpec((None, tm, w), lambda b, i: (
        jnp.where(i + 1 < n, b, jnp.minimum(b + 1, B - 1)), jnp.where(i + 1 < n, i + 1, 0), 0))
    whole = lambda a: pl.BlockSpec(a.shape, lambda b, i: (0,) * a.ndim)
    gla_in = (gq, gk, gv, gg, misc)
    return pl.pallas_call(
        functools.partial(_gla_ffn_kernel, ff_chunk=1024, final=final),
        grid=(B, n),
        in_specs=([row(D), row(on.shape[2]), pl.BlockSpec((None, N_MOD, D), lambda b, i: (b, 0, 0)),
                   whole(wo), whole(g2), whole(w1), whole(w2), whole(gf)]
                  + [first(a.shape[2]) for a in gla_in] + [ahead(a.shape[2]) for a in gla_in]
                  + [whole(wa_p), whole(ba), whole(gn)]),
        out_specs=row(D),
        out_shape=jax.ShapeDtypeStruct((B, S, D), F32),
        scratch_shapes=[pltpu.VMEM((GLA_HEADS, GLA_DK, GLA_DV), F32), pltpu.VMEM((tm, GLA_WIDTH), BF16)],
        compiler_params=_params("arbitrary", "arbitrary"),
        name="gla_ffn",
    )(x, on, mod, wo, g2, w1, w2, gf, *gla_in, *gla_in, wa_p, ba, gn)


def kernel(x, c, positions, w_ada, b_ada, norm1_g, w_in, gla_w_a2, gla_b_a, gla_norm_g, nsa_pe_k, nsa_pe_v, cmp_k_w1, cmp_k_w2, cmp_v_w1, cmp_v_w2, nsa_norm_g, w_o, norm2_g, w_ff1, w_ff2, final_norm_g):
    B, S, D = x.shape
    depth = w_in.shape[0]
    half = NSA_DH // 2
    inv = ROPE_THETA ** (-jnp.arange(half, dtype=F32) / half)
    inv = jnp.concatenate([inv, inv]).reshape(1, NSA_DH)
    for l in range(depth):
        mod = _adaln(c, w_ada[l], b_ada[l]).reshape(B, N_MOD, D)
        (gq, gk, gv, gg, nq, kc, vc, ks, vs, kw, vw, misc) = _in_proj(
            x, mod, norm1_g[l].reshape(1, D), positions, inv, w_in, l, tm=512)

        k_cmp, v_cmp = _compress(kc, vc, cmp_k_w1[l].astype(BF16), _compress_pe(nsa_pe_k[l]),
                                 cmp_k_w2[l].astype(BF16), cmp_v_w1[l].astype(BF16),
                                 _compress_pe(nsa_pe_v[l]), cmp_v_w2[l].astype(BF16), nb=int(np.gcd(B, 4)))
        o_nsa = _nsa(nq, k_cmp, v_cmp, ks, vs, kw, vw, misc, nsa_norm_g[l].reshape(1, NSA_WIDTH),
                     tq=512, tk=256)

        wa_p = jnp.zeros((MISC_W, GLA_QK), F32).at[0:GLA_RANK].set(gla_w_a2[l])
        x = _gla_ffn(x, o_nsa, mod, w_o[l].astype(BF16), norm2_g[l].reshape(1, D),
                     w_ff1[l].astype(BF16), w_ff2[l].astype(BF16), final_norm_g.reshape(1, D),
                     gq, gk, gv, gg, misc, wa_p, gla_b_a[l].reshape(1, GLA_QK),
                     gla_norm_g[l].reshape(1, GLA_WIDTH), final=(l == depth - 1))
    return x
```

```python
import functools

import numpy as np
import jax
import jax.numpy as jnp
from jax import lax
from jax.experimental import pallas as pl
from jax.experimental.pallas import tpu as pltpu

GLA_HEADS = 4
GLA_DK = 64
GLA_DV = 128
GLA_RANK = 16
GLA_TAU = 16.0
GLA_CHUNK = 64
NSA_HEADS = 4
NSA_KV_HEADS = 2
NSA_DH = 128
NSA_BRANCHES = 3
CMP_BLOCK = 32
CMP_STRIDE = 16
CMP_HIDDEN = 256
SEL_BLOCK = 64
SEL_TOPK = 16
WINDOW = 512
N_MOD = 6
ROPE_THETA = 10000.0
EPS = 1e-6
NEG = -1e30
BIG = 1e30

GLA_QK = GLA_HEADS * GLA_DK
GLA_WIDTH = GLA_HEADS * GLA_DV
NSA_WIDTH = NSA_HEADS * NSA_DH
NSA_KV_WIDTH = NSA_KV_HEADS * NSA_DH
N_GATE = NSA_HEADS * NSA_BRANCHES
IN_SPLITS = (GLA_QK, GLA_QK, GLA_WIDTH, GLA_WIDTH, GLA_RANK, NSA_WIDTH) + (NSA_KV_WIDTH,) * 6 + (N_GATE,)

LANES = 128
SUBLANES = 8
MISC_W = LANES
GATE_OFF = GLA_RANK
GLA_TILE = 256
VMEM_LIMIT = 56 * 1024 * 1024

F32 = jnp.float32
BF16 = jnp.bfloat16

_NT = (((1,), (1,)), ((), ()))
_TN = (((0,), (0,)), ((), ()))


def _params(*sem):
    return pltpu.CompilerParams(dimension_semantics=sem, vmem_limit_bytes=VMEM_LIMIT)


def _rms(x):
    return x * lax.rsqrt(jnp.mean(x * x, axis=-1, keepdims=True) + EPS)


def _split_bf16(x):
    hi = x.astype(BF16)
    return hi, (x - hi.astype(F32)).astype(BF16)


def _t_f32(x):
    return x.astype(F32).T


def _adaln_kernel(c_ref, w_ref, b_ref, o_ref):
    c = c_ref[...]
    n = c.shape[0]
    a_hi, a_lo = _split_bf16(c * jax.nn.sigmoid(c))
    w_hi, w_lo = _split_bf16(w_ref[...])
    y = jnp.dot(jnp.concatenate([a_hi, a_lo], axis=0), w_hi, preferred_element_type=F32)
    o_ref[...] = y[0:n] + y[n:2 * n] + jnp.dot(a_hi, w_lo, preferred_element_type=F32) + b_ref[...]


def _adaln(c, w_ada, b_ada):
    B, D = c.shape
    N = w_ada.shape[1]
    tn = D
    return pl.pallas_call(
        _adaln_kernel,
        grid=(N // tn,),
        in_specs=[pl.BlockSpec((B, D), lambda j: (0, 0)),
                  pl.BlockSpec((D, tn), lambda j: (0, j)),
                  pl.BlockSpec((1, tn), lambda j: (0, j))],
        out_specs=pl.BlockSpec((B, tn), lambda j: (0, j)),
        out_shape=jax.ShapeDtypeStruct((B, N), F32),
        compiler_params=_params("arbitrary"),
        name="adaln",
    )(c, w_ada, b_ada.reshape(1, N))


_PROJ_GROUPS = (("gq", GLA_QK, False), ("gk", GLA_QK, False), ("gv", GLA_WIDTH, False),
                ("gg", GLA_WIDTH, False), ("nq", NSA_WIDTH, True), ("kc", NSA_KV_WIDTH, True),
                ("vc", NSA_KV_WIDTH, False), ("ks", NSA_KV_WIDTH, True), ("vs", NSA_KV_WIDTH, False),
                ("kw", NSA_KV_WIDTH, True), ("vw", NSA_KV_WIDTH, False))
_PROJ_W = sum(g[1] for g in _PROJ_GROUPS) + MISC_W


def _w_in_moves():
    names = ("gq", "gk", "gv", "gg", "gr", "nq", "kc", "vc", "ks", "vs", "kw", "vw", "ngate")
    src = dict(zip(names, np.cumsum((0,) + IN_SPLITS[:-1])))
    width = dict(zip(names, IN_SPLITS))
    moves, dst = [], 0
    for name, w, _ in _PROJ_GROUPS:
        moves.append((int(src[name]), dst, w))
        dst += w
    for name in ("gr", "ngate"):
        moves.append((int(src[name]), dst, width[name]))
        dst += width[name]
    return tuple(moves)


_W_IN_MOVES = _w_in_moves()
_GROUPED = ("kc", "vc")
GRP_W = CMP_STRIDE * NSA_DH


def _in_proj_kernel(x_ref, mod_ref, g_ref, pos_ref, inv_ref, win_ref, *refs):
    n_out = len(_PROJ_GROUPS) + 1
    out_refs, stage_refs, w_ref = refs[:n_out], refs[n_out:-1], refs[-1]

    @pl.when((pl.program_id(0) == 0) & (pl.program_id(1) == 0))
    def _():
        for src, dst, width in _W_IN_MOVES:
            w_ref[:, dst:dst + width] = win_ref[:, src:src + width].astype(BF16)
        pad = _PROJ_W - _W_IN_MOVES[-1][1] - _W_IN_MOVES[-1][2]
        w_ref[:, _PROJ_W - pad:] = jnp.zeros((w_ref.shape[0], pad), BF16)

    tm = x_ref.shape[0]
    x = x_ref[...]
    h = _rms(x) * g_ref[...] * (1.0 + mod_ref[1:2, :]) + mod_ref[0:1, :]
    hb = h.astype(BF16)

    qs = NSA_DH ** -0.5 * float(np.log2(np.e))
    cos = sin = None

    off = 0
    for (name, width, rot), o_ref in zip(_PROJ_GROUPS, out_refs[:-1]):
        if rot and cos is None:
            pos = pos_ref[...].astype(F32)
            pos = jnp.concatenate([jnp.broadcast_to(pos[r:r + 1, :], (LANES, LANES)).T
                                   for r in range(pos.shape[0])], axis=0)
            lane = lax.broadcasted_iota(jnp.int32, (1, LANES), 1)
            low = lane < NSA_DH // 2
            ang = jnp.where(low, pos[0:tm // 2], pos[tm // 2:tm]) * inv_ref[...]
            c2, s2 = jnp.cos(ang), jnp.sin(ang)
            c2r, s2r = pltpu.roll(c2, NSA_DH // 2, axis=1), pltpu.roll(s2, NSA_DH // 2, axis=1)
            cos = jnp.concatenate([jnp.where(low, c2, c2r), jnp.where(low, c2r, c2)], axis=0)
            sin = jnp.concatenate([jnp.where(low, s2, s2r), jnp.where(low, s2r, s2)], axis=0)
            sin = sin * jnp.where(low, -1.0, 1.0)
        y = jnp.dot(hb, w_ref[:, off:off + width], preferred_element_type=F32)
        grouped = name in _GROUPED
        dst = stage_refs[_GROUPED.index(name)] if grouped else o_ref
        for hd in range(width // NSA_DH if (rot or grouped) else 0):
            yh = y[:, hd * NSA_DH:(hd + 1) * NSA_DH]
            if rot:
                c, s = (cos * qs, sin * qs) if name == "nq" else (cos, sin)
                yh = yh * c + pltpu.roll(yh, NSA_DH // 2, axis=1) * s
            if grouped:
                dst[hd] = yh
            else:
                dst[:, hd * NSA_DH:(hd + 1) * NSA_DH] = yh.astype(dst.dtype)
        if not (rot or grouped):
            dst[...] = y.astype(dst.dtype)
        if grouped:
            for k in range(NSA_KV_HEADS):
                for tok in range(CMP_STRIDE):
                    c0 = k * GRP_W + tok * NSA_DH
                    o_ref[:, c0:c0 + NSA_DH] = dst[k, pl.ds(tok, tm // CMP_STRIDE, stride=CMP_STRIDE), :].astype(
                        o_ref.dtype)
        off += width
    out_refs[-1][...] = jnp.dot(hb, w_ref[:, off:off + MISC_W], preferred_element_type=F32)


def _in_proj(x, mod, norm_g, positions, inv, w_in, layer, tm):
    B, S, D = x.shape
    pos4 = positions.reshape(B, S // tm, tm // LANES, LANES)
    row = lambda w: pl.BlockSpec((None, tm, w), lambda b, i: (b, i, 0))
    whole = lambda a: pl.BlockSpec(a.shape, lambda b, i: (0,) * a.ndim)
    out_shape, out_specs = [], []
    for name, w, _ in _PROJ_GROUPS:
        if name in _GROUPED:
            out_shape.append(jax.ShapeDtypeStruct((B, S // CMP_STRIDE, CMP_STRIDE * w), BF16))
            out_specs.append(pl.BlockSpec((None, tm // CMP_STRIDE, CMP_STRIDE * w), lambda b, i: (b, i, 0)))
        else:
            out_shape.append(jax.ShapeDtypeStruct((B, S, w), BF16))
            out_specs.append(row(w))
    out_shape.append(jax.ShapeDtypeStruct((B, S, MISC_W), F32))
    out_specs.append(row(MISC_W))
    return pl.pallas_call(
        _in_proj_kernel,
        grid=(B, S // tm),
        in_specs=[row(D), pl.BlockSpec((None, N_MOD, D), lambda b, i: (b, 0, 0)), whole(norm_g),
                  pl.BlockSpec((None, None, tm // LANES, LANES), lambda b, i: (b, i, 0, 0)), whole(inv),
                  pl.BlockSpec((None,) + w_in.shape[1:], lambda b, i: (layer, 0, 0),
                               pipeline_mode=pl.Buffered(1))],
        out_specs=out_specs,
        out_shape=out_shape,
        scratch_shapes=([pltpu.VMEM((NSA_KV_HEADS, tm, NSA_DH), F32) for _ in _GROUPED]
                        + [pltpu.VMEM((D, _PROJ_W), BF16)]),
        compiler_params=_params("arbitrary", "arbitrary"),
        name="in_proj",
    )(x, mod, norm_g, pos4, inv, w_in)


def _compress_kernel(xk_ref, xv_ref, kw1_ref, kpe_ref, kw2_ref, vw1_ref, vpe_ref, vw2_ref, ok_ref, ov_ref):
    nb, n_grp, _ = xk_ref.shape
    rows = nb * n_grp
    for x_ref, w1_ref, pe_ref, w2_ref, o_ref in ((xk_ref, kw1_ref, kpe_ref, kw2_ref, ok_ref),
                                                 (xv_ref, vw1_ref, vpe_ref, vw2_ref, ov_ref)):
        w1a = w1_ref[0:GRP_W, :]
        w1b = w1_ref[GRP_W:2 * GRP_W, :]
        x = jnp.concatenate([x_ref[:, :, k * GRP_W:(k + 1) * GRP_W].reshape(rows, GRP_W)
                             for k in range(NSA_KV_HEADS)], axis=0)
        bias = (jnp.dot(pe_ref[:, 0:GRP_W], w1a, preferred_element_type=F32)[0:1]
                + jnp.dot(pe_ref[:, GRP_W:2 * GRP_W], w1b, preferred_element_type=F32)[0:1])
        ua = jnp.dot(x, w1a, preferred_element_type=F32)
        ub = jnp.dot(x, w1b, preferred_element_type=F32)
        hid = ua + pltpu.roll(ub, NSA_KV_HEADS * rows - 1, axis=0) + bias
        act = jax.nn.gelu(hid, approximate=True).astype(BF16)
        y = jnp.dot(act, w2_ref[...], preferred_element_type=F32).astype(o_ref.dtype)
        for k in range(NSA_KV_HEADS):
            o_ref[:, :, k * NSA_DH:(k + 1) * NSA_DH] = y[k * rows:(k + 1) * rows].reshape(nb, n_grp, NSA_DH)


def _compress(xk, xv, kw1, kpe, kw2, vw1, vpe, vw2, nb):
    B, n_grp, gw = xk.shape
    W = NSA_KV_WIDTH
    whole = lambda a: pl.BlockSpec(a.shape, lambda b: (0,) * a.ndim)
    xspec = pl.BlockSpec((nb, n_grp, gw), lambda b: (b, 0, 0))
    ospec = pl.BlockSpec((nb, n_grp, W), lambda b: (b, 0, 0))
    return pl.pallas_call(
        _compress_kernel,
        grid=(B // nb,),
        in_specs=[xspec, xspec, whole(kw1), whole(kpe), whole(kw2), whole(vw1), whole(vpe), whole(vw2)],
        out_specs=[ospec, ospec],
        out_shape=[jax.ShapeDtypeStruct((B, n_grp, W), BF16)] * 2,
        compiler_params=_params("parallel"),
        name="compress",
    )(xk, xv, kw1, kpe, kw2, vw1, vpe, vw2)


def _compress_pe(pe):
    return jnp.zeros((8, 2 * GRP_W), F32).at[0].set(pe.reshape(2 * GRP_W)).astype(BF16)


V_AUG = NSA_DH + 16


def _attend(k_rows, qt, vt_aug, masked):
    s = jnp.dot(k_rows(), qt, preferred_element_type=F32)
    if masked is not None:
        s = s + masked()
    yield
    m = jnp.max(s, axis=0, keepdims=True)
    p = jnp.exp2((s - m).astype(BF16))
    yield
    return m, jnp.dot(vt_aug(), p, preferred_element_type=F32)


def _pipelined(gens, depth=1, side=()):
    tasks = [_Staged(g) for g in gens]
    for n in range(len(tasks) + depth):
        for t in tasks[max(0, n - depth):n + 1]:
            t.step()
        for s in side:
            s.step()
    return [t.finish() for t in tasks]


def _merge(parts):
    m_all = parts[0][0]
    for m, _ in parts[1:]:
        m_all = jnp.maximum(m_all, m)
    tot = None
    for m, acc in parts:
        w = acc * jnp.exp2(m - m_all)
        tot = w if tot is None else tot + w
    return tot[0:NSA_DH] / tot[NSA_DH:NSA_DH + 1]


def _nsa_step(c, tq, tk, q_ref, kc_ref, vc_ref, kw_ref, misc_ref, gn_ref, o_ref, kaug_ref, vst_ref, vwt_ref):
    S = kw_ref.shape[0]
    n_cmp = kc_ref.shape[0]
    n_sel = S // SEL_BLOCK
    G = NSA_HEADS // NSA_KV_HEADS
    M = G * tq
    q0 = c * tq

    t_q = q0 + lax.broadcasted_iota(jnp.int32, (1, tq), 1)
    t_m = jnp.concatenate([t_q] * G, axis=1)
    key_off = lax.broadcasted_iota(jnp.int32, (tk, 1), 0)

    n_col = lax.broadcasted_iota(jnp.int32, (n_cmp, 1), 0)
    cmp_end = jnp.where(n_col < n_cmp - 1, n_col * CMP_STRIDE + (CMP_BLOCK - 1), jnp.int32(2 ** 30))
    cmp_valid = cmp_end <= t_m
    any_valid = (t_m >= CMP_BLOCK - 1).astype(F32)
    jj = lax.broadcasted_iota(jnp.int32, (n_sel, n_cmp), 0) * SEL_BLOCK
    nn = lax.broadcasted_iota(jnp.int32, (n_sel, n_cmp), 1) * CMP_STRIDE
    ov_t = jnp.maximum(jnp.minimum(nn + CMP_BLOCK, jj + SEL_BLOCK) - jnp.maximum(nn, jj), 0)
    ov_t = (ov_t.astype(F32) * (1.0 / CMP_BLOCK)).astype(BF16)
    j_row = lax.broadcasted_iota(jnp.int32, (n_sel, tq), 0)
    blk_t = lax.shift_right_logical(q0 + lax.broadcasted_iota(jnp.int32, (n_sel, tq), 1), 6)
    forced = (j_row == 0) | (j_row == blk_t) | (j_row == blk_t - 1)
    in_past = j_row <= blk_t

    heads = range(NSA_KV_HEADS)
    qts = [jnp.concatenate([_t_f32(q_ref[:, (k * G + g) * NSA_DH:(k * G + g + 1) * NSA_DH])
                            for g in range(G)], axis=1).astype(BF16) for k in heads]

    def select(k):
        hs = slice(k * NSA_DH, (k + 1) * NSA_DH)
        s = jnp.dot(kc_ref[:, hs], qts[k], preferred_element_type=F32)
        s = jnp.where(cmp_valid, s, NEG)
        yield
        e = jnp.exp2(s - jnp.max(s, axis=0, keepdims=True))
        p = e / jnp.sum(e, axis=0, keepdims=True) * any_valid
        yield
        o_cmp = jnp.dot(_t_f32(vc_ref[:, hs]).astype(BF16), p.astype(BF16), preferred_element_type=F32)
        p_grp = p[:, 0:tq]
        for g in range(1, G):
            p_grp = p_grp + p[:, g * tq:(g + 1) * tq]
        p_hi, p_lo = _split_bf16(p_grp)
        imp = (jnp.dot(ov_t, p_hi, preferred_element_type=F32)
               + jnp.dot(ov_t, p_lo, preferred_element_type=F32))
        imp = jnp.where(forced, BIG, jnp.where(in_past, imp, NEG))
        yield
        rank = []
        for r0 in range(0, n_sel, SUBLANES):
            blk = imp[r0:r0 + SUBLANES, :]
            cnt = jnp.zeros(blk.shape, F32)
            for j in range(n_sel):
                row = imp[j:j + 1, :]
                if j < r0:
                    cnt = cnt + jnp.where(row >= blk, 1.0, 0.0)
                elif j >= r0 + SUBLANES - 1:
                    cnt = cnt + jnp.where(row > blk, 1.0, 0.0)
                else:
                    below = r0 + lax.broadcasted_iota(jnp.int32, blk.shape, 0) > j
                    cnt = cnt + jnp.where(below, jnp.where(row >= blk, 1.0, 0.0), jnp.where(row > blk, 1.0, 0.0))
            rank.append(cnt)
            yield
        rank = jnp.concatenate(rank, axis=0)
        bias = jnp.where(rank < float(min(SEL_TOPK, n_sel)), 0.0, NEG).astype(BF16)
        return o_cmp, jnp.concatenate([qts[k], jnp.concatenate([bias] * G, axis=1),
                                       jnp.zeros((NSA_DH - n_sel, M), BF16)], axis=0)

    subs = range(tq // tk)
    col_off = jnp.concatenate([lax.broadcasted_iota(jnp.int32, (1, tk), 1)] * G, axis=1)
    after_bias = jnp.where(key_off > col_off, NEG, 0.0)
    far_bias = jnp.where(key_off <= col_off, NEG, 0.0)
    after = lambda: after_bias
    too_far = lambda: far_bias

    def cols(x, j):
        return jnp.concatenate([x[:, g * tq + j * tk:g * tq + (j + 1) * tk] for g in range(G)], axis=1)

    def sel_tasks(k, j, qta):
        first, diag = (0, q0 // tk - 1) if j is None else (q0 // tk, (q0 + j * tk) // tk)
        for kt in range(first, diag + 1):
            yield _attend(lambda kt=kt: kaug_ref[k, kt * tk:(kt + 1) * tk, :], qta,
                          lambda kt=kt: vst_ref[k, kt], after if kt == diag and j is not None else None)

    def win_tasks(k, j, qt):
        hs = slice(k * NSA_DH, (k + 1) * NSA_DH)
        diag = (q0 + j * tk) // tk
        back = WINDOW // tk
        for kt in range(max(diag - back, 0), diag + 1):
            masked = after if kt == diag else too_far if kt == diag - back else None
            yield _attend(lambda kt=kt: kw_ref[kt * tk:(kt + 1) * tk, hs], qt, lambda kt=kt: vwt_ref[k, kt], masked)

    def run(task_lists, side=()):
        flat = [(key, t) for key, tasks in task_lists.items() for t in tasks]
        parts = {key: [] for key in task_lists}
        for (key, _), part in zip(flat, _pipelined([t for _, t in flat], side=side)):
            parts[key].append(part)
        merged = {}
        for k in heads:
            for j in subs:
                shared = [(cols(m, j), cols(acc, j)) for m, acc in parts.get((k, None), [])]
                merged[k, j] = _merge(shared + parts[k, j])
        return [jnp.concatenate([merged[k, j][:, g * tk:(g + 1) * tk] for g in range(G) for j in subs], axis=1)
                for k in heads]

    selects = [_Staged(select(k)) for k in heads]
    o_win = run({(k, j): list(win_tasks(k, j, cols(qts[k], j))) for k in heads for j in subs}, side=selects)
    o_cmp, qtas = zip(*[s.finish() for s in selects])
    o_sel = run({(k, j): list(sel_tasks(k, j, qtas[k] if j is None else cols(qtas[k], j)))
                 for k in heads for j in (None, *subs)})


    gates_t = _t_f32(jax.nn.sigmoid(misc_ref[...]))
    heads = []
    for k in range(NSA_KV_HEADS):
        for g in range(G):
            gs = slice(g * tq, (g + 1) * tq)
            gl = GATE_OFF + (k * G + g) * NSA_BRANCHES
            heads.append(gates_t[gl:gl + 1, :] * o_cmp[k][:, gs] + gates_t[gl + 1:gl + 2, :] * o_sel[k][:, gs]
                         + gates_t[gl + 2:gl + 3, :] * o_win[k][:, gs])
    ssq = heads[0] * heads[0]
    for o in heads[1:]:
        ssq = ssq + o * o
    inv = lax.rsqrt(jnp.sum(ssq, axis=0, keepdims=True) * (1.0 / NSA_WIDTH) + EPS)
    for hq, o in enumerate(heads):
        cs = slice(hq * NSA_DH, (hq + 1) * NSA_DH)
        o_ref[:, cs] = ((o * inv).T * gn_ref[:, cs]).astype(o_ref.dtype)


def _nsa_kernel(q_ref, kc_ref, vc_ref, ks_ref, vs_ref, kw_ref, vw_ref, misc_ref, gn_ref, o_ref,
                kaug_ref, vst_ref, vwt_ref):
    tq = q_ref.shape[0]
    tk = vst_ref.shape[3]
    S = ks_ref.shape[0]
    i = pl.program_id(1)

    @pl.when(i == 0)
    def _():
        pos = lax.broadcasted_iota(jnp.int32, (S, LANES), 0)
        lane = lax.broadcasted_iota(jnp.int32, (S, LANES), 1)
        onehot = jnp.where(lax.shift_right_logical(pos, 6) == lane, 1.0, 0.0).astype(BF16)
        row = lax.broadcasted_iota(jnp.int32, (V_AUG - NSA_DH, tk), 0)
        ones_rows = jnp.where(row == 0, 1.0, 0.0).astype(BF16)
        for k in range(NSA_KV_HEADS):
            hs = slice(k * NSA_DH, (k + 1) * NSA_DH)
            kaug_ref[k, :, 0:NSA_DH] = ks_ref[:, hs]
            kaug_ref[k, :, NSA_DH:2 * NSA_DH] = onehot
            for kt in range(S // tk):
                rows = slice(kt * tk, (kt + 1) * tk)
                for src, dst in ((vs_ref, vst_ref), (vw_ref, vwt_ref)):
                    dst[k, kt, 0:NSA_DH, :] = _t_f32(src[rows, hs]).astype(BF16)
                    dst[k, kt, NSA_DH:V_AUG, :] = ones_rows

    for c in range(S // tq):
        pl.when(i == c)(functools.partial(_nsa_step, c, tq, tk, q_ref, kc_ref, vc_ref, kw_ref, misc_ref,
                                          gn_ref, o_ref, kaug_ref, vst_ref, vwt_ref))


def _nsa(nq, kcmp, vcmp, ks, vs, kw, vw, misc, gn, tq, tk):
    B, S, _ = nq.shape
    n_cmp = kcmp.shape[1]
    assert n_cmp <= LANES and S // SEL_BLOCK <= NSA_DH and SEL_BLOCK == 64
    assert tq % tk == 0 and tq <= WINDOW and WINDOW % tk == 0
    row = lambda w: pl.BlockSpec((None, tq, w), lambda b, i: (b, i, 0))
    seq = lambda n: pl.BlockSpec((None, n, NSA_KV_WIDTH), lambda b, i: (b, 0, 0))
    vt_scratch = pltpu.VMEM((NSA_KV_HEADS, S // tk, V_AUG, tk), BF16)
    return pl.pallas_call(
        _nsa_kernel,
        grid=(B, S // tq),
        in_specs=[row(NSA_WIDTH), seq(n_cmp), seq(n_cmp), seq(S), seq(S), seq(S), seq(S), row(MISC_W),
                  pl.BlockSpec(gn.shape, lambda b, i: (0, 0))],
        out_specs=row(NSA_WIDTH),
        out_shape=jax.ShapeDtypeStruct((B, S, NSA_WIDTH), BF16),
        scratch_shapes=[pltpu.VMEM((NSA_KV_HEADS, S, 2 * NSA_DH), BF16), vt_scratch, vt_scratch],
        compiler_params=_params("parallel", "arbitrary"),
        name="nsa",
    )(nq, kcmp, vcmp, ks, vs, kw, vw, misc, gn)


def _log_sigmoid(z):
    return jnp.minimum(z, 0.0) - jnp.log(1.0 + jnp.exp(-jnp.abs(z)))


class _Staged:
    def __init__(self, gen):
        self.gen, self.value, self.done = gen, None, False

    def step(self):
        if not self.done:
            try:
                next(self.gen)
            except StopIteration as stop:
                self.value, self.done = stop.value, True

    def finish(self):
        while not self.done:
            self.step()
        return self.value


def _gla_tile(q, k, v, g, misc, wa, ba, gn, st):
    C = GLA_CHUNK
    T = q.shape[0]
    n_chunk = T // C
    r = lax.broadcasted_iota(jnp.int32, (T, T), 0)
    c = lax.broadcasted_iota(jnp.int32, (T, T), 1)
    causal = (r >= c) & (r - c <= (r & (C - 1)))
    tri = jnp.where(causal, 1.0, 0.0).astype(BF16)

    z = jnp.dot(misc.astype(BF16), wa.astype(BF16), preferred_element_type=F32) + ba
    la = _log_sigmoid(z) * (1.0 / GLA_TAU)
    yield
    b = jnp.dot(tri, la.astype(BF16), preferred_element_type=F32)
    b_lasts = [b[(ci + 1) * C - 1:(ci + 1) * C, :] for ci in range(n_chunk)]
    b_last = jnp.concatenate([jnp.broadcast_to(bl, (C, bl.shape[1])) for bl in b_lasts], axis=0)
    qf = q.astype(F32)
    kf = k.astype(F32)
    q_in = (qf * GLA_DK ** -0.5 * jnp.exp(b)).astype(BF16)
    k_in = (kf * jnp.exp(-b)).astype(BF16)
    k_dec = (kf * jnp.exp(b_last - b)).astype(BF16)
    pad = jnp.zeros((LANES - n_chunk, b.shape[1]), F32)
    dec_cols = jnp.exp(jnp.concatenate(b_lasts + [pad], axis=0).T)
    yield

    o_intra = []
    for h in range(GLA_HEADS):
        ks = slice(h * GLA_DK, (h + 1) * GLA_DK)
        att = lax.dot_general(q_in[:, ks], k_in[:, ks], _NT, preferred_element_type=F32)
        att = jnp.where(causal, att, 0.0).astype(BF16)
        o_intra.append(jnp.dot(att, v[:, h * GLA_DV:(h + 1) * GLA_DV], preferred_element_type=F32))
        yield

    st = [st[h] for h in range(GLA_HEADS)]
    o_inter = []
    for ci in range(n_chunk):
        rows = slice(ci * C, (ci + 1) * C)
        out = []
        for h in range(GLA_HEADS):
            ks = slice(h * GLA_DK, (h + 1) * GLA_DK)
            vs = slice(h * GLA_DV, (h + 1) * GLA_DV)
            out.append(jnp.dot(q_in[rows, ks], st[h].astype(BF16), preferred_element_type=F32))
            d_st = lax.dot_general(k_dec[rows, ks], v[rows, vs], _TN, preferred_element_type=F32)
            st[h] = st[h] * dec_cols[ks, ci:ci + 1] + d_st
        o_inter.append(jnp.concatenate(out, axis=1))
        yield
    st = jnp.stack(st)
    o_inter = jnp.concatenate(o_inter, axis=0)
    gate = g.astype(F32)
    gate = gate * jax.nn.sigmoid(gate)
    outs = []
    for h in range(GLA_HEADS):
        vs = slice(h * GLA_DV, (h + 1) * GLA_DV)
        outs.append(_rms(o_intra[h] + o_inter[:, vs]) * gn[:, vs] * gate[:, vs])
    return jnp.concatenate(outs, axis=1), st


def _gla_ffn_kernel(x_ref, on_ref, mod_ref, wo_ref, g2_ref, w1_ref, w2_ref, gf_ref,
                    q0_ref, k0_ref, v0_ref, g0_ref, m0_ref, q1_ref, k1_ref, v1_ref, g1_ref, m1_ref,
                    wa_ref, ba_ref, gn_ref, o_ref, st_ref, og_ref, *, ff_chunk, final):
    def gla(q_ref, k_ref, v_ref, g_ref, m_ref, st):
        return _Staged(_gla_tile(q_ref[...], k_ref[...], v_ref[...], g_ref[...], m_ref[...],
                                 wa_ref[...], ba_ref[...], gn_ref[...], st))

    @pl.when((pl.program_id(0) == 0) & (pl.program_id(1) == 0))
    def _():
        o, st = gla(q0_ref, k0_ref, v0_ref, g0_ref, m0_ref, jnp.zeros(st_ref.shape, F32)).finish()
        og_ref[...] = o.astype(BF16)
        st_ref[...] = st

    last = pl.program_id(1) == pl.num_programs(1) - 1
    ahead = gla(q1_ref, k1_ref, v1_ref, g1_ref, m1_ref, jnp.where(last, 0.0, st_ref[...]))

    ahead.step()
    mix = (jnp.dot(og_ref[...], wo_ref[0:GLA_WIDTH, :], preferred_element_type=F32)
           + jnp.dot(on_ref[...], wo_ref[GLA_WIDTH:, :], preferred_element_type=F32))
    x1 = x_ref[...] + mod_ref[2:3, :] * mix
    h = (_rms(x1) * g2_ref[...] * (1.0 + mod_ref[4:5, :]) + mod_ref[3:4, :]).astype(BF16)
    acc = jnp.zeros_like(x1)
    ahead.step()
    for c in range(w1_ref.shape[1] // ff_chunk):
        cs = slice(c * ff_chunk, (c + 1) * ff_chunk)
        a = jnp.maximum(jnp.dot(h, w1_ref[:, cs], preferred_element_type=F32), 0.0)
        ahead.step()
        acc = acc + jnp.dot((a * a).astype(BF16), w2_ref[cs, :], preferred_element_type=F32)
        ahead.step()
    x2 = x1 + mod_ref[5:6, :] * acc
    o_ref[...] = _rms(x2) * gf_ref[...] if final else x2

    o, st = ahead.finish()
    og_ref[...] = o.astype(BF16)
    st_ref[...] = st


def _gla_ffn(x, on, mod, wo, g2, w1, w2, gf, gq, gk, gv, gg, misc, wa_p, ba, gn, final):
    B, S, D = x.shape
    tm = GLA_TILE
    n = S // tm
    row = lambda w: pl.BlockSpec((None, tm, w), lambda b, i: (b, i, 0))
    first = lambda w: pl.BlockSpec((None, tm, w), lambda b, i: (0, 0, 0))
    ahead = lambda w: pl.BlockSpec((None, tm, w), lambda b, i: (
        jnp.where(i + 1 < n, b, jnp.minimum(b + 1, B - 1)), jnp.where(i + 1 < n, i + 1, 0), 0))
    whole = lambda a: pl.BlockSpec(a.shape, lambda b, i: (0,) * a.ndim)
    gla_in = (gq, gk, gv, gg, misc)
    return pl.pallas_call(
        functools.partial(_gla_ffn_kernel, ff_chunk=1024, final=final),
        grid=(B, n),
        in_specs=([row(D), row(on.shape[2]), pl.BlockSpec((None, N_MOD, D), lambda b, i: (b, 0, 0)),
                   whole(wo), whole(g2), whole(w1), whole(w2), whole(gf)]
                  + [first(a.shape[2]) for a in gla_in] + [ahead(a.shape[2]) for a in gla_in]
                  + [whole(wa_p), whole(ba), whole(gn)]),
        out_specs=row(D),
        out_shape=jax.ShapeDtypeStruct((B, S, D), F32),
        scratch_shapes=[pltpu.VMEM((GLA_HEADS, GLA_DK, GLA_DV), F32), pltpu.VMEM((tm, GLA_WIDTH), BF16)],
        compiler_params=_params("arbitrary", "arbitrary"),
        name="gla_ffn",
    )(x, on, mod, wo, g2, w1, w2, gf, *gla_in, *gla_in, wa_p, ba, gn)


def kernel(x, c, positions, w_ada, b_ada, norm1_g, w_in, gla_w_a2, gla_b_a, gla_norm_g, nsa_pe_k, nsa_pe_v, cmp_k_w1, cmp_k_w2, cmp_v_w1, cmp_v_w2, nsa_norm_g, w_o, norm2_g, w_ff1, w_ff2, final_norm_g):
    B, S, D = x.shape
    depth = w_in.shape[0]
    half = NSA_DH // 2
    inv = ROPE_THETA ** (-jnp.arange(half, dtype=F32) / half)
    inv = jnp.concatenate([inv, inv]).reshape(1, NSA_DH)
    for l in range(depth):
        mod = _adaln(c, w_ada[l], b_ada[l]).reshape(B, N_MOD, D)
        (gq, gk, gv, gg, nq, kc, vc, ks, vs, kw, vw, misc) = _in_proj(
            x, mod, norm1_g[l].reshape(1, D), positions, inv, w_in, l, tm=512)

        k_cmp, v_cmp = _compress(kc, vc, cmp_k_w1[l].astype(BF16), _compress_pe(nsa_pe_k[l]),
                                 cmp_k_w2[l].astype(BF16), cmp_v_w1[l].astype(BF16),
                                 _compress_pe(nsa_pe_v[l]), cmp_v_w2[l].astype(BF16), nb=int(np.gcd(B, 4)))
        o_nsa = _nsa(nq, k_cmp, v_cmp, ks, vs, kw, vw, misc, nsa_norm_g[l].reshape(1, NSA_WIDTH),
                     tq=512, tk=256)

        wa_p = jnp.zeros((MISC_W, GLA_QK), F32).at[0:GLA_RANK].set(gla_w_a2[l])
        x = _gla_ffn(x, o_nsa, mod, w_o[l].astype(BF16), norm2_g[l].reshape(1, D),
                     w_ff1[l].astype(BF16), w_ff2[l].astype(BF16), final_norm_g.reshape(1, D),
                     gq, gk, gv, gg, misc, wa_p, gla_b_a[l].reshape(1, GLA_QK),
                     gla_norm_g[l].reshape(1, GLA_WIDTH), final=(l == depth - 1))
    return x
```

```python
import functools

import numpy as np
import jax
import jax.numpy as jnp
from jax import lax
from jax.experimental import pallas as pl
from jax.experimental.pallas import tpu as pltpu

GLA_HEADS = 4
GLA_DK = 64
GLA_DV = 128
GLA_RANK = 16
GLA_TAU = 16.0
GLA_CHUNK = 64
NSA_HEADS = 4
NSA_KV_HEADS = 2
NSA_DH = 128
NSA_BRANCHES = 3
CMP_BLOCK = 32
CMP_STRIDE = 16
CMP_HIDDEN = 256
SEL_BLOCK = 64
SEL_TOPK = 16
WINDOW = 512
N_MOD = 6
ROPE_THETA = 10000.0
EPS = 1e-6
NEG = -1e30
BIG = 1e30

GLA_QK = GLA_HEADS * GLA_DK
GLA_WIDTH = GLA_HEADS * GLA_DV
NSA_WIDTH = NSA_HEADS * NSA_DH
NSA_KV_WIDTH = NSA_KV_HEADS * NSA_DH
N_GATE = NSA_HEADS * NSA_BRANCHES
IN_SPLITS = (GLA_QK, GLA_QK, GLA_WIDTH, GLA_WIDTH, GLA_RANK, NSA_WIDTH) + (NSA_KV_WIDTH,) * 6 + (N_GATE,)

LANES = 128
SUBLANES = 8
MISC_W = LANES
GATE_OFF = GLA_RANK
GLA_TILE = 256
VMEM_LIMIT = 56 * 1024 * 1024

F32 = jnp.float32
BF16 = jnp.bfloat16

_NT = (((1,), (1,)), ((), ()))
_TN = (((0,), (0,)), ((), ()))


def _params(*sem):
    return pltpu.CompilerParams(dimension_semantics=sem, vmem_limit_bytes=VMEM_LIMIT)


def _rms(x):
    return x * lax.rsqrt(jnp.mean(x * x, axis=-1, keepdims=True) + EPS)


def _split_bf16(x):
    hi = x.astype(BF16)
    return hi, (x - hi.astype(F32)).astype(BF16)


def _t_f32(x):
    return x.astype(F32).T


class _Staged:
    def __init__(self, gen):
        self.gen, self.value, self.done = gen, None, False

    def step(self):
        if not self.done:
            try:
                next(self.gen)
            except StopIteration as stop:
                self.value, self.done = stop.value, True

    def finish(self):
        while not self.done:
            self.step()
        return self.value


def _adaln_kernel(c_ref, w_ref, b_ref, o_ref):
    c = c_ref[...]
    n = c.shape[0]
    a_hi, a_lo = _split_bf16(c * jax.nn.sigmoid(c))
    w_hi, w_lo = _split_bf16(w_ref[...])
    y = jnp.dot(jnp.concatenate([a_hi, a_lo], axis=0), w_hi, preferred_element_type=F32)
    o_ref[...] = y[0:n] + y[n:2 * n] + jnp.dot(a_hi, w_lo, preferred_element_type=F32) + b_ref[...]


def _adaln(c, w_ada, b_ada):
    B, D = c.shape
    N = w_ada.shape[1]
    tn = D
    return pl.pallas_call(
        _adaln_kernel,
        grid=(N // tn,),
        in_specs=[pl.BlockSpec((B, D), lambda j: (0, 0)),
                  pl.BlockSpec((D, tn), lambda j: (0, j)),
                  pl.BlockSpec((1, tn), lambda j: (0, j))],
        out_specs=pl.BlockSpec((B, tn), lambda j: (0, j)),
        out_shape=jax.ShapeDtypeStruct((B, N), F32),
        compiler_params=_params("arbitrary"),
        name="adaln",
    )(c, w_ada, b_ada.reshape(1, N))


_PROJ_GROUPS = (("gq", GLA_QK, False), ("gk", GLA_QK, False), ("gv", GLA_WIDTH, False),
                ("gg", GLA_WIDTH, False), ("nq", NSA_WIDTH, True), ("kc", NSA_KV_WIDTH, True),
                ("vc", NSA_KV_WIDTH, False), ("ks", NSA_KV_WIDTH, True), ("vs", NSA_KV_WIDTH, False),
                ("kw", NSA_KV_WIDTH, True), ("vw", NSA_KV_WIDTH, False))
_PROJ_W = sum(g[1] for g in _PROJ_GROUPS) + MISC_W


def _w_in_moves():
    names = ("gq", "gk", "gv", "gg", "gr", "nq", "kc", "vc", "ks", "vs", "kw", "vw", "ngate")
    src = dict(zip(names, np.cumsum((0,) + IN_SPLITS[:-1])))
    width = dict(zip(names, IN_SPLITS))
    moves, dst = [], 0
    for name, w, _ in _PROJ_GROUPS:
        moves.append((int(src[name]), dst, w))
        dst += w
    for name in ("gr", "ngate"):
        moves.append((int(src[name]), dst, width[name]))
        dst += width[name]
    return tuple(moves)


_W_IN_MOVES = _w_in_moves()
_GROUPED = ("kc", "vc")
GRP_W = CMP_STRIDE * NSA_DH


def _in_proj_kernel(x_ref, mod_ref, g_ref, pos_ref, inv_ref, win_ref, *refs):
    n_out = len(_PROJ_GROUPS) + 1
    out_refs, stage_refs, w_ref = refs[:n_out], refs[n_out:-1], refs[-1]

    @pl.when((pl.program_id(0) == 0) & (pl.program_id(1) == 0))
    def _():
        for src, dst, width in _W_IN_MOVES:
            w_ref[:, dst:dst + width] = win_ref[:, src:src + width].astype(BF16)
        pad = _PROJ_W - _W_IN_MOVES[-1][1] - _W_IN_MOVES[-1][2]
        w_ref[:, _PROJ_W - pad:] = jnp.zeros((w_ref.shape[0], pad), BF16)

    tm = x_ref.shape[0]
    x = x_ref[...]
    h = _rms(x) * g_ref[...] * (1.0 + mod_ref[1:2, :]) + mod_ref[0:1, :]
    hb = h.astype(BF16)

    qs = NSA_DH ** -0.5 * float(np.log2(np.e))
    cos = sin = None

    off = 0
    for (name, width, rot), o_ref in zip(_PROJ_GROUPS, out_refs[:-1]):
        if rot and cos is None:
            pos = pos_ref[...].astype(F32)
            pos = jnp.concatenate([jnp.broadcast_to(pos[r:r + 1, :], (LANES, LANES)).T
                                   for r in range(pos.shape[0])], axis=0)
            lane = lax.broadcasted_iota(jnp.int32, (1, LANES), 1)
            low = lane < NSA_DH // 2
            ang = jnp.where(low, pos[0:tm // 2], pos[tm // 2:tm]) * inv_ref[...]
            c2, s2 = jnp.cos(ang), jnp.sin(ang)
            c2r, s2r = pltpu.roll(c2, NSA_DH // 2, axis=1), pltpu.roll(s2, NSA_DH // 2, axis=1)
            cos = jnp.concatenate([jnp.where(low, c2, c2r), jnp.where(low, c2r, c2)], axis=0)
            sin = jnp.concatenate([jnp.where(low, s2, s2r), jnp.where(low, s2r, s2)], axis=0)
            sin = sin * jnp.where(low, -1.0, 1.0)
        y = jnp.dot(hb, w_ref[:, off:off + width], preferred_element_type=F32)
        grouped = name in _GROUPED
        dst = stage_refs[_GROUPED.index(name)] if grouped else o_ref
        for hd in range(width // NSA_DH if (rot or grouped) else 0):
            yh = y[:, hd * NSA_DH:(hd + 1) * NSA_DH]
            if rot:
                c, s = (cos * qs, sin * qs) if name == "nq" else (cos, sin)
                yh = yh * c + pltpu.roll(yh, NSA_DH // 2, axis=1) * s
            if grouped:
                dst[hd] = yh
            else:
                dst[:, hd * NSA_DH:(hd + 1) * NSA_DH] = yh.astype(dst.dtype)
        if not (rot or grouped):
            dst[...] = y.astype(dst.dtype)
        if grouped:
            for k in range(NSA_KV_HEADS):
                for tok in range(CMP_STRIDE):
                    c0 = k * GRP_W + tok * NSA_DH
                    o_ref[:, c0:c0 + NSA_DH] = dst[k, pl.ds(tok, tm // CMP_STRIDE, stride=CMP_STRIDE), :].astype(
                        o_ref.dtype)
        off += width
    out_refs[-1][...] = jnp.dot(hb, w_ref[:, off:off + MISC_W], preferred_element_type=F32)


def _in_proj(x, mod, norm_g, positions, inv, w_in, layer, tm):
    B, S, D = x.shape
    pos4 = positions.reshape(B, S // tm, tm // LANES, LANES)
    row = lambda w: pl.BlockSpec((None, tm, w), lambda b, i: (b, i, 0))
    whole = lambda a: pl.BlockSpec(a.shape, lambda b, i: (0,) * a.ndim)
    out_shape, out_specs = [], []
    for name, w, _ in _PROJ_GROUPS:
        if name in _GROUPED:
            out_shape.append(jax.ShapeDtypeStruct((B, S // CMP_STRIDE, CMP_STRIDE * w), BF16))
            out_specs.append(pl.BlockSpec((None, tm // CMP_STRIDE, CMP_STRIDE * w), lambda b, i: (b, i, 0)))
        else:
            out_shape.append(jax.ShapeDtypeStruct((B, S, w), BF16))
            out_specs.append(row(w))
    out_shape.append(jax.ShapeDtypeStruct((B, S, MISC_W), F32))
    out_specs.append(row(MISC_W))
    return pl.pallas_call(
        _in_proj_kernel,
        grid=(B, S // tm),
        in_specs=[row(D), pl.BlockSpec((None, N_MOD, D), lambda b, i: (b, 0, 0)), whole(norm_g),
                  pl.BlockSpec((None, None, tm // LANES, LANES), lambda b, i: (b, i, 0, 0)), whole(inv),
                  pl.BlockSpec((None,) + w_in.shape[1:], lambda b, i: (layer, 0, 0),
                               pipeline_mode=pl.Buffered(1))],
        out_specs=out_specs,
        out_shape=out_shape,
        scratch_shapes=([pltpu.VMEM((NSA_KV_HEADS, tm, NSA_DH), F32) for _ in _GROUPED]
                        + [pltpu.VMEM((D, _PROJ_W), BF16)]),
        compiler_params=_params("arbitrary", "arbitrary"),
        name="in_proj",
    )(x, mod, norm_g, pos4, inv, w_in)


def _compress_kernel(xk_ref, xv_ref, kw1_ref, kpe_ref, kw2_ref, vw1_ref, vpe_ref, vw2_ref, ok_ref, ov_ref):
    nb, n_grp, _ = xk_ref.shape
    rows = nb * n_grp
    for x_ref, w1_ref, pe_ref, w2_ref, o_ref in ((xk_ref, kw1_ref, kpe_ref, kw2_ref, ok_ref),
                                                 (xv_ref, vw1_ref, vpe_ref, vw2_ref, ov_ref)):
        w1a = w1_ref[0:GRP_W, :]
        w1b = w1_ref[GRP_W:2 * GRP_W, :]
        x = jnp.concatenate([x_ref[:, :, k * GRP_W:(k + 1) * GRP_W].reshape(rows, GRP_W)
                             for k in range(NSA_KV_HEADS)], axis=0)
        bias = (jnp.dot(pe_ref[:, 0:GRP_W], w1a, preferred_element_type=F32)[0:1]
                + jnp.dot(pe_ref[:, GRP_W:2 * GRP_W], w1b, preferred_element_type=F32)[0:1])
        ua = jnp.dot(x, w1a, preferred_element_type=F32)
        ub = jnp.dot(x, w1b, preferred_element_type=F32)
        hid = ua + pltpu.roll(ub, NSA_KV_HEADS * rows - 1, axis=0) + bias
        act = jax.nn.gelu(hid, approximate=True).astype(BF16)
        y = jnp.dot(act, w2_ref[...], preferred_element_type=F32).astype(o_ref.dtype)
        for k in range(NSA_KV_HEADS):
            o_ref[:, :, k * NSA_DH:(k + 1) * NSA_DH] = y[k * rows:(k + 1) * rows].reshape(nb, n_grp, NSA_DH)


def _compress(xk, xv, kw1, kpe, kw2, vw1, vpe, vw2, nb):
    B, n_grp, gw = xk.shape
    W = NSA_KV_WIDTH
    whole = lambda a: pl.BlockSpec(a.shape, lambda b: (0,) * a.ndim)
    xspec = pl.BlockSpec((nb, n_grp, gw), lambda b: (b, 0, 0))
    ospec = pl.BlockSpec((nb, n_grp, W), lambda b: (b, 0, 0))
    return pl.pallas_call(
        _compress_kernel,
        grid=(B // nb,),
        in_specs=[xspec, xspec, whole(kw1), whole(kpe), whole(kw2), whole(vw1), whole(vpe), whole(vw2)],
        out_specs=[ospec, ospec],
        out_shape=[jax.ShapeDtypeStruct((B, n_grp, W), BF16)] * 2,
        compiler_params=_params("parallel"),
        name="compress",
    )(xk, xv, kw1, kpe, kw2, vw1, vpe, vw2)


def _compress_pe(pe):
    return jnp.zeros((SUBLANES, 2 * GRP_W), F32).at[0].set(pe.reshape(2 * GRP_W)).astype(BF16)


V_AUG = NSA_DH + 2 * SUBLANES
SEL_SHIFT = SEL_BLOCK.bit_length() - 1


def _attend(k_rows, qt, vt_aug, masked):
    s = jnp.dot(k_rows(), qt, preferred_element_type=F32)
    if masked is not None:
        s = s + masked()
    yield
    m = jnp.max(s, axis=0, keepdims=True)
    p = jnp.exp2((s - m).astype(BF16))
    yield
    return m, jnp.dot(vt_aug(), p, preferred_element_type=F32)


def _pipelined(gens, depth=1, side=()):
    tasks = [_Staged(g) for g in gens]
    for n in range(len(tasks) + depth):
        for t in tasks[max(0, n - depth):n + 1]:
            t.step()
        for s in side:
            s.step()
    return [t.finish() for t in tasks]


def _merge(parts):
    m_all = parts[0][0]
    for m, _ in parts[1:]:
        m_all = jnp.maximum(m_all, m)
    tot = None
    for m, acc in parts:
        w = acc * jnp.exp2(m - m_all)
        tot = w if tot is None else tot + w
    return tot[0:NSA_DH] / tot[NSA_DH:NSA_DH + 1]


def _nsa_step(c, tq, tk, q_ref, kc_ref, vc_ref, kw_ref, misc_ref, gn_ref, o_ref, kaug_ref, vst_ref, vwt_ref):
    S = kw_ref.shape[0]
    n_cmp = kc_ref.shape[0]
    n_sel = S // SEL_BLOCK
    G = NSA_HEADS // NSA_KV_HEADS
    M = G * tq
    q0 = c * tq

    t_q = q0 + lax.broadcasted_iota(jnp.int32, (1, tq), 1)
    t_m = jnp.concatenate([t_q] * G, axis=1)
    key_off = lax.broadcasted_iota(jnp.int32, (tk, 1), 0)

    n_col = lax.broadcasted_iota(jnp.int32, (n_cmp, 1), 0)
    cmp_end = jnp.where(n_col < n_cmp - 1, n_col * CMP_STRIDE + (CMP_BLOCK - 1), jnp.int32(2 ** 30))
    cmp_valid = cmp_end <= t_m
    any_valid = (t_m >= CMP_BLOCK - 1).astype(F32)
    jj = lax.broadcasted_iota(jnp.int32, (n_sel, n_cmp), 0) * SEL_BLOCK
    nn = lax.broadcasted_iota(jnp.int32, (n_sel, n_cmp), 1) * CMP_STRIDE
    ov_t = jnp.maximum(jnp.minimum(nn + CMP_BLOCK, jj + SEL_BLOCK) - jnp.maximum(nn, jj), 0)
    ov_t = (ov_t.astype(F32) * (1.0 / CMP_BLOCK)).astype(BF16)
    j_row = lax.broadcasted_iota(jnp.int32, (n_sel, tq), 0)
    blk_t = lax.shift_right_logical(q0 + lax.broadcasted_iota(jnp.int32, (n_sel, tq), 1), SEL_SHIFT)
    forced = (j_row == 0) | (j_row == blk_t) | (j_row == blk_t - 1)
    in_past = j_row <= blk_t

    heads = range(NSA_KV_HEADS)
    qts = [jnp.concatenate([_t_f32(q_ref[:, (k * G + g) * NSA_DH:(k * G + g + 1) * NSA_DH])
                            for g in range(G)], axis=1).astype(BF16) for k in heads]

    def select(k):
        hs = slice(k * NSA_DH, (k + 1) * NSA_DH)
        s = jnp.dot(kc_ref[:, hs], qts[k], preferred_element_type=F32)
        s = jnp.where(cmp_valid, s, NEG)
        yield
        e = jnp.exp2(s - jnp.max(s, axis=0, keepdims=True))
        p = e / jnp.sum(e, axis=0, keepdims=True) * any_valid
        yield
        o_cmp = jnp.dot(_t_f32(vc_ref[:, hs]).astype(BF16), p.astype(BF16), preferred_element_type=F32)
        p_grp = p[:, 0:tq]
        for g in range(1, G):
            p_grp = p_grp + p[:, g * tq:(g + 1) * tq]
        p_hi, p_lo = _split_bf16(p_grp)
        imp = (jnp.dot(ov_t, p_hi, preferred_element_type=F32)
               + jnp.dot(ov_t, p_lo, preferred_element_type=F32))
        imp = jnp.where(forced, BIG, jnp.where(in_past, imp, NEG))
        yield
        rank = []
        for r0 in range(0, n_sel, SUBLANES):
            blk = imp[r0:r0 + SUBLANES, :]
            cnt = jnp.zeros(blk.shape, F32)
            for j in range(n_sel):
                row = imp[j:j + 1, :]
                if j < r0:
                    cnt = cnt + jnp.where(row >= blk, 1.0, 0.0)
                elif j >= r0 + SUBLANES - 1:
                    cnt = cnt + jnp.where(row > blk, 1.0, 0.0)
                else:
                    below = r0 + lax.broadcasted_iota(jnp.int32, blk.shape, 0) > j
                    cnt = cnt + jnp.where(below, jnp.where(row >= blk, 1.0, 0.0), jnp.where(row > blk, 1.0, 0.0))
            rank.append(cnt)
            yield
        rank = jnp.concatenate(rank, axis=0)
        bias = jnp.where(rank < float(min(SEL_TOPK, n_sel)), 0.0, NEG).astype(BF16)
        return o_cmp, jnp.concatenate([qts[k], jnp.concatenate([bias] * G, axis=1),
                                       jnp.zeros((NSA_DH - n_sel, M), BF16)], axis=0)

    subs = range(tq // tk)
    col_off = jnp.concatenate([lax.broadcasted_iota(jnp.int32, (1, tk), 1)] * G, axis=1)
    after_bias = jnp.where(key_off > col_off, NEG, 0.0)
    far_bias = jnp.where(key_off <= col_off, NEG, 0.0)
    after = lambda: after_bias
    too_far = lambda: far_bias

    def cols(x, j):
        return jnp.concatenate([x[:, g * tq + j * tk:g * tq + (j + 1) * tk] for g in range(G)], axis=1)

    def sel_tasks(k, j, qta):
        first, diag = (0, q0 // tk - 1) if j is None else (q0 // tk, (q0 + j * tk) // tk)
        for kt in range(first, diag + 1):
            yield _attend(lambda kt=kt: kaug_ref[k, kt * tk:(kt + 1) * tk, :], qta,
                          lambda kt=kt: vst_ref[k, kt], after if kt == diag and j is not None else None)

    def win_tasks(k, j, qt):
        hs = slice(k * NSA_DH, (k + 1) * NSA_DH)
        diag = (q0 + j * tk) // tk
        back = WINDOW // tk
        for kt in range(max(diag - back, 0), diag + 1):
            masked = after if kt == diag else too_far if kt == diag - back else None
            yield _attend(lambda kt=kt: kw_ref[kt * tk:(kt + 1) * tk, hs], qt, lambda kt=kt: vwt_ref[k, kt], masked)

    def run(task_lists, side=()):
        flat = [(key, t) for key, tasks in task_lists.items() for t in tasks]
        parts = {key: [] for key in task_lists}
        for (key, _), part in zip(flat, _pipelined([t for _, t in flat], side=side)):
            parts[key].append(part)
        merged = {}
        for k in heads:
            for j in subs:
                shared = [(cols(m, j), cols(acc, j)) for m, acc in parts.get((k, None), [])]
                merged[k, j] = _merge(shared + parts[k, j])
        return [jnp.concatenate([merged[k, j][:, g * tk:(g + 1) * tk] for g in range(G) for j in subs], axis=1)
                for k in heads]

    selects = [_Staged(select(k)) for k in heads]
    o_win = run({(k, j): list(win_tasks(k, j, cols(qts[k], j))) for k in heads for j in subs}, side=selects)
    o_cmp, qtas = zip(*[s.finish() for s in selects])
    o_sel = run({(k, j): list(sel_tasks(k, j, qtas[k] if j is None else cols(qtas[k], j)))
                 for k in heads for j in (None, *subs)})


    gates_t = _t_f32(jax.nn.sigmoid(misc_ref[...]))
    outs = []
    for k in heads:
        for g in range(G):
            gs = slice(g * tq, (g + 1) * tq)
            gl = GATE_OFF + (k * G + g) * NSA_BRANCHES
            outs.append(gates_t[gl:gl + 1, :] * o_cmp[k][:, gs] + gates_t[gl + 1:gl + 2, :] * o_sel[k][:, gs]
                         + gates_t[gl + 2:gl + 3, :] * o_win[k][:, gs])
    ssq = outs[0] * outs[0]
    for o in outs[1:]:
        ssq = ssq + o * o
    inv = lax.rsqrt(jnp.sum(ssq, axis=0, keepdims=True) * (1.0 / NSA_WIDTH) + EPS)
    for hq, o in enumerate(outs):
        cs = slice(hq * NSA_DH, (hq + 1) * NSA_DH)
        o_ref[:, cs] = ((o * inv).T * gn_ref[:, cs]).astype(o_ref.dtype)


def _nsa_kernel(q_ref, kc_ref, vc_ref, ks_ref, vs_ref, kw_ref, vw_ref, misc_ref, gn_ref, o_ref,
                kaug_ref, vst_ref, vwt_ref):
    tq = q_ref.shape[0]
    tk = vst_ref.shape[3]
    S = ks_ref.shape[0]
    i = pl.program_id(1)

    @pl.when(i == 0)
    def _():
        pos = lax.broadcasted_iota(jnp.int32, (S, LANES), 0)
        lane = lax.broadcasted_iota(jnp.int32, (S, LANES), 1)
        onehot = jnp.where(lax.shift_right_logical(pos, SEL_SHIFT) == lane, 1.0, 0.0).astype(BF16)
        row = lax.broadcasted_iota(jnp.int32, (V_AUG - NSA_DH, tk), 0)
        ones_rows = jnp.where(row == 0, 1.0, 0.0).astype(BF16)
        for k in range(NSA_KV_HEADS):
            hs = slice(k * NSA_DH, (k + 1) * NSA_DH)
            kaug_ref[k, :, 0:NSA_DH] = ks_ref[:, hs]
            kaug_ref[k, :, NSA_DH:2 * NSA_DH] = onehot
            for kt in range(S // tk):
                rows = slice(kt * tk, (kt + 1) * tk)
                for src, dst in ((vs_ref, vst_ref), (vw_ref, vwt_ref)):
                    dst[k, kt, 0:NSA_DH, :] = _t_f32(src[rows, hs]).astype(BF16)
                    dst[k, kt, NSA_DH:V_AUG, :] = ones_rows

    for c in range(S // tq):
        pl.when(i == c)(functools.partial(_nsa_step, c, tq, tk, q_ref, kc_ref, vc_ref, kw_ref, misc_ref,
                                          gn_ref, o_ref, kaug_ref, vst_ref, vwt_ref))


def _nsa(nq, kcmp, vcmp, ks, vs, kw, vw, misc, gn, tq, tk):
    B, S, _ = nq.shape
    n_cmp = kcmp.shape[1]
    assert n_cmp <= LANES and S // SEL_BLOCK <= NSA_DH and SEL_BLOCK == 1 << SEL_SHIFT
    assert tq % tk == 0 and tq <= WINDOW and WINDOW % tk == 0
    row = lambda w: pl.BlockSpec((None, tq, w), lambda b, i: (b, i, 0))
    seq = lambda n: pl.BlockSpec((None, n, NSA_KV_WIDTH), lambda b, i: (b, 0, 0))
    vt_scratch = pltpu.VMEM((NSA_KV_HEADS, S // tk, V_AUG, tk), BF16)
    return pl.pallas_call(
        _nsa_kernel,
        grid=(B, S // tq),
        in_specs=[row(NSA_WIDTH), seq(n_cmp), seq(n_cmp), seq(S), seq(S), seq(S), seq(S), row(MISC_W),
                  pl.BlockSpec(gn.shape, lambda b, i: (0, 0))],
        out_specs=row(NSA_WIDTH),
        out_shape=jax.ShapeDtypeStruct((B, S, NSA_WIDTH), BF16),
        scratch_shapes=[pltpu.VMEM((NSA_KV_HEADS, S, 2 * NSA_DH), BF16), vt_scratch, vt_scratch],
        compiler_params=_params("parallel", "arbitrary"),
        name="nsa",
    )(nq, kcmp, vcmp, ks, vs, kw, vw, misc, gn)


def _log_sigmoid(z):
    return jnp.minimum(z, 0.0) - jnp.log(1.0 + jnp.exp(-jnp.abs(z)))


def _gla_tile(q, k, v, g, misc, wa, ba, gn, st):
    C = GLA_CHUNK
    T = q.shape[0]
    n_chunk = T // C
    r = lax.broadcasted_iota(jnp.int32, (T, T), 0)
    c = lax.broadcasted_iota(jnp.int32, (T, T), 1)
    causal = (r >= c) & (r - c <= (r & (C - 1)))
    tri = jnp.where(causal, 1.0, 0.0).astype(BF16)

    z = jnp.dot(misc.astype(BF16), wa.astype(BF16), preferred_element_type=F32) + ba
    la = _log_sigmoid(z) * (1.0 / GLA_TAU)
    yield
    b = jnp.dot(tri, la.astype(BF16), preferred_element_type=F32)
    b_lasts = [b[(ci + 1) * C - 1:(ci + 1) * C, :] for ci in range(n_chunk)]
    b_last = jnp.concatenate([jnp.broadcast_to(bl, (C, bl.shape[1])) for bl in b_lasts], axis=0)
    qf = q.astype(F32)
    kf = k.astype(F32)
    q_in = (qf * GLA_DK ** -0.5 * jnp.exp(b)).astype(BF16)
    k_in = (kf * jnp.exp(-b)).astype(BF16)
    k_dec = (kf * jnp.exp(b_last - b)).astype(BF16)
    pad = jnp.zeros((LANES - n_chunk, b.shape[1]), F32)
    dec_cols = jnp.exp(jnp.concatenate(b_lasts + [pad], axis=0).T)
    yield

    o_intra = []
    for h in range(GLA_HEADS):
        ks = slice(h * GLA_DK, (h + 1) * GLA_DK)
        att = lax.dot_general(q_in[:, ks], k_in[:, ks], _NT, preferred_element_type=F32)
        att = jnp.where(causal, att, 0.0).astype(BF16)
        o_intra.append(jnp.dot(att, v[:, h * GLA_DV:(h + 1) * GLA_DV], preferred_element_type=F32))
        yield

    st = [st[h] for h in range(GLA_HEADS)]
    o_inter = []
    for ci in range(n_chunk):
        rows = slice(ci * C, (ci + 1) * C)
        out = []
        for h in range(GLA_HEADS):
            ks = slice(h * GLA_DK, (h + 1) * GLA_DK)
            vs = slice(h * GLA_DV, (h + 1) * GLA_DV)
            out.append(jnp.dot(q_in[rows, ks], st[h].astype(BF16), preferred_element_type=F32))
            d_st = lax.dot_general(k_dec[rows, ks], v[rows, vs], _TN, preferred_element_type=F32)
            st[h] = st[h] * dec_cols[ks, ci:ci + 1] + d_st
        o_inter.append(jnp.concatenate(out, axis=1))
        yield
    st = jnp.stack(st)
    o_inter = jnp.concatenate(o_inter, axis=0)
    gate = g.astype(F32)
    gate = gate * jax.nn.sigmoid(gate)
    outs = []
    for h in range(GLA_HEADS):
        vs = slice(h * GLA_DV, (h + 1) * GLA_DV)
        outs.append(_rms(o_intra[h] + o_inter[:, vs]) * gn[:, vs] * gate[:, vs])
    return jnp.concatenate(outs, axis=1), st


def _gla_ffn_kernel(x_ref, on_ref, mod_ref, wo_ref, g2_ref, w1_ref, w2_ref, gf_ref,
                    q0_ref, k0_ref, v0_ref, g0_ref, m0_ref, q1_ref, k1_ref, v1_ref, g1_ref, m1_ref,
                    wa_ref, ba_ref, gn_ref, o_ref, st_ref, og_ref, *, ff_chunk, final):
    def gla(q_ref, k_ref, v_ref, g_ref, m_ref, st):
        return _Staged(_gla_tile(q_ref[...], k_ref[...], v_ref[...], g_ref[...], m_ref[...],
                                 wa_ref[...], ba_ref[...], gn_ref[...], st))

    @pl.when((pl.program_id(0) == 0) & (pl.program_id(1) == 0))
    def _():
        o, st = gla(q0_ref, k0_ref, v0_ref, g0_ref, m0_ref, jnp.zeros(st_ref.shape, F32)).finish()
        og_ref[...] = o.astype(BF16)
        st_ref[...] = st

    last = pl.program_id(1) == pl.num_programs(1) - 1
    ahead = gla(q1_ref, k1_ref, v1_ref, g1_ref, m1_ref, jnp.where(last, 0.0, st_ref[...]))

    ahead.step()
    mix = (jnp.dot(og_ref[...], wo_ref[0:GLA_WIDTH, :], preferred_element_type=F32)
           + jnp.dot(on_ref[...], wo_ref[GLA_WIDTH:, :], preferred_element_type=F32))
    x1 = x_ref[...] + mod_ref[2:3, :] * mix
    h = (_rms(x1) * g2_ref[...] * (1.0 + mod_ref[4:5, :]) + mod_ref[3:4, :]).astype(BF16)
    acc = jnp.zeros_like(x1)
    ahead.step()
    for c in range(w1_ref.shape[1] // ff_chunk):
        cs = slice(c * ff_chunk, (c + 1) * ff_chunk)
        a = jnp.maximum(jnp.dot(h, w1_ref[:, cs], preferred_element_type=F32), 0.0)
        ahead.step()
        acc = acc + jnp.dot((a * a).astype(BF16), w2_ref[cs, :], preferred_element_type=F32)
        ahead.step()
    x2 = x1 + mod_ref[5:6, :] * acc
    o_ref[...] = _rms(x2) * gf_ref[...] if final else x2

    o, st = ahead.finish()
    og_ref[...] = o.astype(BF16)
    st_ref[...] = st


def _gla_ffn(x, on, mod, wo, g2, w1, w2, gf, gq, gk, gv, gg, misc, wa_p, ba, gn, final):
    B, S, D = x.shape
    tm = GLA_TILE
    n = S // tm
    row = lambda w: pl.BlockSpec((None, tm, w), lambda b, i: (b, i, 0))
    first = lambda w: pl.BlockSpec((None, tm, w), lambda b, i: (0, 0, 0))
    ahead = lambda w: pl.BlockSpec((None, tm, w), lambda b, i: (
        jnp.where(i + 1 < n, b, jnp.minimum(b + 1, B - 1)), jnp.where(i + 1 < n, i + 1, 0), 0))
    whole = lambda a: pl.BlockSpec(a.shape, lambda b, i: (0,) * a.ndim)
    gla_in = (gq, gk, gv, gg, misc)
    return pl.pallas_call(
        functools.partial(_gla_ffn_kernel, ff_chunk=1024, final=final),
        grid=(B, n),
        in_specs=([row(D), row(on.shape[2]), pl.BlockSpec((None, N_MOD, D), lambda b, i: (b, 0, 0)),
                   whole(wo), whole(g2), whole(w1), whole(w2), whole(gf)]
                  + [first(a.shape[2]) for a in gla_in] + [ahead(a.shape[2]) for a in gla_in]
                  + [whole(wa_p), whole(ba), whole(gn)]),
        out_specs=row(D),
        out_shape=jax.ShapeDtypeStruct((B, S, D), F32),
        scratch_shapes=[pltpu.VMEM((GLA_HEADS, GLA_DK, GLA_DV), F32), pltpu.VMEM((tm, GLA_WIDTH), BF16)],
        compiler_params=_params("arbitrary", "arbitrary"),
        name="gla_ffn",
    )(x, on, mod, wo, g2, w1, w2, gf, *gla_in, *gla_in, wa_p, ba, gn)


def kernel(x, c, positions, w_ada, b_ada, norm1_g, w_in, gla_w_a2, gla_b_a, gla_norm_g, nsa_pe_k, nsa_pe_v, cmp_k_w1, cmp_k_w2, cmp_v_w1, cmp_v_w2, nsa_norm_g, w_o, norm2_g, w_ff1, w_ff2, final_norm_g):
    B, S, D = x.shape
    depth = w_in.shape[0]
    half = NSA_DH // 2
    inv = ROPE_THETA ** (-jnp.arange(half, dtype=F32) / half)
    inv = jnp.concatenate([inv, inv]).reshape(1, NSA_DH)
    for l in range(depth):
        mod = _adaln(c, w_ada[l], b_ada[l]).reshape(B, N_MOD, D)
        (gq, gk, gv, gg, nq, kc, vc, ks, vs, kw, vw, misc) = _in_proj(
            x, mod, norm1_g[l].reshape(1, D), positions, inv, w_in, l, tm=512)

        k_cmp, v_cmp = _compress(kc, vc, cmp_k_w1[l].astype(BF16), _compress_pe(nsa_pe_k[l]),
                                 cmp_k_w2[l].astype(BF16), cmp_v_w1[l].astype(BF16),
                                 _compress_pe(nsa_pe_v[l]), cmp_v_w2[l].astype(BF16), nb=int(np.gcd(B, 4)))
        o_nsa = _nsa(nq, k_cmp, v_cmp, ks, vs, kw, vw, misc, nsa_norm_g[l].reshape(1, NSA_WIDTH),
                     tq=512, tk=256)

        wa_p = jnp.zeros((MISC_W, GLA_QK), F32).at[0:GLA_RANK].set(gla_w_a2[l])
        x = _gla_ffn(x, o_nsa, mod, w_o[l].astype(BF16), norm2_g[l].reshape(1, D),
                     w_ff1[l].astype(BF16), w_ff2[l].astype(BF16), final_norm_g.reshape(1, D),
                     gq, gk, gv, gg, misc, wa_p, gla_b_a[l].reshape(1, GLA_QK),
                     gla_norm_g[l].reshape(1, GLA_WIDTH), final=(l == depth - 1))
    return x
```

```python
import functools

import numpy as np
import jax
import jax.numpy as jnp
from jax import lax
from jax.experimental import pallas as pl
from jax.experimental.pallas import tpu as pltpu

GLA_HEADS = 4
GLA_DK = 64
GLA_DV = 128
GLA_RANK = 16
GLA_TAU = 16.0
GLA_CHUNK = 64
NSA_HEADS = 4
NSA_KV_HEADS = 2
NSA_DH = 128
NSA_BRANCHES = 3
CMP_BLOCK = 32
CMP_STRIDE = 16
CMP_HIDDEN = 256
SEL_BLOCK = 64
SEL_TOPK = 16
WINDOW = 512
N_MOD = 6
ROPE_THETA = 10000.0
EPS = 1e-6
NEG = -1e30
BIG = 1e30

GLA_QK = GLA_HEADS * GLA_DK
GLA_WIDTH = GLA_HEADS * GLA_DV
NSA_WIDTH = NSA_HEADS * NSA_DH
NSA_KV_WIDTH = NSA_KV_HEADS * NSA_DH
N_GATE = NSA_HEADS * NSA_BRANCHES
IN_SPLITS = (GLA_QK, GLA_QK, GLA_WIDTH, GLA_WIDTH, GLA_RANK, NSA_WIDTH) + (NSA_KV_WIDTH,) * 6 + (N_GATE,)

LANES = 128
SUBLANES = 8
MISC_W = LANES
GATE_OFF = GLA_RANK
GLA_TILE = 256
VMEM_LIMIT = 56 * 1024 * 1024

F32 = jnp.float32
BF16 = jnp.bfloat16

_NT = (((1,), (1,)), ((), ()))
_TN = (((0,), (0,)), ((), ()))


def _params(*sem):
    return pltpu.CompilerParams(dimension_semantics=sem, vmem_limit_bytes=VMEM_LIMIT)


def _rms(x):
    return x * lax.rsqrt(jnp.mean(x * x, axis=-1, keepdims=True) + EPS)


def _split_bf16(x):
    hi = x.astype(BF16)
    return hi, (x - hi.astype(F32)).astype(BF16)


def _t_f32(x):
    return x.astype(F32).T


class _Staged:
    def __init__(self, gen):
        self.gen, self.value, self.done = gen, None, False

    def step(self):
        if not self.done:
            try:
                next(self.gen)
            except StopIteration as stop:
                self.value, self.done = stop.value, True

    def finish(self):
        while not self.done:
            self.step()
        return self.value


def _adaln_kernel(c_ref, w_ref, b_ref, o_ref):
    c = c_ref[...]
    n = c.shape[0]
    a_hi, a_lo = _split_bf16(c * jax.nn.sigmoid(c))
    w_hi, w_lo = _split_bf16(w_ref[...])
    y = jnp.dot(jnp.concatenate([a_hi, a_lo], axis=0), w_hi, preferred_element_type=F32)
    o_ref[...] = y[0:n] + y[n:2 * n] + jnp.dot(a_hi, w_lo, preferred_element_type=F32) + b_ref[...]


def _adaln(c, w_ada, b_ada):
    B, D = c.shape
    N = w_ada.shape[1]
    tn = D
    return pl.pallas_call(
        _adaln_kernel,
        grid=(N // tn,),
        in_specs=[pl.BlockSpec((B, D), lambda j: (0, 0)),
                  pl.BlockSpec((D, tn), lambda j: (0, j)),
                  pl.BlockSpec((1, tn), lambda j: (0, j))],
        out_specs=pl.BlockSpec((B, tn), lambda j: (0, j)),
        out_shape=jax.ShapeDtypeStruct((B, N), F32),
        compiler_params=_params("arbitrary"),
        name="adaln",
    )(c, w_ada, b_ada.reshape(1, N))


_PROJ_GROUPS = (("gq", GLA_QK, False), ("gk", GLA_QK, False), ("gv", GLA_WIDTH, False),
                ("gg", GLA_WIDTH, False), ("nq", NSA_WIDTH, True), ("kc", NSA_KV_WIDTH, True),
                ("vc", NSA_KV_WIDTH, False), ("ks", NSA_KV_WIDTH, True), ("vs", NSA_KV_WIDTH, False),
                ("kw", NSA_KV_WIDTH, True), ("vw", NSA_KV_WIDTH, False))
_PROJ_W = sum(g[1] for g in _PROJ_GROUPS) + MISC_W


def _w_in_moves():
    names = ("gq", "gk", "gv", "gg", "gr", "nq", "kc", "vc", "ks", "vs", "kw", "vw", "ngate")
    src = dict(zip(names, np.cumsum((0,) + IN_SPLITS[:-1])))
    width = dict(zip(names, IN_SPLITS))
    moves, dst = [], 0
    for name, w, _ in _PROJ_GROUPS:
        moves.append((int(src[name]), dst, w))
        dst += w
    for name in ("gr", "ngate"):
        moves.append((int(src[name]), dst, width[name]))
        dst += width[name]
    return tuple(moves)


_W_IN_MOVES = _w_in_moves()
_GROUPED = ("kc", "vc")
GRP_W = CMP_STRIDE * NSA_DH


def _in_proj_kernel(x_ref, mod_ref, g_ref, pos_ref, inv_ref, win_ref, *refs):
    n_out = len(_PROJ_GROUPS) + 1
    out_refs, stage_refs, w_ref = refs[:n_out], refs[n_out:-1], refs[-1]

    @pl.when((pl.program_id(0) == 0) & (pl.program_id(1) == 0))
    def _():
        for src, dst, width in _W_IN_MOVES:
            w_ref[:, dst:dst + width] = win_ref[:, src:src + width].astype(BF16)
        pad = _PROJ_W - _W_IN_MOVES[-1][1] - _W_IN_MOVES[-1][2]
        w_ref[:, _PROJ_W - pad:] = jnp.zeros((w_ref.shape[0], pad), BF16)

    tm = x_ref.shape[0]
    x = x_ref[...]
    h = _rms(x) * g_ref[...] * (1.0 + mod_ref[1:2, :]) + mod_ref[0:1, :]
    hb = h.astype(BF16)

    qs = NSA_DH ** -0.5 * float(np.log2(np.e))
    cos = sin = None

    off = 0
    for (name, width, rot), o_ref in zip(_PROJ_GROUPS, out_refs[:-1]):
        if rot and cos is None:
            pos = pos_ref[...].astype(F32)
            pos = jnp.concatenate([jnp.broadcast_to(pos[r:r + 1, :], (LANES, LANES)).T
                                   for r in range(pos.shape[0])], axis=0)
            lane = lax.broadcasted_iota(jnp.int32, (1, LANES), 1)
            low = lane < NSA_DH // 2
            ang = jnp.where(low, pos[0:tm // 2], pos[tm // 2:tm]) * inv_ref[...]
            c2, s2 = jnp.cos(ang), jnp.sin(ang)
            c2r, s2r = pltpu.roll(c2, NSA_DH // 2, axis=1), pltpu.roll(s2, NSA_DH // 2, axis=1)
            cos = jnp.concatenate([jnp.where(low, c2, c2r), jnp.where(low, c2r, c2)], axis=0)
            sin = jnp.concatenate([jnp.where(low, s2, s2r), jnp.where(low, s2r, s2)], axis=0)
            sin = sin * jnp.where(low, -1.0, 1.0)
        y = jnp.dot(hb, w_ref[:, off:off + width], preferred_element_type=F32)
        grouped = name in _GROUPED
        dst = stage_refs[_GROUPED.index(name)] if grouped else o_ref
        for hd in range(width // NSA_DH if (rot or grouped) else 0):
            yh = y[:, hd * NSA_DH:(hd + 1) * NSA_DH]
            if rot:
                c, s = (cos * qs, sin * qs) if name == "nq" else (cos, sin)
                yh = yh * c + pltpu.roll(yh, NSA_DH // 2, axis=1) * s
            if grouped:
                dst[hd] = yh
            else:
                dst[:, hd * NSA_DH:(hd + 1) * NSA_DH] = yh.astype(dst.dtype)
        if not (rot or grouped):
            dst[...] = y.astype(dst.dtype)
        if grouped:
            for k in range(NSA_KV_HEADS):
                for tok in range(CMP_STRIDE):
                    c0 = k * GRP_W + tok * NSA_DH
                    o_ref[:, c0:c0 + NSA_DH] = dst[k, pl.ds(tok, tm // CMP_STRIDE, stride=CMP_STRIDE), :].astype(
                        o_ref.dtype)
        off += width
    out_refs[-1][...] = jnp.dot(hb, w_ref[:, off:off + MISC_W], preferred_element_type=F32)


def _in_proj(x, mod, norm_g, positions, inv, w_in, layer, tm):
    B, S, D = x.shape
    pos4 = positions.reshape(B, S // tm, tm // LANES, LANES)
    row = lambda w: pl.BlockSpec((None, tm, w), lambda b, i: (b, i, 0))
    whole = lambda a: pl.BlockSpec(a.shape, lambda b, i: (0,) * a.ndim)
    out_shape, out_specs = [], []
    for name, w, _ in _PROJ_GROUPS:
        if name in _GROUPED:
            out_shape.append(jax.ShapeDtypeStruct((B, S // CMP_STRIDE, CMP_STRIDE * w), BF16))
            out_specs.append(pl.BlockSpec((None, tm // CMP_STRIDE, CMP_STRIDE * w), lambda b, i: (b, i, 0)))
        else:
            out_shape.append(jax.ShapeDtypeStruct((B, S, w), BF16))
            out_specs.append(row(w))
    out_shape.append(jax.ShapeDtypeStruct((B, S, MISC_W), F32))
    out_specs.append(row(MISC_W))
    return pl.pallas_call(
        _in_proj_kernel,
        grid=(B, S // tm),
        in_specs=[row(D), pl.BlockSpec((None, N_MOD, D), lambda b, i: (b, 0, 0)), whole(norm_g),
                  pl.BlockSpec((None, None, tm // LANES, LANES), lambda b, i: (b, i, 0, 0)), whole(inv),
                  pl.BlockSpec((None,) + w_in.shape[1:], lambda b, i: (layer, 0, 0),
                               pipeline_mode=pl.Buffered(1))],
        out_specs=out_specs,
        out_shape=out_shape,
        scratch_shapes=([pltpu.VMEM((NSA_KV_HEADS, tm, NSA_DH), F32) for _ in _GROUPED]
                        + [pltpu.VMEM((D, _PROJ_W), BF16)]),
        compiler_params=_params("arbitrary", "arbitrary"),
        name="in_proj",
    )(x, mod, norm_g, pos4, inv, w_in)


def _compress_kernel(xk_ref, xv_ref, kw1_ref, kpe_ref, kw2_ref, vw1_ref, vpe_ref, vw2_ref, ok_ref, ov_ref):
    nb, n_grp, _ = xk_ref.shape
    rows = nb * n_grp
    for x_ref, w1_ref, pe_ref, w2_ref, o_ref in ((xk_ref, kw1_ref, kpe_ref, kw2_ref, ok_ref),
                                                 (xv_ref, vw1_ref, vpe_ref, vw2_ref, ov_ref)):
        w1a = w1_ref[0:GRP_W, :]
        w1b = w1_ref[GRP_W:2 * GRP_W, :]
        x = jnp.concatenate([x_ref[:, :, k * GRP_W:(k + 1) * GRP_W].reshape(rows, GRP_W)
                             for k in range(NSA_KV_HEADS)], axis=0)
        bias = (jnp.dot(pe_ref[:, 0:GRP_W], w1a, preferred_element_type=F32)[0:1]
                + jnp.dot(pe_ref[:, GRP_W:2 * GRP_W], w1b, preferred_element_type=F32)[0:1])
        ua = jnp.dot(x, w1a, preferred_element_type=F32)
        ub = jnp.dot(x, w1b, preferred_element_type=F32)
        hid = ua + pltpu.roll(ub, NSA_KV_HEADS * rows - 1, axis=0) + bias
        act = jax.nn.gelu(hid, approximate=True).astype(BF16)
        y = jnp.dot(act, w2_ref[...], preferred_element_type=F32).astype(o_ref.dtype)
        for k in range(NSA_KV_HEADS):
            o_ref[:, :, k * NSA_DH:(k + 1) * NSA_DH] = y[k * rows:(k + 1) * rows].reshape(nb, n_grp, NSA_DH)


def _compress(xk, xv, kw1, kpe, kw2, vw1, vpe, vw2, nb):
    B, n_grp, gw = xk.shape
    W = NSA_KV_WIDTH
    whole = lambda a: pl.BlockSpec(a.shape, lambda b: (0,) * a.ndim)
    xspec = pl.BlockSpec((nb, n_grp, gw), lambda b: (b, 0, 0))
    ospec = pl.BlockSpec((nb, n_grp, W), lambda b: (b, 0, 0))
    return pl.pallas_call(
        _compress_kernel,
        grid=(B // nb,),
        in_specs=[xspec, xspec, whole(kw1), whole(kpe), whole(kw2), whole(vw1), whole(vpe), whole(vw2)],
        out_specs=[ospec, ospec],
        out_shape=[jax.ShapeDtypeStruct((B, n_grp, W), BF16)] * 2,
        compiler_params=_params("parallel"),
        name="compress",
    )(xk, xv, kw1, kpe, kw2, vw1, vpe, vw2)


def _compress_pe(pe):
    return jnp.zeros((SUBLANES, 2 * GRP_W), F32).at[0].set(pe.reshape(2 * GRP_W)).astype(BF16)


V_AUG = NSA_DH + 2 * SUBLANES
SEL_SHIFT = SEL_BLOCK.bit_length() - 1


def _attend(k_rows, qt, vt_aug, masked):
    s = jnp.dot(k_rows(), qt, preferred_element_type=F32)
    if masked is not None:
        s = s + masked()
    yield
    m = jnp.max(s, axis=0, keepdims=True)
    p = jnp.exp2((s - m).astype(BF16))
    yield
    return m, jnp.dot(vt_aug(), p, preferred_element_type=F32)


def _pipelined(gens, depth=1, side=()):
    tasks = [_Staged(g) for g in gens]
    for n in range(len(tasks) + depth):
        for t in tasks[max(0, n - depth):n + 1]:
            t.step()
        for s in side:
            s.step()
    return [t.finish() for t in tasks]


def _merge(parts):
    m_all = parts[0][0]
    for m, _ in parts[1:]:
        m_all = jnp.maximum(m_all, m)
    tot = None
    for m, acc in parts:
        w = acc * jnp.exp2(m - m_all)
        tot = w if tot is None else tot + w
    return tot[0:NSA_DH] / tot[NSA_DH:NSA_DH + 1]


def _nsa_step(c, tq, tk, q_ref, kc_ref, vc_ref, kw_ref, misc_ref, gn_ref, o_ref, kaug_ref, vst_ref, vwt_ref):
    S = kw_ref.shape[0]
    n_cmp = kc_ref.shape[0]
    n_sel = S // SEL_BLOCK
    G = NSA_HEADS // NSA_KV_HEADS
    M = G * tq
    q0 = c * tq

    t_q = q0 + lax.broadcasted_iota(jnp.int32, (1, tq), 1)
    t_m = jnp.concatenate([t_q] * G, axis=1)
    key_off = lax.broadcasted_iota(jnp.int32, (tk, 1), 0)

    n_col = lax.broadcasted_iota(jnp.int32, (n_cmp, 1), 0)
    cmp_end = jnp.where(n_col < n_cmp - 1, n_col * CMP_STRIDE + (CMP_BLOCK - 1), jnp.int32(2 ** 30))
    cmp_valid = cmp_end <= t_m
    any_valid = (t_m >= CMP_BLOCK - 1).astype(F32)
    jj = lax.broadcasted_iota(jnp.int32, (n_sel, n_cmp), 0) * SEL_BLOCK
    nn = lax.broadcasted_iota(jnp.int32, (n_sel, n_cmp), 1) * CMP_STRIDE
    ov_t = jnp.maximum(jnp.minimum(nn + CMP_BLOCK, jj + SEL_BLOCK) - jnp.maximum(nn, jj), 0)
    ov_t = (ov_t.astype(F32) * (1.0 / CMP_BLOCK)).astype(BF16)
    j_row = lax.broadcasted_iota(jnp.int32, (n_sel, tq), 0)
    blk_t = lax.shift_right_logical(q0 + lax.broadcasted_iota(jnp.int32, (n_sel, tq), 1), SEL_SHIFT)
    forced = (j_row == 0) | (j_row == blk_t) | (j_row == blk_t - 1)
    in_past = j_row <= blk_t

    heads = range(NSA_KV_HEADS)
    qts = [jnp.concatenate([_t_f32(q_ref[:, (k * G + g) * NSA_DH:(k * G + g + 1) * NSA_DH])
                            for g in range(G)], axis=1).astype(BF16) for k in heads]

    def select(k):
        hs = slice(k * NSA_DH, (k + 1) * NSA_DH)
        s = jnp.dot(kc_ref[:, hs], qts[k], preferred_element_type=F32)
        s = jnp.where(cmp_valid, s, NEG)
        yield
        e = jnp.exp2(s - jnp.max(s, axis=0, keepdims=True))
        p = e / jnp.sum(e, axis=0, keepdims=True) * any_valid
        yield
        o_cmp = jnp.dot(_t_f32(vc_ref[:, hs]).astype(BF16), p.astype(BF16), preferred_element_type=F32)
        p_grp = p[:, 0:tq]
        for g in range(1, G):
            p_grp = p_grp + p[:, g * tq:(g + 1) * tq]
        p_hi, p_lo = _split_bf16(p_grp)
        imp = (jnp.dot(ov_t, p_hi, preferred_element_type=F32)
               + jnp.dot(ov_t, p_lo, preferred_element_type=F32))
        imp = jnp.where(forced, BIG, jnp.where(in_past, imp, NEG))
        yield
        rank = []
        for r0 in range(0, n_sel, SUBLANES):
            blk = imp[r0:r0 + SUBLANES, :]
            cnt = jnp.zeros(blk.shape, F32)
            for j in range(n_sel):
                row = imp[j:j + 1, :]
                if j < r0:
                    cnt = cnt + jnp.where(row >= blk, 1.0, 0.0)
                elif j >= r0 + SUBLANES - 1:
                    cnt = cnt + jnp.where(row > blk, 1.0, 0.0)
                else:
                    below = r0 + lax.broadcasted_iota(jnp.int32, blk.shape, 0) > j
                    cnt = cnt + jnp.where(below, jnp.where(row >= blk, 1.0, 0.0), jnp.where(row > blk, 1.0, 0.0))
            rank.append(cnt)
            yield
        rank = jnp.concatenate(rank, axis=0)
        bias = jnp.where(rank < float(min(SEL_TOPK, n_sel)), 0.0, NEG).astype(BF16)
        return o_cmp, jnp.concatenate([qts[k], jnp.concatenate([bias] * G, axis=1),
                                       jnp.zeros((NSA_DH - n_sel, M), BF16)], axis=0)

    subs = range(tq // tk)
    col_off = jnp.concatenate([lax.broadcasted_iota(jnp.int32, (1, tk), 1)] * G, axis=1)
    after_bias = jnp.where(key_off > col_off, NEG, 0.0)
    far_bias = jnp.where(key_off <= col_off, NEG, 0.0)
    after = lambda: after_bias
    too_far = lambda: far_bias

    def cols(x, j):
        return jnp.concatenate([x[:, g * tq + j * tk:g * tq + (j + 1) * tk] for g in range(G)], axis=1)

    def sel_tasks(k, j, qta):
        first, diag = (0, q0 // tk - 1) if j is None else (q0 // tk, (q0 + j * tk) // tk)
        for kt in range(first, diag + 1):
            yield _attend(lambda kt=kt: kaug_ref[k, kt * tk:(kt + 1) * tk, :], qta,
                          lambda kt=kt: vst_ref[k, kt], after if kt == diag and j is not None else None)

    def win_tasks(k, j, qt):
        hs = slice(k * NSA_DH, (k + 1) * NSA_DH)
        diag = (q0 + j * tk) // tk
        back = WINDOW // tk
        for kt in range(max(diag - back, 0), diag + 1):
            masked = after if kt == diag else too_far if kt == diag - back else None
            yield _attend(lambda kt=kt: kw_ref[kt * tk:(kt + 1) * tk, hs], qt, lambda kt=kt: vwt_ref[k, kt], masked)

    def run(task_lists, side=()):
        flat = [(key, t) for key, tasks in task_lists.items() for t in tasks]
        parts = {key: [] for key in task_lists}
        for (key, _), part in zip(flat, _pipelined([t for _, t in flat], side=side)):
            parts[key].append(part)
        merged = {}
        for k in heads:
            for j in subs:
                shared = [(cols(m, j), cols(acc, j)) for m, acc in parts.get((k, None), [])]
                merged[k, j] = _merge(shared + parts[k, j])
        return [jnp.concatenate([merged[k, j][:, g * tk:(g + 1) * tk] for g in range(G) for j in subs], axis=1)
                for k in heads]

    selects = [_Staged(select(k)) for k in heads]
    o_win = run({(k, j): list(win_tasks(k, j, cols(qts[k], j))) for k in heads for j in subs}, side=selects)
    o_cmp, qtas = zip(*[s.finish() for s in selects])
    o_sel = run({(k, j): list(sel_tasks(k, j, qtas[k] if j is None else cols(qtas[k], j)))
                 for k in heads for j in (None, *subs)})


    gates_t = _t_f32(jax.nn.sigmoid(misc_ref[...]))
    outs = []
    for k in heads:
        for g in range(G):
            gs = slice(g * tq, (g + 1) * tq)
            gl = GATE_OFF + (k * G + g) * NSA_BRANCHES
            outs.append(gates_t[gl:gl + 1, :] * o_cmp[k][:, gs] + gates_t[gl + 1:gl + 2, :] * o_sel[k][:, gs]
                         + gates_t[gl + 2:gl + 3, :] * o_win[k][:, gs])
    ssq = outs[0] * outs[0]
    for o in outs[1:]:
        ssq = ssq + o * o
    inv = lax.rsqrt(jnp.sum(ssq, axis=0, keepdims=True) * (1.0 / NSA_WIDTH) + EPS)
    for hq, o in enumerate(outs):
        cs = slice(hq * NSA_DH, (hq + 1) * NSA_DH)
        o_ref[:, cs] = ((o * inv).T * gn_ref[:, cs]).astype(o_ref.dtype)


def _nsa_kernel(q_ref, kc_ref, vc_ref, ks_ref, vs_ref, kw_ref, vw_ref, misc_ref, gn_ref, o_ref,
                kaug_ref, vst_ref, vwt_ref):
    tq = q_ref.shape[0]
    tk = vst_ref.shape[3]
    S = ks_ref.shape[0]
    i = pl.program_id(1)

    @pl.when(i == 0)
    def _():
        pos = lax.broadcasted_iota(jnp.int32, (S, LANES), 0)
        lane = lax.broadcasted_iota(jnp.int32, (S, LANES), 1)
        onehot = jnp.where(lax.shift_right_logical(pos, SEL_SHIFT) == lane, 1.0, 0.0).astype(BF16)
        row = lax.broadcasted_iota(jnp.int32, (V_AUG - NSA_DH, tk), 0)
        ones_rows = jnp.where(row == 0, 1.0, 0.0).astype(BF16)
        for k in range(NSA_KV_HEADS):
            hs = slice(k * NSA_DH, (k + 1) * NSA_DH)
            kaug_ref[k, :, 0:NSA_DH] = ks_ref[:, hs]
            kaug_ref[k, :, NSA_DH:2 * NSA_DH] = onehot
            for kt in range(S // tk):
                rows = slice(kt * tk, (kt + 1) * tk)
                for src, dst in ((vs_ref, vst_ref), (vw_ref, vwt_ref)):
                    dst[k, kt, 0:NSA_DH, :] = _t_f32(src[rows, hs]).astype(BF16)
                    dst[k, kt, NSA_DH:V_AUG, :] = ones_rows

    for c in range(S // tq):
        pl.when(i == c)(functools.partial(_nsa_step, c, tq, tk, q_ref, kc_ref, vc_ref, kw_ref, misc_ref,
                                          gn_ref, o_ref, kaug_ref, vst_ref, vwt_ref))


def _nsa(nq, kcmp, vcmp, ks, vs, kw, vw, misc, gn, tq, tk):
    B, S, _ = nq.shape
    n_cmp = kcmp.shape[1]
    assert n_cmp <= LANES and S // SEL_BLOCK <= NSA_DH and SEL_BLOCK == 1 << SEL_SHIFT
    assert tq % tk == 0 and tq <= WINDOW and WINDOW % tk == 0
    row = lambda w: pl.BlockSpec((None, tq, w), lambda b, i: (b, i, 0))
    seq = lambda n: pl.BlockSpec((None, n, NSA_KV_WIDTH), lambda b, i: (b, 0, 0))
    vt_scratch = pltpu.VMEM((NSA_KV_HEADS, S // tk, V_AUG, tk), BF16)
    return pl.pallas_call(
        _nsa_kernel,
        grid=(B, S // tq),
        in_specs=[row(NSA_WIDTH), seq(n_cmp), seq(n_cmp), seq(S), seq(S), seq(S), seq(S), row(MISC_W),
                  pl.BlockSpec(gn.shape, lambda b, i: (0, 0))],
        out_specs=row(NSA_WIDTH),
        out_shape=jax.ShapeDtypeStruct((B, S, NSA_WIDTH), BF16),
        scratch_shapes=[pltpu.VMEM((NSA_KV_HEADS, S, 2 * NSA_DH), BF16), vt_scratch, vt_scratch],
        compiler_params=_params("parallel", "arbitrary"),
        name="nsa",
    )(nq, kcmp, vcmp, ks, vs, kw, vw, misc, gn)


def _log_sigmoid(z):
    return jnp.minimum(z, 0.0) - jnp.log(1.0 + jnp.exp(-jnp.abs(z)))


def _gla_tile(q, k, v, g, misc, wa, ba, gn, st):
    C = GLA_CHUNK
    T = q.shape[0]
    n_chunk = T // C
    r = lax.broadcasted_iota(jnp.int32, (T, T), 0)
    c = lax.broadcasted_iota(jnp.int32, (T, T), 1)
    causal = (r >= c) & (r - c <= (r & (C - 1)))
    tri = jnp.where(causal, 1.0, 0.0).astype(BF16)

    z = jnp.dot(misc.astype(BF16), wa.astype(BF16), preferred_element_type=F32) + ba
    la = _log_sigmoid(z) * (1.0 / GLA_TAU)
    yield
    b = jnp.dot(tri, la.astype(BF16), preferred_element_type=F32)
    b_lasts = [b[(ci + 1) * C - 1:(ci + 1) * C, :] for ci in range(n_chunk)]
    b_last = jnp.concatenate([jnp.broadcast_to(bl, (C, bl.shape[1])) for bl in b_lasts], axis=0)
    qf = q.astype(F32)
    kf = k.astype(F32)
    q_in = (qf * GLA_DK ** -0.5 * jnp.exp(b)).astype(BF16)
    k_in = (kf * jnp.exp(-b)).astype(BF16)
    k_dec = (kf * jnp.exp(b_last - b)).astype(BF16)
    pad = jnp.zeros((LANES - n_chunk, b.shape[1]), F32)
    dec_cols = jnp.exp(jnp.concatenate(b_lasts + [pad], axis=0).T)
    yield

    o_intra = []
    for h in range(GLA_HEADS):
        ks = slice(h * GLA_DK, (h + 1) * GLA_DK)
        att = lax.dot_general(q_in[:, ks], k_in[:, ks], _NT, preferred_element_type=F32)
        att = jnp.where(causal, att, 0.0).astype(BF16)
        o_intra.append(jnp.dot(att, v[:, h * GLA_DV:(h + 1) * GLA_DV], preferred_element_type=F32))
        yield

    st = [st[h] for h in range(GLA_HEADS)]
    o_inter = []
    for ci in range(n_chunk):
        rows = slice(ci * C, (ci + 1) * C)
        out = []
        for h in range(GLA_HEADS):
            ks = slice(h * GLA_DK, (h + 1) * GLA_DK)
            vs = slice(h * GLA_DV, (h + 1) * GLA_DV)
            out.append(jnp.dot(q_in[rows, ks], st[h].astype(BF16), preferred_element_type=F32))
            d_st = lax.dot_general(k_dec[rows, ks], v[rows, vs], _TN, preferred_element_type=F32)
            st[h] = st[h] * dec_cols[ks, ci:ci + 1] + d_st
        o_inter.append(jnp.concatenate(out, axis=1))
        yield
    st = jnp.stack(st)
    o_inter = jnp.concatenate(o_inter, axis=0)
    gate = g.astype(F32)
    gate = gate * jax.nn.sigmoid(gate)
    outs = []
    for h in range(GLA_HEADS):
        vs = slice(h * GLA_DV, (h + 1) * GLA_DV)
        outs.append(_rms(o_intra[h] + o_inter[:, vs]) * gn[:, vs] * gate[:, vs])
    return jnp.concatenate(outs, axis=1), st


def _gla_ffn_kernel(x_ref, on_ref, mod_ref, wo_ref, g2_ref, w1_ref, w2_ref, gf_ref,
                    q0_ref, k0_ref, v0_ref, g0_ref, m0_ref, q1_ref, k1_ref, v1_ref, g1_ref, m1_ref,
                    wa_ref, ba_ref, gn_ref, o_ref, st_ref, og_ref, *, ff_chunk, final):
    def gla(q_ref, k_ref, v_ref, g_ref, m_ref, st):
        return _Staged(_gla_tile(q_ref[...], k_ref[...], v_ref[...], g_ref[...], m_ref[...],
                                 wa_ref[...], ba_ref[...], gn_ref[...], st))

    @pl.when((pl.program_id(0) == 0) & (pl.program_id(1) == 0))
    def _():
        o, st = gla(q0_ref, k0_ref, v0_ref, g0_ref, m0_ref, jnp.zeros(st_ref.shape, F32)).finish()
        og_ref[...] = o.astype(BF16)
        st_ref[...] = st

    last = pl.program_id(1) == pl.num_programs(1) - 1
    ahead = gla(q1_ref, k1_ref, v1_ref, g1_ref, m1_ref, jnp.where(last, 0.0, st_ref[...]))

    ahead.step()
    mix = (jnp.dot(og_ref[...], wo_ref[0:GLA_WIDTH, :], preferred_element_type=F32)
           + jnp.dot(on_ref[...], wo_ref[GLA_WIDTH:, :], preferred_element_type=F32))
    x1 = x_ref[...] + mod_ref[2:3, :] * mix
    h = (_rms(x1) * g2_ref[...] * (1.0 + mod_ref[4:5, :]) + mod_ref[3:4, :]).astype(BF16)
    acc = jnp.zeros_like(x1)
    ahead.step()
    for c in range(w1_ref.shape[1] // ff_chunk):
        cs = slice(c * ff_chunk, (c + 1) * ff_chunk)
        a = jnp.maximum(jnp.dot(h, w1_ref[:, cs], preferred_element_type=F32), 0.0)
        ahead.step()
        acc = acc + jnp.dot((a * a).astype(BF16), w2_ref[cs, :], preferred_element_type=F32)
        ahead.step()
    x2 = x1 + mod_ref[5:6, :] * acc
    o_ref[...] = _rms(x2) * gf_ref[...] if final else x2

    o, st = ahead.finish()
    og_ref[...] = o.astype(BF16)
    st_ref[...] = st


def _gla_ffn(x, on, mod, wo, g2, w1, w2, gf, gq, gk, gv, gg, misc, wa_p, ba, gn, final):
    B, S, D = x.shape
    tm = GLA_TILE
    n = S // tm
    row = lambda w: pl.BlockSpec((None, tm, w), lambda b, i: (b, i, 0))
    first = lambda w: pl.BlockSpec((None, tm, w), lambda b, i: (0, 0, 0))
    ahead = lambda w: pl.BlockSpec((None, tm, w), lambda b, i: (
        jnp.where(i + 1 < n, b, jnp.minimum(b + 1, B - 1)), jnp.where(i + 1 < n, i + 1, 0), 0))
    whole = lambda a: pl.BlockSpec(a.shape, lambda b, i: (0,) * a.ndim)
    gla_in = (gq, gk, gv, gg, misc)
    return pl.pallas_call(
        functools.partial(_gla_ffn_kernel, ff_chunk=1024, final=final),
        grid=(B, n),
        in_specs=([row(D), row(on.shape[2]), pl.BlockSpec((None, N_MOD, D), lambda b, i: (b, 0, 0)),
                   whole(wo), whole(g2), whole(w1), whole(w2), whole(gf)]
                  + [first(a.shape[2]) for a in gla_in] + [ahead(a.shape[2]) for a in gla_in]
                  + [whole(wa_p), whole(ba), whole(gn)]),
        out_specs=row(D),
        out_shape=jax.ShapeDtypeStruct((B, S, D), F32),
        scratch_shapes=[pltpu.VMEM((GLA_HEADS, GLA_DK, GLA_DV), F32), pltpu.VMEM((tm, GLA_WIDTH), BF16)],
        compiler_params=_params("arbitrary", "arbitrary"),
        name="gla_ffn",
    )(x, on, mod, wo, g2, w1, w2, gf, *gla_in, *gla_in, wa_p, ba, gn)


def kernel(x, c, positions, w_ada, b_ada, norm1_g, w_in, gla_w_a2, gla_b_a, gla_norm_g, nsa_pe_k, nsa_pe_v, cmp_k_w1, cmp_k_w2, cmp_v_w1, cmp_v_w2, nsa_norm_g, w_o, norm2_g, w_ff1, w_ff2, final_norm_g):
    B, S, D = x.shape
    depth = w_in.shape[0]
    half = NSA_DH // 2
    inv = ROPE_THETA ** (-jnp.arange(half, dtype=F32) / half)
    inv = jnp.concatenate([inv, inv]).reshape(1, NSA_DH)
    for l in range(depth):
        mod = _adaln(c, w_ada[l], b_ada[l]).reshape(B, N_MOD, D)
        (gq, gk, gv, gg, nq, kc, vc, ks, vs, kw, vw, misc) = _in_proj(
            x, mod, norm1_g[l].reshape(1, D), positions, inv, w_in, l, tm=512)

        k_cmp, v_cmp = _compress(kc, vc, cmp_k_w1[l].astype(BF16), _compress_pe(nsa_pe_k[l]),
                                 cmp_k_w2[l].astype(BF16), cmp_v_w1[l].astype(BF16),
                                 _compress_pe(nsa_pe_v[l]), cmp_v_w2[l].astype(BF16), nb=int(np.gcd(B, 4)))
        o_nsa = _nsa(nq, k_cmp, v_cmp, ks, vs, kw, vw, misc, nsa_norm_g[l].reshape(1, NSA_WIDTH),
                     tq=512, tk=128)

        wa_p = jnp.zeros((MISC_W, GLA_QK), F32).at[0:GLA_RANK].set(gla_w_a2[l])
        x = _gla_ffn(x, o_nsa, mod, w_o[l].astype(BF16), norm2_g[l].reshape(1, D),
                     w_ff1[l].astype(BF16), w_ff2[l].astype(BF16), final_norm_g.reshape(1, D),
                     gq, gk, gv, gg, misc, wa_p, gla_b_a[l].reshape(1, GLA_QK),
                     gla_norm_g[l].reshape(1, GLA_WIDTH), final=(l == depth - 1))
    return x
```

```python
import functools

import numpy as np
import jax
import jax.numpy as jnp
from jax import lax
from jax.experimental import pallas as pl
from jax.experimental.pallas import tpu as pltpu

GLA_HEADS = 4
GLA_DK = 64
GLA_DV = 128
GLA_RANK = 16
GLA_TAU = 16.0
GLA_CHUNK = 64
NSA_HEADS = 4
NSA_KV_HEADS = 2
NSA_DH = 128
NSA_BRANCHES = 3
CMP_BLOCK = 32
CMP_STRIDE = 16
CMP_HIDDEN = 256
SEL_BLOCK = 64
SEL_TOPK = 16
WINDOW = 512
N_MOD = 6
ROPE_THETA = 10000.0
EPS = 1e-6
NEG = -1e30
BIG = 1e30

GLA_QK = GLA_HEADS * GLA_DK
GLA_WIDTH = GLA_HEADS * GLA_DV
NSA_WIDTH = NSA_HEADS * NSA_DH
NSA_KV_WIDTH = NSA_KV_HEADS * NSA_DH
N_GATE = NSA_HEADS * NSA_BRANCHES
IN_SPLITS = (GLA_QK, GLA_QK, GLA_WIDTH, GLA_WIDTH, GLA_RANK, NSA_WIDTH) + (NSA_KV_WIDTH,) * 6 + (N_GATE,)

LANES = 128
SUBLANES = 8
MISC_W = LANES
GATE_OFF = GLA_RANK
GLA_TILE = 256
VMEM_LIMIT = 56 * 1024 * 1024

F32 = jnp.float32
BF16 = jnp.bfloat16

_NT = (((1,), (1,)), ((), ()))
_TN = (((0,), (0,)), ((), ()))


def _params(*sem):
    return pltpu.CompilerParams(dimension_semantics=sem, vmem_limit_bytes=VMEM_LIMIT)


def _rms(x):
    return x * lax.rsqrt(jnp.mean(x * x, axis=-1, keepdims=True) + EPS)


def _split_bf16(x):
    hi = x.astype(BF16)
    return hi, (x - hi.astype(F32)).astype(BF16)


def _t_f32(x):
    return x.astype(F32).T


class _Staged:
    def __init__(self, gen):
        self.gen, self.value, self.done = gen, None, False

    def step(self):
        if not self.done:
            try:
                next(self.gen)
            except StopIteration as stop:
                self.value, self.done = stop.value, True

    def finish(self):
        while not self.done:
            self.step()
        return self.value


def _adaln_kernel(c_ref, w_ref, b_ref, o_ref):
    c = c_ref[...]
    n = c.shape[0]
    a_hi, a_lo = _split_bf16(c * jax.nn.sigmoid(c))
    w_hi, w_lo = _split_bf16(w_ref[...])
    y = jnp.dot(jnp.concatenate([a_hi, a_lo], axis=0), w_hi, preferred_element_type=F32)
    o_ref[...] = y[0:n] + y[n:2 * n] + jnp.dot(a_hi, w_lo, preferred_element_type=F32) + b_ref[...]


def _adaln(c, w_ada, b_ada):
    B, D = c.shape
    N = w_ada.shape[1]
    tn = D
    return pl.pallas_call(
        _adaln_kernel,
        grid=(N // tn,),
        in_specs=[pl.BlockSpec((B, D), lambda j: (0, 0)),
                  pl.BlockSpec((D, tn), lambda j: (0, j)),
                  pl.BlockSpec((1, tn), lambda j: (0, j))],
        out_specs=pl.BlockSpec((B, tn), lambda j: (0, j)),
        out_shape=jax.ShapeDtypeStruct((B, N), F32),
        compiler_params=_params("arbitrary"),
        name="adaln",
    )(c, w_ada, b_ada.reshape(1, N))


_PROJ_GROUPS = (("gq", GLA_QK, False), ("gk", GLA_QK, False), ("gv", GLA_WIDTH, False),
                ("gg", GLA_WIDTH, False), ("nq", NSA_WIDTH, True), ("kc", NSA_KV_WIDTH, True),
                ("vc", NSA_KV_WIDTH, False), ("ks", NSA_KV_WIDTH, True), ("vs", NSA_KV_WIDTH, False),
                ("kw", NSA_KV_WIDTH, True), ("vw", NSA_KV_WIDTH, False))
_PROJ_W = sum(g[1] for g in _PROJ_GROUPS) + MISC_W


def _w_in_moves():
    names = ("gq", "gk", "gv", "gg", "gr", "nq", "kc", "vc", "ks", "vs", "kw", "vw", "ngate")
    src = dict(zip(names, np.cumsum((0,) + IN_SPLITS[:-1])))
    width = dict(zip(names, IN_SPLITS))
    moves, dst = [], 0
    for name, w, _ in _PROJ_GROUPS:
        moves.append((int(src[name]), dst, w))
        dst += w
    for name in ("gr", "ngate"):
        moves.append((int(src[name]), dst, width[name]))
        dst += width[name]
    return tuple(moves)


_W_IN_MOVES = _w_in_moves()
_GROUPED = ("kc", "vc")
GRP_W = CMP_STRIDE * NSA_DH


def _in_proj_kernel(x_ref, mod_ref, g_ref, pos_ref, inv_ref, win_ref, *refs):
    n_out = len(_PROJ_GROUPS) + 1
    out_refs, stage_refs, w_ref = refs[:n_out], refs[n_out:-1], refs[-1]

    @pl.when((pl.program_id(0) == 0) & (pl.program_id(1) == 0))
    def _():
        for src, dst, width in _W_IN_MOVES:
            w_ref[:, dst:dst + width] = win_ref[:, src:src + width].astype(BF16)
        pad = _PROJ_W - _W_IN_MOVES[-1][1] - _W_IN_MOVES[-1][2]
        w_ref[:, _PROJ_W - pad:] = jnp.zeros((w_ref.shape[0], pad), BF16)

    tm = x_ref.shape[0]
    x = x_ref[...]
    h = _rms(x) * g_ref[...] * (1.0 + mod_ref[1:2, :]) + mod_ref[0:1, :]
    hb = h.astype(BF16)

    qs = NSA_DH ** -0.5 * float(np.log2(np.e))
    cos = sin = None

    off = 0
    for (name, width, rot), o_ref in zip(_PROJ_GROUPS, out_refs[:-1]):
        if rot and cos is None:
            pos = pos_ref[...].astype(F32)
            pos = jnp.concatenate([jnp.broadcast_to(pos[r:r + 1, :], (LANES, LANES)).T
                                   for r in range(pos.shape[0])], axis=0)
            lane = lax.broadcasted_iota(jnp.int32, (1, LANES), 1)
            low = lane < NSA_DH // 2
            ang = jnp.where(low, pos[0:tm // 2], pos[tm // 2:tm]) * inv_ref[...]
            c2, s2 = jnp.cos(ang), jnp.sin(ang)
            c2r, s2r = pltpu.roll(c2, NSA_DH // 2, axis=1), pltpu.roll(s2, NSA_DH // 2, axis=1)
            cos = jnp.concatenate([jnp.where(low, c2, c2r), jnp.where(low, c2r, c2)], axis=0)
            sin = jnp.concatenate([jnp.where(low, s2, s2r), jnp.where(low, s2r, s2)], axis=0)
            sin = sin * jnp.where(low, -1.0, 1.0)
        y = jnp.dot(hb, w_ref[:, off:off + width], preferred_element_type=F32)
        grouped = name in _GROUPED
        dst = stage_refs[_GROUPED.index(name)] if grouped else o_ref
        for hd in range(width // NSA_DH if (rot or grouped) else 0):
            yh = y[:, hd * NSA_DH:(hd + 1) * NSA_DH]
            if rot:
                c, s = (cos * qs, sin * qs) if name == "nq" else (cos, sin)
                yh = yh * c + pltpu.roll(yh, NSA_DH // 2, axis=1) * s
            if grouped:
                dst[hd] = yh
            else:
                dst[:, hd * NSA_DH:(hd + 1) * NSA_DH] = yh.astype(dst.dtype)
        if not (rot or grouped):
            dst[...] = y.astype(dst.dtype)
        if grouped:
            for k in range(NSA_KV_HEADS):
                for tok in range(CMP_STRIDE):
                    c0 = k * GRP_W + tok * NSA_DH
                    o_ref[:, c0:c0 + NSA_DH] = dst[k, pl.ds(tok, tm // CMP_STRIDE, stride=CMP_STRIDE), :].astype(
                        o_ref.dtype)
        off += width
    out_refs[-1][...] = jnp.dot(hb, w_ref[:, off:off + MISC_W], preferred_element_type=F32)


def _in_proj(x, mod, norm_g, positions, inv, w_in, layer, tm):
    B, S, D = x.shape
    pos4 = positions.reshape(B, S // tm, tm // LANES, LANES)
    row = lambda w: pl.BlockSpec((None, tm, w), lambda b, i: (b, i, 0))
    whole = lambda a: pl.BlockSpec(a.shape, lambda b, i: (0,) * a.ndim)
    out_shape, out_specs = [], []
    for name, w, _ in _PROJ_GROUPS:
        if name in _GROUPED:
            out_shape.append(jax.ShapeDtypeStruct((B, S // CMP_STRIDE, CMP_STRIDE * w), BF16))
            out_specs.append(pl.BlockSpec((None, tm // CMP_STRIDE, CMP_STRIDE * w), lambda b, i: (b, i, 0)))
        else:
            out_shape.append(jax.ShapeDtypeStruct((B, S, w), BF16))
            out_specs.append(row(w))
    out_shape.append(jax.ShapeDtypeStruct((B, S, MISC_W), F32))
    out_specs.append(row(MISC_W))
    return pl.pallas_call(
        _in_proj_kernel,
        grid=(B, S // tm),
        in_specs=[row(D), pl.BlockSpec((None, N_MOD, D), lambda b, i: (b, 0, 0)), whole(norm_g),
                  pl.BlockSpec((None, None, tm // LANES, LANES), lambda b, i: (b, i, 0, 0)), whole(inv),
                  pl.BlockSpec((None,) + w_in.shape[1:], lambda b, i: (layer, 0, 0),
                               pipeline_mode=pl.Buffered(1))],
        out_specs=out_specs,
        out_shape=out_shape,
        scratch_shapes=([pltpu.VMEM((NSA_KV_HEADS, tm, NSA_DH), F32) for _ in _GROUPED]
                        + [pltpu.VMEM((D, _PROJ_W), BF16)]),
        compiler_params=_params("arbitrary", "arbitrary"),
        name="in_proj",
    )(x, mod, norm_g, pos4, inv, w_in)


def _compress_kernel(xk_ref, xv_ref, kw1_ref, kpe_ref, kw2_ref, vw1_ref, vpe_ref, vw2_ref, ok_ref, ov_ref):
    nb, n_grp, _ = xk_ref.shape
    rows = nb * n_grp
    for x_ref, w1_ref, pe_ref, w2_ref, o_ref in ((xk_ref, kw1_ref, kpe_ref, kw2_ref, ok_ref),
                                                 (xv_ref, vw1_ref, vpe_ref, vw2_ref, ov_ref)):
        w1a = w1_ref[0:GRP_W, :]
        w1b = w1_ref[GRP_W:2 * GRP_W, :]
        x = jnp.concatenate([x_ref[:, :, k * GRP_W:(k + 1) * GRP_W].reshape(rows, GRP_W)
                             for k in range(NSA_KV_HEADS)], axis=0)
        bias = (jnp.dot(pe_ref[:, 0:GRP_W], w1a, preferred_element_type=F32)[0:1]
                + jnp.dot(pe_ref[:, GRP_W:2 * GRP_W], w1b, preferred_element_type=F32)[0:1])
        ua = jnp.dot(x, w1a, preferred_element_type=F32)
        ub = jnp.dot(x, w1b, preferred_element_type=F32)
        hid = ua + pltpu.roll(ub, NSA_KV_HEADS * rows - 1, axis=0) + bias
        act = jax.nn.gelu(hid, approximate=True).astype(BF16)
        y = jnp.dot(act, w2_ref[...], preferred_element_type=F32).astype(o_ref.dtype)
        for k in range(NSA_KV_HEADS):
            o_ref[:, :, k * NSA_DH:(k + 1) * NSA_DH] = y[k * rows:(k + 1) * rows].reshape(nb, n_grp, NSA_DH)


def _compress(xk, xv, kw1, kpe, kw2, vw1, vpe, vw2, nb):
    B, n_grp, gw = xk.shape
    W = NSA_KV_WIDTH
    whole = lambda a: pl.BlockSpec(a.shape, lambda b: (0,) * a.ndim)
    xspec = pl.BlockSpec((nb, n_grp, gw), lambda b: (b, 0, 0))
    ospec = pl.BlockSpec((nb, n_grp, W), lambda b: (b, 0, 0))
    return pl.pallas_call(
        _compress_kernel,
        grid=(B // nb,),
        in_specs=[xspec, xspec, whole(kw1), whole(kpe), whole(kw2), whole(vw1), whole(vpe), whole(vw2)],
        out_specs=[ospec, ospec],
        out_shape=[jax.ShapeDtypeStruct((B, n_grp, W), BF16)] * 2,
        compiler_params=_params("parallel"),
        name="compress",
    )(xk, xv, kw1, kpe, kw2, vw1, vpe, vw2)


def _compress_pe(pe):
    return jnp.zeros((SUBLANES, 2 * GRP_W), F32).at[0].set(pe.reshape(2 * GRP_W)).astype(BF16)


V_AUG = NSA_DH + 2 * SUBLANES
SEL_SHIFT = SEL_BLOCK.bit_length() - 1


def _attend(k_rows, qt, vt_aug, masked):
    s = jnp.dot(k_rows(), qt, preferred_element_type=F32)
    if masked is not None:
        s = s + masked()
    yield
    m = jnp.max(s, axis=0, keepdims=True)
    p = jnp.exp2((s - m).astype(BF16))
    yield
    return m, jnp.dot(vt_aug(), p, preferred_element_type=F32)


def _pipelined(gens, depth=1, side=()):
    tasks = [_Staged(g) for g in gens]
    for n in range(len(tasks) + depth):
        for t in tasks[max(0, n - depth):n + 1]:
            t.step()
        for s in side:
            s.step()
    return [t.finish() for t in tasks]


def _merge(parts):
    m_all = parts[0][0]
    for m, _ in parts[1:]:
        m_all = jnp.maximum(m_all, m)
    tot = None
    for m, acc in parts:
        w = acc * jnp.exp2(m - m_all)
        tot = w if tot is None else tot + w
    return tot[0:NSA_DH] / tot[NSA_DH:NSA_DH + 1]


def _nsa_step(c, tq, tk, q_ref, kc_ref, vc_ref, kw_ref, misc_ref, gn_ref, o_ref, kaug_ref, vst_ref, vwt_ref):
    S = kw_ref.shape[0]
    n_cmp = kc_ref.shape[0]
    n_sel = S // SEL_BLOCK
    G = NSA_HEADS // NSA_KV_HEADS
    M = G * tq
    q0 = c * tq

    t_q = q0 + lax.broadcasted_iota(jnp.int32, (1, tq), 1)
    t_m = jnp.concatenate([t_q] * G, axis=1)
    key_off = lax.broadcasted_iota(jnp.int32, (tk, 1), 0)

    n_col = lax.broadcasted_iota(jnp.int32, (n_cmp, 1), 0)
    cmp_end = jnp.where(n_col < n_cmp - 1, n_col * CMP_STRIDE + (CMP_BLOCK - 1), jnp.int32(2 ** 30))
    cmp_valid = cmp_end <= t_m
    any_valid = (t_m >= CMP_BLOCK - 1).astype(F32)
    jj = lax.broadcasted_iota(jnp.int32, (n_sel, n_cmp), 0) * SEL_BLOCK
    nn = lax.broadcasted_iota(jnp.int32, (n_sel, n_cmp), 1) * CMP_STRIDE
    ov_t = jnp.maximum(jnp.minimum(nn + CMP_BLOCK, jj + SEL_BLOCK) - jnp.maximum(nn, jj), 0)
    ov_t = (ov_t.astype(F32) * (1.0 / CMP_BLOCK)).astype(BF16)
    j_row = lax.broadcasted_iota(jnp.int32, (n_sel, tq), 0)
    blk_t = lax.shift_right_logical(q0 + lax.broadcasted_iota(jnp.int32, (n_sel, tq), 1), SEL_SHIFT)
    forced = (j_row == 0) | (j_row == blk_t) | (j_row == blk_t - 1)
    in_past = j_row <= blk_t

    heads = range(NSA_KV_HEADS)
    qts = [jnp.concatenate([_t_f32(q_ref[:, (k * G + g) * NSA_DH:(k * G + g + 1) * NSA_DH])
                            for g in range(G)], axis=1).astype(BF16) for k in heads]

    def select(k):
        hs = slice(k * NSA_DH, (k + 1) * NSA_DH)
        s = jnp.dot(kc_ref[:, hs], qts[k], preferred_element_type=F32)
        s = jnp.where(cmp_valid, s, NEG)
        yield
        e = jnp.exp2(s - jnp.max(s, axis=0, keepdims=True))
        p = e / jnp.sum(e, axis=0, keepdims=True) * any_valid
        yield
        o_cmp = jnp.dot(_t_f32(vc_ref[:, hs]).astype(BF16), p.astype(BF16), preferred_element_type=F32)
        p_grp = p[:, 0:tq]
        for g in range(1, G):
            p_grp = p_grp + p[:, g * tq:(g + 1) * tq]
        p_hi, p_lo = _split_bf16(p_grp)
        imp = (jnp.dot(ov_t, p_hi, preferred_element_type=F32)
               + jnp.dot(ov_t, p_lo, preferred_element_type=F32))
        imp = jnp.where(forced, BIG, jnp.where(in_past, imp, NEG))
        yield
        rank = []
        for r0 in range(0, n_sel, SUBLANES):
            blk = imp[r0:r0 + SUBLANES, :]
            cnt = jnp.zeros(blk.shape, F32)
            for j in range(n_sel):
                row = imp[j:j + 1, :]
                if j < r0:
                    cnt = cnt + jnp.where(row >= blk, 1.0, 0.0)
                elif j >= r0 + SUBLANES - 1:
                    cnt = cnt + jnp.where(row > blk, 1.0, 0.0)
                else:
                    below = r0 + lax.broadcasted_iota(jnp.int32, blk.shape, 0) > j
                    cnt = cnt + jnp.where(below, jnp.where(row >= blk, 1.0, 0.0), jnp.where(row > blk, 1.0, 0.0))
            rank.append(cnt)
            yield
        rank = jnp.concatenate(rank, axis=0)
        bias = jnp.where(rank < float(min(SEL_TOPK, n_sel)), 0.0, NEG).astype(BF16)
        return o_cmp, jnp.concatenate([qts[k], jnp.concatenate([bias] * G, axis=1),
                                       jnp.zeros((NSA_DH - n_sel, M), BF16)], axis=0)

    subs = range(tq // tk)
    col_off = jnp.concatenate([lax.broadcasted_iota(jnp.int32, (1, tk), 1)] * G, axis=1)
    after_bias = jnp.where(key_off > col_off, NEG, 0.0)
    far_bias = jnp.where(key_off <= col_off, NEG, 0.0)
    after = lambda: after_bias
    too_far = lambda: far_bias

    def cols(x, j):
        return jnp.concatenate([x[:, g * tq + j * tk:g * tq + (j + 1) * tk] for g in range(G)], axis=1)

    def sel_tasks(k, j, qta):
        first, diag = (0, q0 // tk - 1) if j is None else (q0 // tk, (q0 + j * tk) // tk)
        for kt in range(first, diag + 1):
            yield _attend(lambda kt=kt: kaug_ref[k, kt * tk:(kt + 1) * tk, :], qta,
                          lambda kt=kt: vst_ref[k, kt], after if kt == diag and j is not None else None)

    def win_tasks(k, j, qt):
        hs = slice(k * NSA_DH, (k + 1) * NSA_DH)
        diag = (q0 + j * tk) // tk
        back = WINDOW // tk
        for kt in range(max(diag - back, 0), diag + 1):
            masked = after if kt == diag else too_far if kt == diag - back else None
            yield _attend(lambda kt=kt: kw_ref[kt * tk:(kt + 1) * tk, hs], qt, lambda kt=kt: vwt_ref[k, kt], masked)

    def run(task_lists, side=()):
        flat = [(key, t) for key, tasks in task_lists.items() for t in tasks]
        parts = {key: [] for key in task_lists}
        for (key, _), part in zip(flat, _pipelined([t for _, t in flat], side=side)):
            parts[key].append(part)
        merged = {}
        for k in heads:
            for j in subs:
                shared = [(cols(m, j), cols(acc, j)) for m, acc in parts.get((k, None), [])]
                merged[k, j] = _merge(shared + parts[k, j])
        return [jnp.concatenate([merged[k, j][:, g * tk:(g + 1) * tk] for g in range(G) for j in subs], axis=1)
                for k in heads]

    selects = [_Staged(select(k)) for k in heads]
    o_win = run({(k, j): list(win_tasks(k, j, cols(qts[k], j))) for k in heads for j in subs}, side=selects)
    o_cmp, qtas = zip(*[s.finish() for s in selects])
    o_sel = run({(k, j): list(sel_tasks(k, j, qtas[k] if j is None else cols(qtas[k], j)))
                 for k in heads for j in (None, *subs)})


    gates_t = _t_f32(jax.nn.sigmoid(misc_ref[...]))
    outs = []
    for k in heads:
        for g in range(G):
            gs = slice(g * tq, (g + 1) * tq)
            gl = GATE_OFF + (k * G + g) * NSA_BRANCHES
            outs.append(gates_t[gl:gl + 1, :] * o_cmp[k][:, gs] + gates_t[gl + 1:gl + 2, :] * o_sel[k][:, gs]
                         + gates_t[gl + 2:gl + 3, :] * o_win[k][:, gs])
    ssq = outs[0] * outs[0]
    for o in outs[1:]:
        ssq = ssq + o * o
    inv = lax.rsqrt(jnp.sum(ssq, axis=0, keepdims=True) * (1.0 / NSA_WIDTH) + EPS)
    for hq, o in enumerate(outs):
        cs = slice(hq * NSA_DH, (hq + 1) * NSA_DH)
        o_ref[:, cs] = ((o * inv).T * gn_ref[:, cs]).astype(o_ref.dtype)


def _nsa_kernel(q_ref, kc_ref, vc_ref, ks_ref, vs_ref, kw_ref, vw_ref, misc_ref, gn_ref, o_ref,
                kaug_ref, vst_ref, vwt_ref):
    tq = q_ref.shape[0]
    tk = vst_ref.shape[3]
    S = ks_ref.shape[0]
    i = pl.program_id(1)

    @pl.when(i == 0)
    def _():
        pos = lax.broadcasted_iota(jnp.int32, (S, LANES), 0)
        lane = lax.broadcasted_iota(jnp.int32, (S, LANES), 1)
        onehot = jnp.where(lax.shift_right_logical(pos, SEL_SHIFT) == lane, 1.0, 0.0).astype(BF16)
        row = lax.broadcasted_iota(jnp.int32, (V_AUG - NSA_DH, tk), 0)
        ones_rows = jnp.where(row == 0, 1.0, 0.0).astype(BF16)
        for k in range(NSA_KV_HEADS):
            hs = slice(k * NSA_DH, (k + 1) * NSA_DH)
            kaug_ref[k, :, 0:NSA_DH] = ks_ref[:, hs]
            kaug_ref[k, :, NSA_DH:2 * NSA_DH] = onehot
            for kt in range(S // tk):
                rows = slice(kt * tk, (kt + 1) * tk)
                for src, dst in ((vs_ref, vst_ref), (vw_ref, vwt_ref)):
                    dst[k, kt, 0:NSA_DH, :] = _t_f32(src[rows, hs]).astype(BF16)
                    dst[k, kt, NSA_DH:V_AUG, :] = ones_rows

    for c in range(S // tq):
        pl.when(i == c)(functools.partial(_nsa_step, c, tq, tk, q_ref, kc_ref, vc_ref, kw_ref, misc_ref,
                                          gn_ref, o_ref, kaug_ref, vst_ref, vwt_ref))


def _nsa(nq, kcmp, vcmp, ks, vs, kw, vw, misc, gn, tq, tk):
    B, S, _ = nq.shape
    n_cmp = kcmp.shape[1]
    assert n_cmp <= LANES and S // SEL_BLOCK <= NSA_DH and SEL_BLOCK == 1 << SEL_SHIFT
    assert tq % tk == 0 and tk <= WINDOW and WINDOW % tk == 0
    row = lambda w: pl.BlockSpec((None, tq, w), lambda b, i: (b, i, 0))
    seq = lambda n: pl.BlockSpec((None, n, NSA_KV_WIDTH), lambda b, i: (b, 0, 0))
    vt_scratch = pltpu.VMEM((NSA_KV_HEADS, S // tk, V_AUG, tk), BF16)
    return pl.pallas_call(
        _nsa_kernel,
        grid=(B, S // tq),
        in_specs=[row(NSA_WIDTH), seq(n_cmp), seq(n_cmp), seq(S), seq(S), seq(S), seq(S), row(MISC_W),
                  pl.BlockSpec(gn.shape, lambda b, i: (0, 0))],
        out_specs=row(NSA_WIDTH),
        out_shape=jax.ShapeDtypeStruct((B, S, NSA_WIDTH), BF16),
        scratch_shapes=[pltpu.VMEM((NSA_KV_HEADS, S, 2 * NSA_DH), BF16), vt_scratch, vt_scratch],
        compiler_params=_params("parallel", "arbitrary"),
        name="nsa",
    )(nq, kcmp, vcmp, ks, vs, kw, vw, misc, gn)


def _log_sigmoid(z):
    return jnp.minimum(z, 0.0) - jnp.log(1.0 + jnp.exp(-jnp.abs(z)))


def _gla_tile(q, k, v, g, misc, wa, ba, gn, st):
    C = GLA_CHUNK
    T = q.shape[0]
    n_chunk = T // C
    r = lax.broadcasted_iota(jnp.int32, (T, T), 0)
    c = lax.broadcasted_iota(jnp.int32, (T, T), 1)
    causal = (r >= c) & (r - c <= (r & (C - 1)))
    tri = jnp.where(causal, 1.0, 0.0).astype(BF16)

    z = jnp.dot(misc.astype(BF16), wa.astype(BF16), preferred_element_type=F32) + ba
    la = _log_sigmoid(z) * (1.0 / GLA_TAU)
    yield
    b = jnp.dot(tri, la.astype(BF16), preferred_element_type=F32)
    b_lasts = [b[(ci + 1) * C - 1:(ci + 1) * C, :] for ci in range(n_chunk)]
    b_last = jnp.concatenate([jnp.broadcast_to(bl, (C, bl.shape[1])) for bl in b_lasts], axis=0)
    qf = q.astype(F32)
    kf = k.astype(F32)
    q_in = (qf * GLA_DK ** -0.5 * jnp.exp(b)).astype(BF16)
    k_in = (kf * jnp.exp(-b)).astype(BF16)
    k_dec = (kf * jnp.exp(b_last - b)).astype(BF16)
    pad = jnp.zeros((LANES - n_chunk, b.shape[1]), F32)
    dec_cols = jnp.exp(jnp.concatenate(b_lasts + [pad], axis=0).T)
    yield

    o_intra = []
    for h in range(GLA_HEADS):
        ks = slice(h * GLA_DK, (h + 1) * GLA_DK)
        att = lax.dot_general(q_in[:, ks], k_in[:, ks], _NT, preferred_element_type=F32)
        att = jnp.where(causal, att, 0.0).astype(BF16)
        o_intra.append(jnp.dot(att, v[:, h * GLA_DV:(h + 1) * GLA_DV], preferred_element_type=F32))
        yield

    st = [st[h] for h in range(GLA_HEADS)]
    o_inter = []
    for ci in range(n_chunk):
        rows = slice(ci * C, (ci + 1) * C)
        out = []
        for h in range(GLA_HEADS):
            ks = slice(h * GLA_DK, (h + 1) * GLA_DK)
            vs = slice(h * GLA_DV, (h + 1) * GLA_DV)
            out.append(jnp.dot(q_in[rows, ks], st[h].astype(BF16), preferred_element_type=F32))
            d_st = lax.dot_general(k_dec[rows, ks], v[rows, vs], _TN, preferred_element_type=F32)
            st[h] = st[h] * dec_cols[ks, ci:ci + 1] + d_st
        o_inter.append(jnp.concatenate(out, axis=1))
        yield
    st = jnp.stack(st)
    o_inter = jnp.concatenate(o_inter, axis=0)
    gate = g.astype(F32)
    gate = gate * jax.nn.sigmoid(gate)
    outs = []
    for h in range(GLA_HEADS):
        vs = slice(h * GLA_DV, (h + 1) * GLA_DV)
        outs.append(_rms(o_intra[h] + o_inter[:, vs]) * gn[:, vs] * gate[:, vs])
    return jnp.concatenate(outs, axis=1), st


def _gla_ffn_kernel(x_ref, on_ref, mod_ref, wo_ref, g2_ref, w1_ref, w2_ref, gf_ref,
                    q0_ref, k0_ref, v0_ref, g0_ref, m0_ref, q1_ref, k1_ref, v1_ref, g1_ref, m1_ref,
                    wa_ref, ba_ref, gn_ref, o_ref, st_ref, og_ref, *, ff_chunk, final):
    def gla(q_ref, k_ref, v_ref, g_ref, m_ref, st):
        return _Staged(_gla_tile(q_ref[...], k_ref[...], v_ref[...], g_ref[...], m_ref[...],
                                 wa_ref[...], ba_ref[...], gn_ref[...], st))

    @pl.when((pl.program_id(0) == 0) & (pl.program_id(1) == 0))
    def _():
        o, st = gla(q0_ref, k0_ref, v0_ref, g0_ref, m0_ref, jnp.zeros(st_ref.shape, F32)).finish()
        og_ref[...] = o.astype(BF16)
        st_ref[...] = st

    last = pl.program_id(1) == pl.num_programs(1) - 1
    ahead = gla(q1_ref, k1_ref, v1_ref, g1_ref, m1_ref, jnp.where(last, 0.0, st_ref[...]))

    ahead.step()
    mix = (jnp.dot(og_ref[...], wo_ref[0:GLA_WIDTH, :], preferred_element_type=F32)
           + jnp.dot(on_ref[...], wo_ref[GLA_WIDTH:, :], preferred_element_type=F32))
    x1 = x_ref[...] + mod_ref[2:3, :] * mix
    h = (_rms(x1) * g2_ref[...] * (1.0 + mod_ref[4:5, :]) + mod_ref[3:4, :]).astype(BF16)
    acc = jnp.zeros_like(x1)
    ahead.step()
    for c in range(w1_ref.shape[1] // ff_chunk):
        cs = slice(c * ff_chunk, (c + 1) * ff_chunk)
        a = jnp.maximum(jnp.dot(h, w1_ref[:, cs], preferred_element_type=F32), 0.0)
        ahead.step()
        acc = acc + jnp.dot((a * a).astype(BF16), w2_ref[cs, :], preferred_element_type=F32)
        ahead.step()
    x2 = x1 + mod_ref[5:6, :] * acc
    o_ref[...] = _rms(x2) * gf_ref[...] if final else x2

    o, st = ahead.finish()
    og_ref[...] = o.astype(BF16)
    st_ref[...] = st


def _gla_ffn(x, on, mod, wo, g2, w1, w2, gf, gq, gk, gv, gg, misc, wa_p, ba, gn, final):
    B, S, D = x.shape
    tm = GLA_TILE
    n = S // tm
    row = lambda w: pl.BlockSpec((None, tm, w), lambda b, i: (b, i, 0))
    first = lambda w: pl.BlockSpec((None, tm, w), lambda b, i: (0, 0, 0))
    ahead = lambda w: pl.BlockSpec((None, tm, w), lambda b, i: (
        jnp.where(i + 1 < n, b, jnp.minimum(b + 1, B - 1)), jnp.where(i + 1 < n, i + 1, 0), 0))
    whole = lambda a: pl.BlockSpec(a.shape, lambda b, i: (0,) * a.ndim)
    gla_in = (gq, gk, gv, gg, misc)
    return pl.pallas_call(
        functools.partial(_gla_ffn_kernel, ff_chunk=1024, final=final),
        grid=(B, n),
        in_specs=([row(D), row(on.shape[2]), pl.BlockSpec((None, N_MOD, D), lambda b, i: (b, 0, 0)),
                   whole(wo), whole(g2), whole(w1), whole(w2), whole(gf)]
                  + [first(a.shape[2]) for a in gla_in] + [ahead(a.shape[2]) for a in gla_in]
                  + [whole(wa_p), whole(ba), whole(gn)]),
        out_specs=row(D),
        out_shape=jax.ShapeDtypeStruct((B, S, D), F32),
        scratch_shapes=[pltpu.VMEM((GLA_HEADS, GLA_DK, GLA_DV), F32), pltpu.VMEM((tm, GLA_WIDTH), BF16)],
        compiler_params=_params("arbitrary", "arbitrary"),
        name="gla_ffn",
    )(x, on, mod, wo, g2, w1, w2, gf, *gla_in, *gla_in, wa_p, ba, gn)


def kernel(x, c, positions, w_ada, b_ada, norm1_g, w_in, gla_w_a2, gla_b_a, gla_norm_g, nsa_pe_k, nsa_pe_v, cmp_k_w1, cmp_k_w2, cmp_v_w1, cmp_v_w2, nsa_norm_g, w_o, norm2_g, w_ff1, w_ff2, final_norm_g):
    B, S, D = x.shape
    depth = w_in.shape[0]
    half = NSA_DH // 2
    inv = ROPE_THETA ** (-jnp.arange(half, dtype=F32) / half)
    inv = jnp.concatenate([inv, inv]).reshape(1, NSA_DH)
    for l in range(depth):
        mod = _adaln(c, w_ada[l], b_ada[l]).reshape(B, N_MOD, D)
        (gq, gk, gv, gg, nq, kc, vc, ks, vs, kw, vw, misc) = _in_proj(
            x, mod, norm1_g[l].reshape(1, D), positions, inv, w_in, l, tm=512)

        k_cmp, v_cmp = _compress(kc, vc, cmp_k_w1[l].astype(BF16), _compress_pe(nsa_pe_k[l]),
                                 cmp_k_w2[l].astype(BF16), cmp_v_w1[l].astype(BF16),
                                 _compress_pe(nsa_pe_v[l]), cmp_v_w2[l].astype(BF16), nb=int(np.gcd(B, 4)))
        o_nsa = _nsa(nq, k_cmp, v_cmp, ks, vs, kw, vw, misc, nsa_norm_g[l].reshape(1, NSA_WIDTH),
                     tq=1024, tk=128)

        wa_p = jnp.zeros((MISC_W, GLA_QK), F32).at[0:GLA_RANK].set(gla_w_a2[l])
        x = _gla_ffn(x, o_nsa, mod, w_o[l].astype(BF16), norm2_g[l].reshape(1, D),
                     w_ff1[l].astype(BF16), w_ff2[l].astype(BF16), final_norm_g.reshape(1, D),
                     gq, gk, gv, gg, misc, wa_p, gla_b_a[l].reshape(1, GLA_QK),
                     gla_norm_g[l].reshape(1, GLA_WIDTH), final=(l == depth - 1))
    return x
```

```python
import functools

import numpy as np
import jax
import jax.numpy as jnp
from jax import lax
from jax.experimental import pallas as pl
from jax.experimental.pallas import tpu as pltpu

GLA_HEADS = 4
GLA_DK = 64
GLA_DV = 128
GLA_RANK = 16
GLA_TAU = 16.0
GLA_CHUNK = 64
NSA_HEADS = 4
NSA_KV_HEADS = 2
NSA_DH = 128
NSA_BRANCHES = 3
CMP_BLOCK = 32
CMP_STRIDE = 16
CMP_HIDDEN = 256
SEL_BLOCK = 64
SEL_TOPK = 16
WINDOW = 512
N_MOD = 6
ROPE_THETA = 10000.0
EPS = 1e-6
NEG = -1e30
BIG = 1e30

GLA_QK = GLA_HEADS * GLA_DK
GLA_WIDTH = GLA_HEADS * GLA_DV
NSA_WIDTH = NSA_HEADS * NSA_DH
NSA_KV_WIDTH = NSA_KV_HEADS * NSA_DH
N_GATE = NSA_HEADS * NSA_BRANCHES
IN_SPLITS = (GLA_QK, GLA_QK, GLA_WIDTH, GLA_WIDTH, GLA_RANK, NSA_WIDTH) + (NSA_KV_WIDTH,) * 6 + (N_GATE,)

LANES = 128
SUBLANES = 8
MISC_W = LANES
GATE_OFF = GLA_RANK
GLA_TILE = 256
VMEM_LIMIT = 56 * 1024 * 1024

F32 = jnp.float32
BF16 = jnp.bfloat16

_NT = (((1,), (1,)), ((), ()))
_TN = (((0,), (0,)), ((), ()))


def _params(*sem):
    return pltpu.CompilerParams(dimension_semantics=sem, vmem_limit_bytes=VMEM_LIMIT)


def _rms(x):
    return x * lax.rsqrt(jnp.mean(x * x, axis=-1, keepdims=True) + EPS)


def _split_bf16(x):
    hi = x.astype(BF16)
    return hi, (x - hi.astype(F32)).astype(BF16)


def _t_f32(x):
    return x.astype(F32).T


class _Staged:
    def __init__(self, gen):
        self.gen, self.value, self.done = gen, None, False

    def step(self):
        if not self.done:
            try:
                next(self.gen)
            except StopIteration as stop:
                self.value, self.done = stop.value, True

    def finish(self):
        while not self.done:
            self.step()
        return self.value


def _adaln_kernel(c_ref, w_ref, b_ref, o_ref):
    c = c_ref[...]
    n = c.shape[0]
    a_hi, a_lo = _split_bf16(c * jax.nn.sigmoid(c))
    w_hi, w_lo = _split_bf16(w_ref[...])
    y = jnp.dot(jnp.concatenate([a_hi, a_lo], axis=0), w_hi, preferred_element_type=F32)
    o_ref[...] = y[0:n] + y[n:2 * n] + jnp.dot(a_hi, w_lo, preferred_element_type=F32) + b_ref[...]


def _adaln(c, w_ada, b_ada):
    B, D = c.shape
    N = w_ada.shape[1]
    tn = D
    return pl.pallas_call(
        _adaln_kernel,
        grid=(N // tn,),
        in_specs=[pl.BlockSpec((B, D), lambda j: (0, 0)),
                  pl.BlockSpec((D, tn), lambda j: (0, j)),
                  pl.BlockSpec((1, tn), lambda j: (0, j))],
        out_specs=pl.BlockSpec((B, tn), lambda j: (0, j)),
        out_shape=jax.ShapeDtypeStruct((B, N), F32),
        compiler_params=_params("arbitrary"),
        name="adaln",
    )(c, w_ada, b_ada.reshape(1, N))


_PROJ_GROUPS = (("gq", GLA_QK, False), ("gk", GLA_QK, False), ("gv", GLA_WIDTH, False),
                ("gg", GLA_WIDTH, False), ("nq", NSA_WIDTH, True), ("kc", NSA_KV_WIDTH, True),
                ("vc", NSA_KV_WIDTH, False), ("ks", NSA_KV_WIDTH, True), ("vs", NSA_KV_WIDTH, False),
                ("kw", NSA_KV_WIDTH, True), ("vw", NSA_KV_WIDTH, False))
_PROJ_W = sum(g[1] for g in _PROJ_GROUPS) + MISC_W


def _w_in_moves():
    names = ("gq", "gk", "gv", "gg", "gr", "nq", "kc", "vc", "ks", "vs", "kw", "vw", "ngate")
    src = dict(zip(names, np.cumsum((0,) + IN_SPLITS[:-1])))
    width = dict(zip(names, IN_SPLITS))
    moves, dst = [], 0
    for name, w, _ in _PROJ_GROUPS:
        moves.append((int(src[name]), dst, w))
        dst += w
    for name in ("gr", "ngate"):
        moves.append((int(src[name]), dst, width[name]))
        dst += width[name]
    return tuple(moves)


_W_IN_MOVES = _w_in_moves()
_GROUPED = ("kc", "vc")
GRP_W = CMP_STRIDE * NSA_DH


def _in_proj_kernel(x_ref, mod_ref, g_ref, pos_ref, inv_ref, win_ref, *refs):
    n_out = len(_PROJ_GROUPS) + 1
    out_refs, stage_refs, w_ref = refs[:n_out], refs[n_out:-1], refs[-1]

    @pl.when((pl.program_id(0) == 0) & (pl.program_id(1) == 0))
    def _():
        for src, dst, width in _W_IN_MOVES:
            w_ref[:, dst:dst + width] = win_ref[:, src:src + width].astype(BF16)
        pad = _PROJ_W - _W_IN_MOVES[-1][1] - _W_IN_MOVES[-1][2]
        w_ref[:, _PROJ_W - pad:] = jnp.zeros((w_ref.shape[0], pad), BF16)

    tm = x_ref.shape[0]
    x = x_ref[...]
    h = _rms(x) * g_ref[...] * (1.0 + mod_ref[1:2, :]) + mod_ref[0:1, :]
    hb = h.astype(BF16)

    qs = NSA_DH ** -0.5 * float(np.log2(np.e))
    cos = sin = None

    off = 0
    for (name, width, rot), o_ref in zip(_PROJ_GROUPS, out_refs[:-1]):
        if rot and cos is None:
            pos = pos_ref[...].astype(F32)
            pos = jnp.concatenate([jnp.broadcast_to(pos[r:r + 1, :], (LANES, LANES)).T
                                   for r in range(pos.shape[0])], axis=0)
            lane = lax.broadcasted_iota(jnp.int32, (1, LANES), 1)
            low = lane < NSA_DH // 2
            ang = jnp.where(low, pos[0:tm // 2], pos[tm // 2:tm]) * inv_ref[...]
            c2, s2 = jnp.cos(ang), jnp.sin(ang)
            c2r, s2r = pltpu.roll(c2, NSA_DH // 2, axis=1), pltpu.roll(s2, NSA_DH // 2, axis=1)
            cos = jnp.concatenate([jnp.where(low, c2, c2r), jnp.where(low, c2r, c2)], axis=0)
            sin = jnp.concatenate([jnp.where(low, s2, s2r), jnp.where(low, s2r, s2)], axis=0)
            sin = sin * jnp.where(low, -1.0, 1.0)
        y = jnp.dot(hb, w_ref[:, off:off + width], preferred_element_type=F32)
        grouped = name in _GROUPED
        dst = stage_refs[_GROUPED.index(name)] if grouped else o_ref
        for hd in range(width // NSA_DH if (rot or grouped) else 0):
            yh = y[:, hd * NSA_DH:(hd + 1) * NSA_DH]
            if rot:
                c, s = (cos * qs, sin * qs) if name == "nq" else (cos, sin)
                yh = yh * c + pltpu.roll(yh, NSA_DH // 2, axis=1) * s
            if grouped:
                dst[hd] = yh
            else:
                dst[:, hd * NSA_DH:(hd + 1) * NSA_DH] = yh.astype(dst.dtype)
        if not (rot or grouped):
            dst[...] = y.astype(dst.dtype)
        if grouped:
            for k in range(NSA_KV_HEADS):
                for tok in range(CMP_STRIDE):
                    c0 = k * GRP_W + tok * NSA_DH
                    o_ref[:, c0:c0 + NSA_DH] = dst[k, pl.ds(tok, tm // CMP_STRIDE, stride=CMP_STRIDE), :].astype(
                        o_ref.dtype)
        off += width
    out_refs[-1][...] = jnp.dot(hb, w_ref[:, off:off + MISC_W], preferred_element_type=F32)


def _in_proj(x, mod, norm_g, positions, inv, w_in, layer, tm):
    B, S, D = x.shape
    pos4 = positions.reshape(B, S // tm, tm // LANES, LANES)
    row = lambda w: pl.BlockSpec((None, tm, w), lambda b, i: (b, i, 0))
    whole = lambda a: pl.BlockSpec(a.shape, lambda b, i: (0,) * a.ndim)
    out_shape, out_specs = [], []
    for name, w, _ in _PROJ_GROUPS:
        if name in _GROUPED:
            out_shape.append(jax.ShapeDtypeStruct((B, S // CMP_STRIDE, CMP_STRIDE * w), BF16))
            out_specs.append(pl.BlockSpec((None, tm // CMP_STRIDE, CMP_STRIDE * w), lambda b, i: (b, i, 0)))
        else:
            out_shape.append(jax.ShapeDtypeStruct((B, S, w), BF16))
            out_specs.append(row(w))
    out_shape.append(jax.ShapeDtypeStruct((B, S, MISC_W), F32))
    out_specs.append(row(MISC_W))
    return pl.pallas_call(
        _in_proj_kernel,
        grid=(B, S // tm),
        in_specs=[row(D), pl.BlockSpec((None, N_MOD, D), lambda b, i: (b, 0, 0)), whole(norm_g),
                  pl.BlockSpec((None, None, tm // LANES, LANES), lambda b, i: (b, i, 0, 0)), whole(inv),
                  pl.BlockSpec((None,) + w_in.shape[1:], lambda b, i: (layer, 0, 0),
                               pipeline_mode=pl.Buffered(1))],
        out_specs=out_specs,
        out_shape=out_shape,
        scratch_shapes=([pltpu.VMEM((NSA_KV_HEADS, tm, NSA_DH), F32) for _ in _GROUPED]
                        + [pltpu.VMEM((D, _PROJ_W), BF16)]),
        compiler_params=_params("arbitrary", "arbitrary"),
        name="in_proj",
    )(x, mod, norm_g, pos4, inv, w_in)


def _compress_kernel(xk_ref, xv_ref, kw1_ref, kpe_ref, kw2_ref, vw1_ref, vpe_ref, vw2_ref, ok_ref, ov_ref):
    nb, n_grp, _ = xk_ref.shape
    rows = nb * n_grp
    for x_ref, w1_ref, pe_ref, w2_ref, o_ref in ((xk_ref, kw1_ref, kpe_ref, kw2_ref, ok_ref),
                                                 (xv_ref, vw1_ref, vpe_ref, vw2_ref, ov_ref)):
        w1a = w1_ref[0:GRP_W, :]
        w1b = w1_ref[GRP_W:2 * GRP_W, :]
        x = jnp.concatenate([x_ref[:, :, k * GRP_W:(k + 1) * GRP_W].reshape(rows, GRP_W)
                             for k in range(NSA_KV_HEADS)], axis=0)
        bias = (jnp.dot(pe_ref[:, 0:GRP_W], w1a, preferred_element_type=F32)[0:1]
                + jnp.dot(pe_ref[:, GRP_W:2 * GRP_W], w1b, preferred_element_type=F32)[0:1])
        ua = jnp.dot(x, w1a, preferred_element_type=F32)
        ub = jnp.dot(x, w1b, preferred_element_type=F32)
        hid = ua + pltpu.roll(ub, NSA_KV_HEADS * rows - 1, axis=0) + bias
        act = jax.nn.gelu(hid, approximate=True).astype(BF16)
        y = jnp.dot(act, w2_ref[...], preferred_element_type=F32).astype(o_ref.dtype)
        for k in range(NSA_KV_HEADS):
            o_ref[:, :, k * NSA_DH:(k + 1) * NSA_DH] = y[k * rows:(k + 1) * rows].reshape(nb, n_grp, NSA_DH)


def _compress(xk, xv, kw1, kpe, kw2, vw1, vpe, vw2, nb):
    B, n_grp, gw = xk.shape
    W = NSA_KV_WIDTH
    whole = lambda a: pl.BlockSpec(a.shape, lambda b: (0,) * a.ndim)
    xspec = pl.BlockSpec((nb, n_grp, gw), lambda b: (b, 0, 0))
    ospec = pl.BlockSpec((nb, n_grp, W), lambda b: (b, 0, 0))
    return pl.pallas_call(
        _compress_kernel,
        grid=(B // nb,),
        in_specs=[xspec, xspec, whole(kw1), whole(kpe), whole(kw2), whole(vw1), whole(vpe), whole(vw2)],
        out_specs=[ospec, ospec],
        out_shape=[jax.ShapeDtypeStruct((B, n_grp, W), BF16)] * 2,
        compiler_params=_params("parallel"),
        name="compress",
    )(xk, xv, kw1, kpe, kw2, vw1, vpe, vw2)


def _compress_pe(pe):
    return jnp.zeros((SUBLANES, 2 * GRP_W), F32).at[0].set(pe.reshape(2 * GRP_W)).astype(BF16)


V_AUG = NSA_DH + 2 * SUBLANES
SEL_SHIFT = SEL_BLOCK.bit_length() - 1


def _attend(k_rows, qt, vt_aug, masked):
    s = jnp.dot(k_rows(), qt, preferred_element_type=F32)
    if masked is not None:
        s = s + masked()
    yield
    m = jnp.max(s, axis=0, keepdims=True)
    p = jnp.exp2((s - m).astype(BF16))
    yield
    return m, jnp.dot(vt_aug(), p, preferred_element_type=F32)


def _pipelined(gens, depth=1, side=()):
    tasks = [_Staged(g) for g in gens]
    for n in range(len(tasks) + depth):
        for t in tasks[max(0, n - depth):n + 1]:
            t.step()
        for s in side:
            s.step()
    return [t.finish() for t in tasks]


def _merge(parts):
    m_all = parts[0][0]
    for m, _ in parts[1:]:
        m_all = jnp.maximum(m_all, m)
    tot = None
    for m, acc in parts:
        w = acc * jnp.exp2(m - m_all)
        tot = w if tot is None else tot + w
    return tot[0:NSA_DH] / tot[NSA_DH:NSA_DH + 1]


def _nsa_step(c, tq, tk, q_ref, kc_ref, vc_ref, kw_ref, misc_ref, gn_ref, o_ref, kaug_ref, vst_ref, vwt_ref):
    S = kw_ref.shape[0]
    n_cmp = kc_ref.shape[0]
    n_sel = S // SEL_BLOCK
    G = NSA_HEADS // NSA_KV_HEADS
    M = G * tq
    q0 = c * tq

    t_q = q0 + lax.broadcasted_iota(jnp.int32, (1, tq), 1)
    t_m = jnp.concatenate([t_q] * G, axis=1)
    key_off = lax.broadcasted_iota(jnp.int32, (tk, 1), 0)

    n_col = lax.broadcasted_iota(jnp.int32, (n_cmp, 1), 0)
    cmp_end = jnp.where(n_col < n_cmp - 1, n_col * CMP_STRIDE + (CMP_BLOCK - 1), jnp.int32(2 ** 30))
    cmp_valid = cmp_end <= t_m
    any_valid = (t_m >= CMP_BLOCK - 1).astype(F32)
    jj = lax.broadcasted_iota(jnp.int32, (n_sel, n_cmp), 0) * SEL_BLOCK
    nn = lax.broadcasted_iota(jnp.int32, (n_sel, n_cmp), 1) * CMP_STRIDE
    ov_t = jnp.maximum(jnp.minimum(nn + CMP_BLOCK, jj + SEL_BLOCK) - jnp.maximum(nn, jj), 0)
    ov_t = (ov_t.astype(F32) * (1.0 / CMP_BLOCK)).astype(BF16)
    j_row = lax.broadcasted_iota(jnp.int32, (n_sel, tq), 0)
    blk_t = lax.shift_right_logical(q0 + lax.broadcasted_iota(jnp.int32, (n_sel, tq), 1), SEL_SHIFT)
    forced = (j_row == 0) | (j_row == blk_t) | (j_row == blk_t - 1)
    in_past = j_row <= blk_t

    heads = range(NSA_KV_HEADS)
    qts = [jnp.concatenate([_t_f32(q_ref[:, (k * G + g) * NSA_DH:(k * G + g + 1) * NSA_DH])
                            for g in range(G)], axis=1).astype(BF16) for k in heads]

    def select(k):
        hs = slice(k * NSA_DH, (k + 1) * NSA_DH)
        s = jnp.dot(kc_ref[:, hs], qts[k], preferred_element_type=F32)
        s = jnp.where(cmp_valid, s, NEG)
        yield
        e = jnp.exp2(s - jnp.max(s, axis=0, keepdims=True))
        p = e / jnp.sum(e, axis=0, keepdims=True) * any_valid
        yield
        o_cmp = jnp.dot(_t_f32(vc_ref[:, hs]).astype(BF16), p.astype(BF16), preferred_element_type=F32)
        p_grp = p[:, 0:tq]
        for g in range(1, G):
            p_grp = p_grp + p[:, g * tq:(g + 1) * tq]
        p_hi, p_lo = _split_bf16(p_grp)
        imp = (jnp.dot(ov_t, p_hi, preferred_element_type=F32)
               + jnp.dot(ov_t, p_lo, preferred_element_type=F32))
        imp = jnp.where(forced, BIG, jnp.where(in_past, imp, NEG))
        yield
        rank = []
        for r0 in range(0, n_sel, SUBLANES):
            blk = imp[r0:r0 + SUBLANES, :]
            cnt = jnp.zeros(blk.shape, F32)
            for j in range(n_sel):
                row = imp[j:j + 1, :]
                if j < r0:
                    cnt = cnt + jnp.where(row >= blk, 1.0, 0.0)
                elif j >= r0 + SUBLANES - 1:
                    cnt = cnt + jnp.where(row > blk, 1.0, 0.0)
                else:
                    below = r0 + lax.broadcasted_iota(jnp.int32, blk.shape, 0) > j
                    cnt = cnt + jnp.where(below, jnp.where(row >= blk, 1.0, 0.0), jnp.where(row > blk, 1.0, 0.0))
            rank.append(cnt)
            yield
        rank = jnp.concatenate(rank, axis=0)
        bias = jnp.where(rank < float(min(SEL_TOPK, n_sel)), 0.0, NEG).astype(BF16)
        return o_cmp, jnp.concatenate([qts[k], jnp.concatenate([bias] * G, axis=1),
                                       jnp.zeros((NSA_DH - n_sel, M), BF16)], axis=0)

    subs = range(tq // tk)
    col_off = jnp.concatenate([lax.broadcasted_iota(jnp.int32, (1, tk), 1)] * G, axis=1)
    after_bias = jnp.where(key_off > col_off, NEG, 0.0)
    far_bias = jnp.where(key_off <= col_off, NEG, 0.0)
    after = lambda: after_bias
    too_far = lambda: far_bias

    def cols(x, j):
        return jnp.concatenate([x[:, g * tq + j * tk:g * tq + (j + 1) * tk] for g in range(G)], axis=1)

    def sel_tasks(k, j, qta):
        first, diag = (0, q0 // tk - 1) if j is None else (q0 // tk, (q0 + j * tk) // tk)
        for kt in range(first, diag + 1):
            yield _attend(lambda kt=kt: kaug_ref[k, kt * tk:(kt + 1) * tk, :], qta,
                          lambda kt=kt: vst_ref[k, kt], after if kt == diag and j is not None else None)

    def win_tasks(k, j, qt):
        hs = slice(k * NSA_DH, (k + 1) * NSA_DH)
        diag = (q0 + j * tk) // tk
        back = WINDOW // tk
        for kt in range(max(diag - back, 0), diag + 1):
            masked = after if kt == diag else too_far if kt == diag - back else None
            yield _attend(lambda kt=kt: kw_ref[kt * tk:(kt + 1) * tk, hs], qt, lambda kt=kt: vwt_ref[k, kt], masked)

    def run(task_lists, side=()):
        flat = [(key, t) for key, tasks in task_lists.items() for t in tasks]
        parts = {key: [] for key in task_lists}
        for (key, _), part in zip(flat, _pipelined([t for _, t in flat], side=side)):
            parts[key].append(part)
        merged = {}
        for k in heads:
            for j in subs:
                shared = [(cols(m, j), cols(acc, j)) for m, acc in parts.get((k, None), [])]
                merged[k, j] = _merge(shared + parts[k, j])
        return [jnp.concatenate([merged[k, j][:, g * tk:(g + 1) * tk] for g in range(G) for j in subs], axis=1)
                for k in heads]

    selects = [_Staged(select(k)) for k in heads]
    o_win = run({(k, j): list(win_tasks(k, j, cols(qts[k], j))) for k in heads for j in subs}, side=selects)
    o_cmp, qtas = zip(*[s.finish() for s in selects])
    o_sel = run({(k, j): list(sel_tasks(k, j, qtas[k] if j is None else cols(qtas[k], j)))
                 for k in heads for j in (None, *subs)})


    gates_t = _t_f32(jax.nn.sigmoid(misc_ref[...]))
    outs = []
    for k in heads:
        for g in range(G):
            gs = slice(g * tq, (g + 1) * tq)
            gl = GATE_OFF + (k * G + g) * NSA_BRANCHES
            outs.append(gates_t[gl:gl + 1, :] * o_cmp[k][:, gs] + gates_t[gl + 1:gl + 2, :] * o_sel[k][:, gs]
                         + gates_t[gl + 2:gl + 3, :] * o_win[k][:, gs])
    ssq = outs[0] * outs[0]
    for o in outs[1:]:
        ssq = ssq + o * o
    inv = lax.rsqrt(jnp.sum(ssq, axis=0, keepdims=True) * (1.0 / NSA_WIDTH) + EPS)
    for hq, o in enumerate(outs):
        cs = slice(hq * NSA_DH, (hq + 1) * NSA_DH)
        o_ref[:, cs] = ((o * inv).T * gn_ref[:, cs]).astype(o_ref.dtype)


def _nsa_kernel(q_ref, kc_ref, vc_ref, ks_ref, vs_ref, kw_ref, vw_ref, misc_ref, gn_ref, o_ref,
                kaug_ref, vst_ref, vwt_ref):
    tq = q_ref.shape[0]
    tk = vst_ref.shape[3]
    S = ks_ref.shape[0]
    i = pl.program_id(1)

    @pl.when(i == 0)
    def _():
        pos = lax.broadcasted_iota(jnp.int32, (S, LANES), 0)
        lane = lax.broadcasted_iota(jnp.int32, (S, LANES), 1)
        onehot = jnp.where(lax.shift_right_logical(pos, SEL_SHIFT) == lane, 1.0, 0.0).astype(BF16)
        row = lax.broadcasted_iota(jnp.int32, (V_AUG - NSA_DH, tk), 0)
        ones_rows = jnp.where(row == 0, 1.0, 0.0).astype(BF16)
        for k in range(NSA_KV_HEADS):
            hs = slice(k * NSA_DH, (k + 1) * NSA_DH)
            kaug_ref[k, :, 0:NSA_DH] = ks_ref[:, hs]
            kaug_ref[k, :, NSA_DH:2 * NSA_DH] = onehot
            for kt in range(S // tk):
                rows = slice(kt * tk, (kt + 1) * tk)
                for src, dst in ((vs_ref, vst_ref), (vw_ref, vwt_ref)):
                    dst[k, kt, 0:NSA_DH, :] = _t_f32(src[rows, hs]).astype(BF16)
                    dst[k, kt, NSA_DH:V_AUG, :] = ones_rows

    for c in range(S // tq):
        pl.when(i == c)(functools.partial(_nsa_step, c, tq, tk, q_ref, kc_ref, vc_ref, kw_ref, misc_ref,
                                          gn_ref, o_ref, kaug_ref, vst_ref, vwt_ref))


def _nsa(nq, kcmp, vcmp, ks, vs, kw, vw, misc, gn, tq, tk):
    B, S, _ = nq.shape
    n_cmp = kcmp.shape[1]
    assert n_cmp <= LANES and S // SEL_BLOCK <= NSA_DH and SEL_BLOCK == 1 << SEL_SHIFT
    assert tq % tk == 0 and tk <= WINDOW and WINDOW % tk == 0
    row = lambda w: pl.BlockSpec((None, tq, w), lambda b, i: (b, i, 0))
    seq = lambda n: pl.BlockSpec((None, n, NSA_KV_WIDTH), lambda b, i: (b, 0, 0))
    vt_scratch = pltpu.VMEM((NSA_KV_HEADS, S // tk, V_AUG, tk), BF16)
    return pl.pallas_call(
        _nsa_kernel,
        grid=(B, S // tq),
        in_specs=[row(NSA_WIDTH), seq(n_cmp), seq(n_cmp), seq(S), seq(S), seq(S), seq(S), row(MISC_W),
                  pl.BlockSpec(gn.shape, lambda b, i: (0, 0))],
        out_specs=row(NSA_WIDTH),
        out_shape=jax.ShapeDtypeStruct((B, S, NSA_WIDTH), BF16),
        scratch_shapes=[pltpu.VMEM((NSA_KV_HEADS, S, 2 * NSA_DH), BF16), vt_scratch, vt_scratch],
        compiler_params=_params("parallel", "arbitrary"),
        name="nsa",
    )(nq, kcmp, vcmp, ks, vs, kw, vw, misc, gn)


def _log_sigmoid(z):
    return jnp.minimum(z, 0.0) - jnp.log(1.0 + jnp.exp(-jnp.abs(z)))


def _gla_tile(q, k, v, g, misc, wa, ba, gn, st):
    C = GLA_CHUNK
    T = q.shape[0]
    n_chunk = T // C
    r = lax.broadcasted_iota(jnp.int32, (T, T), 0)
    c = lax.broadcasted_iota(jnp.int32, (T, T), 1)
    causal = (r >= c) & (r - c <= (r & (C - 1)))
    tri = jnp.where(causal, 1.0, 0.0).astype(BF16)

    z = jnp.dot(misc.astype(BF16), wa.astype(BF16), preferred_element_type=F32) + ba
    la = _log_sigmoid(z) * (1.0 / GLA_TAU)
    yield
    b = jnp.dot(tri, la.astype(BF16), preferred_element_type=F32)
    b_lasts = [b[(ci + 1) * C - 1:(ci + 1) * C, :] for ci in range(n_chunk)]
    b_last = jnp.concatenate([jnp.broadcast_to(bl, (C, bl.shape[1])) for bl in b_lasts], axis=0)
    qf = q.astype(F32)
    kf = k.astype(F32)
    q_in = (qf * GLA_DK ** -0.5 * jnp.exp(b)).astype(BF16)
    k_in = (kf * jnp.exp(-b)).astype(BF16)
    k_dec = (kf * jnp.exp(b_last - b)).astype(BF16)
    pad = jnp.zeros((LANES - n_chunk, b.shape[1]), F32)
    dec_cols = jnp.exp(jnp.concatenate(b_lasts + [pad], axis=0).T)
    yield

    o_intra = []
    for h in range(GLA_HEADS):
        ks = slice(h * GLA_DK, (h + 1) * GLA_DK)
        att = lax.dot_general(q_in[:, ks], k_in[:, ks], _NT, preferred_element_type=F32)
        att = jnp.where(causal, att, 0.0).astype(BF16)
        o_intra.append(jnp.dot(att, v[:, h * GLA_DV:(h + 1) * GLA_DV], preferred_element_type=F32))
        yield

    st = [st[h] for h in range(GLA_HEADS)]
    o_inter = []
    for ci in range(n_chunk):
        rows = slice(ci * C, (ci + 1) * C)
        out = []
        for h in range(GLA_HEADS):
            ks = slice(h * GLA_DK, (h + 1) * GLA_DK)
            vs = slice(h * GLA_DV, (h + 1) * GLA_DV)
            out.append(jnp.dot(q_in[rows, ks], st[h].astype(BF16), preferred_element_type=F32))
            d_st = lax.dot_general(k_dec[rows, ks], v[rows, vs], _TN, preferred_element_type=F32)
            st[h] = st[h] * dec_cols[ks, ci:ci + 1] + d_st
        o_inter.append(jnp.concatenate(out, axis=1))
        yield
    st = jnp.stack(st)
    o_inter = jnp.concatenate(o_inter, axis=0)
    gate = g.astype(F32)
    gate = gate * jax.nn.sigmoid(gate)
    outs = []
    for h in range(GLA_HEADS):
        vs = slice(h * GLA_DV, (h + 1) * GLA_DV)
        outs.append(_rms(o_intra[h] + o_inter[:, vs]) * gn[:, vs] * gate[:, vs])
    return jnp.concatenate(outs, axis=1), st


def _gla_ffn_kernel(x_ref, on_ref, mod_ref, wo_ref, g2_ref, w1_ref, w2_ref, gf_ref,
                    q0_ref, k0_ref, v0_ref, g0_ref, m0_ref, q1_ref, k1_ref, v1_ref, g1_ref, m1_ref,
                    wa_ref, ba_ref, gn_ref, o_ref, st_ref, og_ref, *, ff_chunk, final):
    def gla(q_ref, k_ref, v_ref, g_ref, m_ref, st):
        return _Staged(_gla_tile(q_ref[...], k_ref[...], v_ref[...], g_ref[...], m_ref[...],
                                 wa_ref[...], ba_ref[...], gn_ref[...], st))

    @pl.when((pl.program_id(0) == 0) & (pl.program_id(1) == 0))
    def _():
        o, st = gla(q0_ref, k0_ref, v0_ref, g0_ref, m0_ref, jnp.zeros(st_ref.shape, F32)).finish()
        og_ref[...] = o.astype(BF16)
        st_ref[...] = st

    last = pl.program_id(1) == pl.num_programs(1) - 1
    ahead = gla(q1_ref, k1_ref, v1_ref, g1_ref, m1_ref, jnp.where(last, 0.0, st_ref[...]))

    ahead.step()
    mix = (jnp.dot(og_ref[...], wo_ref[0:GLA_WIDTH, :], preferred_element_type=F32)
           + jnp.dot(on_ref[...], wo_ref[GLA_WIDTH:, :], preferred_element_type=F32))
    x1 = x_ref[...] + mod_ref[2:3, :] * mix
    h = (_rms(x1) * g2_ref[...] * (1.0 + mod_ref[4:5, :]) + mod_ref[3:4, :]).astype(BF16)
    acc = jnp.zeros_like(x1)
    ahead.step()
    for c in range(w1_ref.shape[1] // ff_chunk):
        cs = slice(c * ff_chunk, (c + 1) * ff_chunk)
        a = jnp.maximum(jnp.dot(h, w1_ref[:, cs], preferred_element_type=F32), 0.0)
        ahead.step()
        acc = acc + jnp.dot((a * a).astype(BF16), w2_ref[cs, :], preferred_element_type=F32)
        ahead.step()
    x2 = x1 + mod_ref[5:6, :] * acc
    o_ref[...] = _rms(x2) * gf_ref[...] if final else x2

    o, st = ahead.finish()
    og_ref[...] = o.astype(BF16)
    st_ref[...] = st


def _gla_ffn(x, on, mod, wo, g2, w1, w2, gf, gq, gk, gv, gg, misc, wa_p, ba, gn, final):
    B, S, D = x.shape
    tm = GLA_TILE
    n = S // tm
    row = lambda w: pl.BlockSpec((None, tm, w), lambda b, i: (b, i, 0))
    first = lambda w: pl.BlockSpec((None, tm, w), lambda b, i: (0, 0, 0))
    ahead = lambda w: pl.BlockSpec((None, tm, w), lambda b, i: (
        jnp.where(i + 1 < n, b, jnp.minimum(b + 1, B - 1)), jnp.where(i + 1 < n, i + 1, 0), 0))
    whole = lambda a: pl.BlockSpec(a.shape, lambda b, i: (0,) * a.ndim)
    gla_in = (gq, gk, gv, gg, misc)
    return pl.pallas_call(
        functools.partial(_gla_ffn_kernel, ff_chunk=1024, final=final),
        grid=(B, n),
        in_specs=([row(D), row(on.shape[2]), pl.BlockSpec((None, N_MOD, D), lambda b, i: (b, 0, 0)),
                   whole(wo), whole(g2), whole(w1), whole(w2), whole(gf)]
                  + [first(a.shape[2]) for a in gla_in] + [ahead(a.shape[2]) for a in gla_in]
                  + [whole(wa_p), whole(ba), whole(gn)]),
        out_specs=row(D),
        out_shape=jax.ShapeDtypeStruct((B, S, D), F32),
        scratch_shapes=[pltpu.VMEM((GLA_HEADS, GLA_DK, GLA_DV), F32), pltpu.VMEM((tm, GLA_WIDTH), BF16)],
        compiler_params=_params("arbitrary", "arbitrary"),
        name="gla_ffn",
    )(x, on, mod, wo, g2, w1, w2, gf, *gla_in, *gla_in, wa_p, ba, gn)


def kernel(x, c, positions, w_ada, b_ada, norm1_g, w_in, gla_w_a2, gla_b_a, gla_norm_g, nsa_pe_k, nsa_pe_v, cmp_k_w1, cmp_k_w2, cmp_v_w1, cmp_v_w2, nsa_norm_g, w_o, norm2_g, w_ff1, w_ff2, final_norm_g):
    B, S, D = x.shape
    depth = w_in.shape[0]
    half = NSA_DH // 2
    inv = ROPE_THETA ** (-jnp.arange(half, dtype=F32) / half)
    inv = jnp.concatenate([inv, inv]).reshape(1, NSA_DH)
    for l in range(depth):
        mod = _adaln(c, w_ada[l], b_ada[l]).reshape(B, N_MOD, D)
        (gq, gk, gv, gg, nq, kc, vc, ks, vs, kw, vw, misc) = _in_proj(
            x, mod, norm1_g[l].reshape(1, D), positions, inv, w_in, l, tm=1024)

        k_cmp, v_cmp = _compress(kc, vc, cmp_k_w1[l].astype(BF16), _compress_pe(nsa_pe_k[l]),
                                 cmp_k_w2[l].astype(BF16), cmp_v_w1[l].astype(BF16),
                                 _compress_pe(nsa_pe_v[l]), cmp_v_w2[l].astype(BF16), nb=int(np.gcd(B, 4)))
        o_nsa = _nsa(nq, k_cmp, v_cmp, ks, vs, kw, vw, misc, nsa_norm_g[l].reshape(1, NSA_WIDTH),
                     tq=1024, tk=128)

        wa_p = jnp.zeros((MISC_W, GLA_QK), F32).at[0:GLA_RANK].set(gla_w_a2[l])
        x = _gla_ffn(x, o_nsa, mod, w_o[l].astype(BF16), norm2_g[l].reshape(1, D),
                     w_ff1[l].astype(BF16), w_ff2[l].astype(BF16), final_norm_g.reshape(1, D),
                     gq, gk, gv, gg, misc, wa_p, gla_b_a[l].reshape(1, GLA_QK),
                     gla_norm_g[l].reshape(1, GLA_WIDTH), final=(l == depth - 1))
    return x
```

```python
import functools

import numpy as np
import jax
import jax.numpy as jnp
from jax import lax
from jax.experimental import pallas as pl
from jax.experimental.pallas import tpu as pltpu

GLA_HEADS = 4
GLA_DK = 64
GLA_DV = 128
GLA_RANK = 16
GLA_TAU = 16.0
GLA_CHUNK = 64
NSA_HEADS = 4
NSA_KV_HEADS = 2
NSA_DH = 128
NSA_BRANCHES = 3
CMP_BLOCK = 32
CMP_STRIDE = 16
CMP_HIDDEN = 256
SEL_BLOCK = 64
SEL_TOPK = 16
WINDOW = 512
N_MOD = 6
ROPE_THETA = 10000.0
EPS = 1e-6
NEG = -1e30
BIG = 1e30

GLA_QK = GLA_HEADS * GLA_DK
GLA_WIDTH = GLA_HEADS * GLA_DV
NSA_WIDTH = NSA_HEADS * NSA_DH
NSA_KV_WIDTH = NSA_KV_HEADS * NSA_DH
N_GATE = NSA_HEADS * NSA_BRANCHES
IN_SPLITS = (GLA_QK, GLA_QK, GLA_WIDTH, GLA_WIDTH, GLA_RANK, NSA_WIDTH) + (NSA_KV_WIDTH,) * 6 + (N_GATE,)

LANES = 128
SUBLANES = 8
MISC_W = LANES
GATE_OFF = GLA_RANK
GLA_TILE = 256
VMEM_LIMIT = 56 * 1024 * 1024

F32 = jnp.float32
BF16 = jnp.bfloat16

_NT = (((1,), (1,)), ((), ()))
_TN = (((0,), (0,)), ((), ()))


def _params(*sem):
    return pltpu.CompilerParams(dimension_semantics=sem, vmem_limit_bytes=VMEM_LIMIT)


def _rms(x):
    return x * lax.rsqrt(jnp.mean(x * x, axis=-1, keepdims=True) + EPS)


def _split_bf16(x):
    hi = x.astype(BF16)
    return hi, (x - hi.astype(F32)).astype(BF16)


def _t_f32(x):
    return x.astype(F32).T


class _Staged:
    def __init__(self, gen):
        self.gen, self.value, self.done = gen, None, False

    def step(self):
        if not self.done:
            try:
                next(self.gen)
            except StopIteration as stop:
                self.value, self.done = stop.value, True

    def finish(self):
        while not self.done:
            self.step()
        return self.value


def _adaln_kernel(c_ref, w_ref, b_ref, o_ref):
    c = c_ref[...]
    n = c.shape[0]
    a_hi, a_lo = _split_bf16(c * jax.nn.sigmoid(c))
    w_hi, w_lo = _split_bf16(w_ref[...])
    y = jnp.dot(jnp.concatenate([a_hi, a_lo], axis=0), w_hi, preferred_element_type=F32)
    o_ref[...] = y[0:n] + y[n:2 * n] + jnp.dot(a_hi, w_lo, preferred_element_type=F32) + b_ref[...]


def _adaln(c, w_ada, b_ada):
    B, D = c.shape
    N = w_ada.shape[1]
    tn = D
    return pl.pallas_call(
        _adaln_kernel,
        grid=(N // tn,),
        in_specs=[pl.BlockSpec((B, D), lambda j: (0, 0)),
                  pl.BlockSpec((D, tn), lambda j: (0, j)),
                  pl.BlockSpec((1, tn), lambda j: (0, j))],
        out_specs=pl.BlockSpec((B, tn), lambda j: (0, j)),
        out_shape=jax.ShapeDtypeStruct((B, N), F32),
        compiler_params=_params("arbitrary"),
        name="adaln",
    )(c, w_ada, b_ada.reshape(1, N))


_PROJ_GROUPS = (("gq", GLA_QK, False), ("gk", GLA_QK, False), ("gv", GLA_WIDTH, False),
                ("gg", GLA_WIDTH, False), ("nq", NSA_WIDTH, True), ("kc", NSA_KV_WIDTH, True),
                ("vc", NSA_KV_WIDTH, False), ("ks", NSA_KV_WIDTH, True), ("vs", NSA_KV_WIDTH, False),
                ("kw", NSA_KV_WIDTH, True), ("vw", NSA_KV_WIDTH, False))
_PROJ_W = sum(g[1] for g in _PROJ_GROUPS) + MISC_W


def _w_in_moves():
    names = ("gq", "gk", "gv", "gg", "gr", "nq", "kc", "vc", "ks", "vs", "kw", "vw", "ngate")
    src = dict(zip(names, np.cumsum((0,) + IN_SPLITS[:-1])))
    width = dict(zip(names, IN_SPLITS))
    moves, dst = [], 0
    for name, w, _ in _PROJ_GROUPS:
        moves.append((int(src[name]), dst, w))
        dst += w
    for name in ("gr", "ngate"):
        moves.append((int(src[name]), dst, width[name]))
        dst += width[name]
    return tuple(moves)


_W_IN_MOVES = _w_in_moves()
_GROUPED = ("kc", "vc")
GRP_W = CMP_STRIDE * NSA_DH


def _in_proj_kernel(x_ref, mod_ref, g_ref, pos_ref, inv_ref, win_ref, *refs):
    n_out = len(_PROJ_GROUPS) + 1
    out_refs, stage_refs, w_ref = refs[:n_out], refs[n_out:-1], refs[-1]

    @pl.when((pl.program_id(0) == 0) & (pl.program_id(1) == 0))
    def _():
        for src, dst, width in _W_IN_MOVES:
            w_ref[:, dst:dst + width] = win_ref[:, src:src + width].astype(BF16)
        pad = _PROJ_W - _W_IN_MOVES[-1][1] - _W_IN_MOVES[-1][2]
        w_ref[:, _PROJ_W - pad:] = jnp.zeros((w_ref.shape[0], pad), BF16)

    tm = x_ref.shape[0]
    x = x_ref[...]
    h = _rms(x) * g_ref[...] * (1.0 + mod_ref[1:2, :]) + mod_ref[0:1, :]
    hb = h.astype(BF16)

    qs = NSA_DH ** -0.5 * float(np.log2(np.e))
    cos = sin = None

    off = 0
    for (name, width, rot), o_ref in zip(_PROJ_GROUPS, out_refs[:-1]):
        if rot and cos is None:
            pos = pos_ref[...].astype(F32)
            pos = jnp.concatenate([jnp.broadcast_to(pos[r:r + 1, :], (LANES, LANES)).T
                                   for r in range(pos.shape[0])], axis=0)
            lane = lax.broadcasted_iota(jnp.int32, (1, LANES), 1)
            low = lane < NSA_DH // 2
            ang = jnp.where(low, pos[0:tm // 2], pos[tm // 2:tm]) * inv_ref[...]
            c2, s2 = jnp.cos(ang), jnp.sin(ang)
            c2r, s2r = pltpu.roll(c2, NSA_DH // 2, axis=1), pltpu.roll(s2, NSA_DH // 2, axis=1)
            cos = jnp.concatenate([jnp.where(low, c2, c2r), jnp.where(low, c2r, c2)], axis=0)
            sin = jnp.concatenate([jnp.where(low, s2, s2r), jnp.where(low, s2r, s2)], axis=0)
            sin = sin * jnp.where(low, -1.0, 1.0)
        y = jnp.dot(hb, w_ref[:, off:off + width], preferred_element_type=F32)
        grouped = name in _GROUPED
        dst = stage_refs[_GROUPED.index(name)] if grouped else o_ref
        for hd in range(width // NSA_DH if (rot or grouped) else 0):
            yh = y[:, hd * NSA_DH:(hd + 1) * NSA_DH]
            if rot:
                c, s = (cos * qs, sin * qs) if name == "nq" else (cos, sin)
                yh = yh * c + pltpu.roll(yh, NSA_DH // 2, axis=1) * s
            if grouped:
                dst[hd] = yh
            else:
                dst[:, hd * NSA_DH:(hd + 1) * NSA_DH] = yh.astype(dst.dtype)
        if not (rot or grouped):
            dst[...] = y.astype(dst.dtype)
        if grouped:
            for k in range(NSA_KV_HEADS):
                for tok in range(CMP_STRIDE):
                    c0 = k * GRP_W + tok * NSA_DH
                    o_ref[:, c0:c0 + NSA_DH] = dst[k, pl.ds(tok, tm // CMP_STRIDE, stride=CMP_STRIDE), :].astype(
                        o_ref.dtype)
        off += width
    out_refs[-1][...] = jnp.dot(hb, w_ref[:, off:off + MISC_W], preferred_element_type=F32)


def _in_proj(x, mod, norm_g, positions, inv, w_in, layer, tm):
    B, S, D = x.shape
    pos4 = positions.reshape(B, S // tm, tm // LANES, LANES)
    row = lambda w: pl.BlockSpec((None, tm, w), lambda b, i: (b, i, 0))
    whole = lambda a: pl.BlockSpec(a.shape, lambda b, i: (0,) * a.ndim)
    out_shape, out_specs = [], []
    for name, w, _ in _PROJ_GROUPS:
        if name in _GROUPED:
            out_shape.append(jax.ShapeDtypeStruct((B, S // CMP_STRIDE, CMP_STRIDE * w), BF16))
            out_specs.append(pl.BlockSpec((None, tm // CMP_STRIDE, CMP_STRIDE * w), lambda b, i: (b, i, 0)))
        else:
            out_shape.append(jax.ShapeDtypeStruct((B, S, w), BF16))
            out_specs.append(row(w))
    out_shape.append(jax.ShapeDtypeStruct((B, S, MISC_W), F32))
    out_specs.append(row(MISC_W))
    return pl.pallas_call(
        _in_proj_kernel,
        grid=(B, S // tm),
        in_specs=[row(D), pl.BlockSpec((None, N_MOD, D), lambda b, i: (b, 0, 0)), whole(norm_g),
                  pl.BlockSpec((None, None, tm // LANES, LANES), lambda b, i: (b, i, 0, 0)), whole(inv),
                  pl.BlockSpec((None,) + w_in.shape[1:], lambda b, i: (layer, 0, 0),
                               pipeline_mode=pl.Buffered(1))],
        out_specs=out_specs,
        out_shape=out_shape,
        scratch_shapes=([pltpu.VMEM((NSA_KV_HEADS, tm, NSA_DH), F32) for _ in _GROUPED]
                        + [pltpu.VMEM((D, _PROJ_W), BF16)]),
        compiler_params=_params("arbitrary", "arbitrary"),
        name="in_proj",
    )(x, mod, norm_g, pos4, inv, w_in)


def _compress_kernel(xk_ref, xv_ref, kw1_ref, kpe_ref, kw2_ref, vw1_ref, vpe_ref, vw2_ref, ok_ref, ov_ref):
    nb, n_grp, _ = xk_ref.shape
    rows = nb * n_grp
    for x_ref, w1_ref, pe_ref, w2_ref, o_ref in ((xk_ref, kw1_ref, kpe_ref, kw2_ref, ok_ref),
                                                 (xv_ref, vw1_ref, vpe_ref, vw2_ref, ov_ref)):
        w1a = w1_ref[0:GRP_W, :].astype(BF16)
        w1b = w1_ref[GRP_W:2 * GRP_W, :].astype(BF16)
        x = jnp.concatenate([x_ref[:, :, k * GRP_W:(k + 1) * GRP_W].reshape(rows, GRP_W)
                             for k in range(NSA_KV_HEADS)], axis=0)
        bias = (jnp.dot(pe_ref[:, 0:GRP_W], w1a, preferred_element_type=F32)[0:1]
                + jnp.dot(pe_ref[:, GRP_W:2 * GRP_W], w1b, preferred_element_type=F32)[0:1])
        ua = jnp.dot(x, w1a, preferred_element_type=F32)
        ub = jnp.dot(x, w1b, preferred_element_type=F32)
        hid = ua + pltpu.roll(ub, NSA_KV_HEADS * rows - 1, axis=0) + bias
        act = jax.nn.gelu(hid, approximate=True).astype(BF16)
        y = jnp.dot(act, w2_ref[...].astype(BF16), preferred_element_type=F32).astype(o_ref.dtype)
        for k in range(NSA_KV_HEADS):
            o_ref[:, :, k * NSA_DH:(k + 1) * NSA_DH] = y[k * rows:(k + 1) * rows].reshape(nb, n_grp, NSA_DH)


def _compress(xk, xv, kw1, kpe, kw2, vw1, vpe, vw2, nb):
    B, n_grp, gw = xk.shape
    W = NSA_KV_WIDTH
    whole = lambda a: pl.BlockSpec(a.shape, lambda b: (0,) * a.ndim, pipeline_mode=pl.Buffered(1))
    xspec = pl.BlockSpec((nb, n_grp, gw), lambda b: (b, 0, 0))
    ospec = pl.BlockSpec((nb, n_grp, W), lambda b: (b, 0, 0))
    return pl.pallas_call(
        _compress_kernel,
        grid=(B // nb,),
        in_specs=[xspec, xspec, whole(kw1), whole(kpe), whole(kw2), whole(vw1), whole(vpe), whole(vw2)],
        out_specs=[ospec, ospec],
        out_shape=[jax.ShapeDtypeStruct((B, n_grp, W), BF16)] * 2,
        compiler_params=_params("parallel"),
        name="compress",
    )(xk, xv, kw1, kpe, kw2, vw1, vpe, vw2)


def _compress_pe(pe):
    return jnp.zeros((SUBLANES, 2 * GRP_W), F32).at[0].set(pe.reshape(2 * GRP_W)).astype(BF16)


V_AUG = NSA_DH + 2 * SUBLANES
SEL_SHIFT = SEL_BLOCK.bit_length() - 1


def _attend(k_rows, qt, vt_aug, masked):
    s = jnp.dot(k_rows(), qt, preferred_element_type=F32)
    if masked is not None:
        s = s + masked()
    yield
    m = jnp.max(s, axis=0, keepdims=True)
    p = jnp.exp2((s - m).astype(BF16))
    yield
    return m, jnp.dot(vt_aug(), p, preferred_element_type=F32)


def _pipelined(gens, depth=1, side=()):
    tasks = [_Staged(g) for g in gens]
    for n in range(len(tasks) + depth):
        for t in tasks[max(0, n - depth):n + 1]:
            t.step()
        for s in side:
            s.step()
    return [t.finish() for t in tasks]


def _merge(parts):
    m_all = parts[0][0]
    for m, _ in parts[1:]:
        m_all = jnp.maximum(m_all, m)
    tot = None
    for m, acc in parts:
        w = acc * jnp.exp2(m - m_all)
        tot = w if tot is None else tot + w
    return tot[0:NSA_DH] / tot[NSA_DH:NSA_DH + 1]


def _nsa_step(c, tq, tk, q_ref, kc_ref, vc_ref, kw_ref, misc_ref, gn_ref, o_ref, kaug_ref, vst_ref, vwt_ref):
    S = kw_ref.shape[0]
    n_cmp = kc_ref.shape[0]
    n_sel = S // SEL_BLOCK
    G = NSA_HEADS // NSA_KV_HEADS
    M = G * tq
    q0 = c * tq

    t_q = q0 + lax.broadcasted_iota(jnp.int32, (1, tq), 1)
    t_m = jnp.concatenate([t_q] * G, axis=1)
    key_off = lax.broadcasted_iota(jnp.int32, (tk, 1), 0)

    n_col = lax.broadcasted_iota(jnp.int32, (n_cmp, 1), 0)
    cmp_end = jnp.where(n_col < n_cmp - 1, n_col * CMP_STRIDE + (CMP_BLOCK - 1), jnp.int32(2 ** 30))
    cmp_valid = cmp_end <= t_m
    any_valid = (t_m >= CMP_BLOCK - 1).astype(F32)
    jj = lax.broadcasted_iota(jnp.int32, (n_sel, n_cmp), 0) * SEL_BLOCK
    nn = lax.broadcasted_iota(jnp.int32, (n_sel, n_cmp), 1) * CMP_STRIDE
    ov_t = jnp.maximum(jnp.minimum(nn + CMP_BLOCK, jj + SEL_BLOCK) - jnp.maximum(nn, jj), 0)
    ov_t = (ov_t.astype(F32) * (1.0 / CMP_BLOCK)).astype(BF16)
    j_row = lax.broadcasted_iota(jnp.int32, (n_sel, tq), 0)
    blk_t = lax.shift_right_logical(q0 + lax.broadcasted_iota(jnp.int32, (n_sel, tq), 1), SEL_SHIFT)
    forced = (j_row == 0) | (j_row == blk_t) | (j_row == blk_t - 1)
    in_past = j_row <= blk_t

    heads = range(NSA_KV_HEADS)
    qts = [jnp.concatenate([_t_f32(q_ref[:, (k * G + g) * NSA_DH:(k * G + g + 1) * NSA_DH])
                            for g in range(G)], axis=1).astype(BF16) for k in heads]

    def select(k):
        hs = slice(k * NSA_DH, (k + 1) * NSA_DH)
        s = jnp.dot(kc_ref[:, hs], qts[k], preferred_element_type=F32)
        s = jnp.where(cmp_valid, s, NEG)
        yield
        e = jnp.exp2(s - jnp.max(s, axis=0, keepdims=True))
        p = e / jnp.sum(e, axis=0, keepdims=True) * any_valid
        yield
        o_cmp = jnp.dot(_t_f32(vc_ref[:, hs]).astype(BF16), p.astype(BF16), preferred_element_type=F32)
        p_grp = p[:, 0:tq]
        for g in range(1, G):
            p_grp = p_grp + p[:, g * tq:(g + 1) * tq]
        p_hi, p_lo = _split_bf16(p_grp)
        imp = (jnp.dot(ov_t, p_hi, preferred_element_type=F32)
               + jnp.dot(ov_t, p_lo, preferred_element_type=F32))
        imp = jnp.where(forced, BIG, jnp.where(in_past, imp, NEG))
        yield
        rank = []
        for r0 in range(0, n_sel, SUBLANES):
            blk = imp[r0:r0 + SUBLANES, :]
            cnt = jnp.zeros(blk.shape, F32)
            for j in range(n_sel):
                row = imp[j:j + 1, :]
                if j < r0:
                    cnt = cnt + jnp.where(row >= blk, 1.0, 0.0)
                elif j >= r0 + SUBLANES - 1:
                    cnt = cnt + jnp.where(row > blk, 1.0, 0.0)
                else:
                    below = r0 + lax.broadcasted_iota(jnp.int32, blk.shape, 0) > j
                    cnt = cnt + jnp.where(below, jnp.where(row >= blk, 1.0, 0.0), jnp.where(row > blk, 1.0, 0.0))
            rank.append(cnt)
            yield
        rank = jnp.concatenate(rank, axis=0)
        bias = jnp.where(rank < float(min(SEL_TOPK, n_sel)), 0.0, NEG).astype(BF16)
        return o_cmp, jnp.concatenate([qts[k], jnp.concatenate([bias] * G, axis=1),
                                       jnp.zeros((NSA_DH - n_sel, M), BF16)], axis=0)

    subs = range(tq // tk)
    col_off = jnp.concatenate([lax.broadcasted_iota(jnp.int32, (1, tk), 1)] * G, axis=1)
    after_bias = jnp.where(key_off > col_off, NEG, 0.0)
    far_bias = jnp.where(key_off <= col_off, NEG, 0.0)
    after = lambda: after_bias
    too_far = lambda: far_bias

    def cols(x, j):
        return jnp.concatenate([x[:, g * tq + j * tk:g * tq + (j + 1) * tk] for g in range(G)], axis=1)

    def sel_tasks(k, j, qta):
        first, diag = (0, q0 // tk - 1) if j is None else (q0 // tk, (q0 + j * tk) // tk)
        for kt in range(first, diag + 1):
            yield _attend(lambda kt=kt: kaug_ref[k, kt * tk:(kt + 1) * tk, :], qta,
                          lambda kt=kt: vst_ref[k, kt], after if kt == diag and j is not None else None)

    def win_tasks(k, j, qt):
        hs = slice(k * NSA_DH, (k + 1) * NSA_DH)
        diag = (q0 + j * tk) // tk
        back = WINDOW // tk
        for kt in range(max(diag - back, 0), diag + 1):
            masked = after if kt == diag else too_far if kt == diag - back else None
            yield _attend(lambda kt=kt: kw_ref[kt * tk:(kt + 1) * tk, hs], qt, lambda kt=kt: vwt_ref[k, kt], masked)

    def run(task_lists, side=()):
        flat = [(key, t) for key, tasks in task_lists.items() for t in tasks]
        parts = {key: [] for key in task_lists}
        for (key, _), part in zip(flat, _pipelined([t for _, t in flat], side=side)):
            parts[key].append(part)
        merged = {}
        for k in heads:
            for j in subs:
                shared = [(cols(m, j), cols(acc, j)) for m, acc in parts.get((k, None), [])]
                merged[k, j] = _merge(shared + parts[k, j])
        return [jnp.concatenate([merged[k, j][:, g * tk:(g + 1) * tk] for g in range(G) for j in subs], axis=1)
                for k in heads]

    selects = [_Staged(select(k)) for k in heads]
    o_win = run({(k, j): list(win_tasks(k, j, cols(qts[k], j))) for k in heads for j in subs}, side=selects)
    o_cmp, qtas = zip(*[s.finish() for s in selects])
    o_sel = run({(k, j): list(sel_tasks(k, j, qtas[k] if j is None else cols(qtas[k], j)))
                 for k in heads for j in (None, *subs)})


    gates_t = _t_f32(jax.nn.sigmoid(misc_ref[...]))
    outs = []
    for k in heads:
        for g in range(G):
            gs = slice(g * tq, (g + 1) * tq)
            gl = GATE_OFF + (k * G + g) * NSA_BRANCHES
            outs.append(gates_t[gl:gl + 1, :] * o_cmp[k][:, gs] + gates_t[gl + 1:gl + 2, :] * o_sel[k][:, gs]
                         + gates_t[gl + 2:gl + 3, :] * o_win[k][:, gs])
    ssq = outs[0] * outs[0]
    for o in outs[1:]:
        ssq = ssq + o * o
    inv = lax.rsqrt(jnp.sum(ssq, axis=0, keepdims=True) * (1.0 / NSA_WIDTH) + EPS)
    for hq, o in enumerate(outs):
        cs = slice(hq * NSA_DH, (hq + 1) * NSA_DH)
        o_ref[:, cs] = ((o * inv).T * gn_ref[:, cs]).astype(o_ref.dtype)


def _nsa_kernel(q_ref, kc_ref, vc_ref, ks_ref, vs_ref, kw_ref, vw_ref, misc_ref, gn_ref, o_ref,
                kaug_ref, vst_ref, vwt_ref):
    tq = q_ref.shape[0]
    tk = vst_ref.shape[3]
    S = ks_ref.shape[0]
    i = pl.program_id(1)

    @pl.when(i == 0)
    def _():
        pos = lax.broadcasted_iota(jnp.int32, (S, LANES), 0)
        lane = lax.broadcasted_iota(jnp.int32, (S, LANES), 1)
        onehot = jnp.where(lax.shift_right_logical(pos, SEL_SHIFT) == lane, 1.0, 0.0).astype(BF16)
        row = lax.broadcasted_iota(jnp.int32, (V_AUG - NSA_DH, tk), 0)
        ones_rows = jnp.where(row == 0, 1.0, 0.0).astype(BF16)
        for k in range(NSA_KV_HEADS):
            hs = slice(k * NSA_DH, (k + 1) * NSA_DH)
            kaug_ref[k, :, 0:NSA_DH] = ks_ref[:, hs]
            kaug_ref[k, :, NSA_DH:2 * NSA_DH] = onehot
            for kt in range(S // tk):
                rows = slice(kt * tk, (kt + 1) * tk)
                for src, dst in ((vs_ref, vst_ref), (vw_ref, vwt_ref)):
                    dst[k, kt, 0:NSA_DH, :] = _t_f32(src[rows, hs]).astype(BF16)
                    dst[k, kt, NSA_DH:V_AUG, :] = ones_rows

    for c in range(S // tq):
        pl.when(i == c)(functools.partial(_nsa_step, c, tq, tk, q_ref, kc_ref, vc_ref, kw_ref, misc_ref,
                                          gn_ref, o_ref, kaug_ref, vst_ref, vwt_ref))


def _nsa(nq, kcmp, vcmp, ks, vs, kw, vw, misc, gn, tq, tk):
    B, S, _ = nq.shape
    n_cmp = kcmp.shape[1]
    assert n_cmp <= LANES and S // SEL_BLOCK <= NSA_DH and SEL_BLOCK == 1 << SEL_SHIFT
    assert tq % tk == 0 and tk <= WINDOW and WINDOW % tk == 0
    row = lambda w: pl.BlockSpec((None, tq, w), lambda b, i: (b, i, 0))
    seq = lambda n: pl.BlockSpec((None, n, NSA_KV_WIDTH), lambda b, i: (b, 0, 0))
    vt_scratch = pltpu.VMEM((NSA_KV_HEADS, S // tk, V_AUG, tk), BF16)
    return pl.pallas_call(
        _nsa_kernel,
        grid=(B, S // tq),
        in_specs=[row(NSA_WIDTH), seq(n_cmp), seq(n_cmp), seq(S), seq(S), seq(S), seq(S), row(MISC_W),
                  pl.BlockSpec(gn.shape, lambda b, i: (0, 0))],
        out_specs=row(NSA_WIDTH),
        out_shape=jax.ShapeDtypeStruct((B, S, NSA_WIDTH), BF16),
        scratch_shapes=[pltpu.VMEM((NSA_KV_HEADS, S, 2 * NSA_DH), BF16), vt_scratch, vt_scratch],
        compiler_params=_params("parallel", "arbitrary"),
        name="nsa",
    )(nq, kcmp, vcmp, ks, vs, kw, vw, misc, gn)


def _log_sigmoid(z):
    return jnp.minimum(z, 0.0) - jnp.log(1.0 + jnp.exp(-jnp.abs(z)))


def _gla_tile(q, k, v, g, misc, wa, ba, gn, st):
    C = GLA_CHUNK
    T = q.shape[0]
    n_chunk = T // C
    r = lax.broadcasted_iota(jnp.int32, (T, T), 0)
    c = lax.broadcasted_iota(jnp.int32, (T, T), 1)
    causal = (r >= c) & (r - c <= (r & (C - 1)))
    tri = jnp.where(causal, 1.0, 0.0).astype(BF16)

    z = jnp.dot(misc.astype(BF16), wa.astype(BF16), preferred_element_type=F32) + ba
    la = _log_sigmoid(z) * (1.0 / GLA_TAU)
    yield
    b = jnp.dot(tri, la.astype(BF16), preferred_element_type=F32)
    b_lasts = [b[(ci + 1) * C - 1:(ci + 1) * C, :] for ci in range(n_chunk)]
    b_last = jnp.concatenate([jnp.broadcast_to(bl, (C, bl.shape[1])) for bl in b_lasts], axis=0)
    qf = q.astype(F32)
    kf = k.astype(F32)
    q_in = (qf * GLA_DK ** -0.5 * jnp.exp(b)).astype(BF16)
    k_in = (kf * jnp.exp(-b)).astype(BF16)
    k_dec = (kf * jnp.exp(b_last - b)).astype(BF16)
    pad = jnp.zeros((LANES - n_chunk, b.shape[1]), F32)
    dec_cols = jnp.exp(jnp.concatenate(b_lasts + [pad], axis=0).T)
    yield

    o_intra = []
    for h in range(GLA_HEADS):
        ks = slice(h * GLA_DK, (h + 1) * GLA_DK)
        att = lax.dot_general(q_in[:, ks], k_in[:, ks], _NT, preferred_element_type=F32)
        att = jnp.where(causal, att, 0.0).astype(BF16)
        o_intra.append(jnp.dot(att, v[:, h * GLA_DV:(h + 1) * GLA_DV], preferred_element_type=F32))
        yield

    st = [st[h] for h in range(GLA_HEADS)]
    o_inter = []
    for ci in range(n_chunk):
        rows = slice(ci * C, (ci + 1) * C)
        out = []
        for h in range(GLA_HEADS):
            ks = slice(h * GLA_DK, (h + 1) * GLA_DK)
            vs = slice(h * GLA_DV, (h + 1) * GLA_DV)
            out.append(jnp.dot(q_in[rows, ks], st[h].astype(BF16), preferred_element_type=F32))
            d_st = lax.dot_general(k_dec[rows, ks], v[rows, vs], _TN, preferred_element_type=F32)
            st[h] = st[h] * dec_cols[ks, ci:ci + 1] + d_st
        o_inter.append(jnp.concatenate(out, axis=1))
        yield
    st = jnp.stack(st)
    o_inter = jnp.concatenate(o_inter, axis=0)
    gate = g.astype(F32)
    gate = gate * jax.nn.sigmoid(gate)
    outs = []
    for h in range(GLA_HEADS):
        vs = slice(h * GLA_DV, (h + 1) * GLA_DV)
        outs.append(_rms(o_intra[h] + o_inter[:, vs]) * gn[:, vs] * gate[:, vs])
    return jnp.concatenate(outs, axis=1), st


def _gla_ffn_kernel(x_ref, on_ref, mod_ref, wo_ref, g2_ref, w1_ref, w2_ref, gf_ref,
                    q0_ref, k0_ref, v0_ref, g0_ref, m0_ref, q1_ref, k1_ref, v1_ref, g1_ref, m1_ref,
                    wa_ref, ba_ref, gn_ref, o_ref, st_ref, og_ref, *, ff_chunk, final):
    def gla(q_ref, k_ref, v_ref, g_ref, m_ref, st):
        return _Staged(_gla_tile(q_ref[...], k_ref[...], v_ref[...], g_ref[...], m_ref[...],
                                 wa_ref[...], ba_ref[...], gn_ref[...], st))

    @pl.when((pl.program_id(0) == 0) & (pl.program_id(1) == 0))
    def _():
        o, st = gla(q0_ref, k0_ref, v0_ref, g0_ref, m0_ref, jnp.zeros(st_ref.shape, F32)).finish()
        og_ref[...] = o.astype(BF16)
        st_ref[...] = st

    last = pl.program_id(1) == pl.num_programs(1) - 1
    ahead = gla(q1_ref, k1_ref, v1_ref, g1_ref, m1_ref, jnp.where(last, 0.0, st_ref[...]))

    ahead.step()
    mix = (jnp.dot(og_ref[...], wo_ref[0:GLA_WIDTH, :], preferred_element_type=F32)
           + jnp.dot(on_ref[...], wo_ref[GLA_WIDTH:, :], preferred_element_type=F32))
    x1 = x_ref[...] + mod_ref[2:3, :] * mix
    h = (_rms(x1) * g2_ref[...] * (1.0 + mod_ref[4:5, :]) + mod_ref[3:4, :]).astype(BF16)
    acc = jnp.zeros_like(x1)
    ahead.step()
    for c in range(w1_ref.shape[1] // ff_chunk):
        cs = slice(c * ff_chunk, (c + 1) * ff_chunk)
        a = jnp.maximum(jnp.dot(h, w1_ref[:, cs], preferred_element_type=F32), 0.0)
        ahead.step()
        acc = acc + jnp.dot((a * a).astype(BF16), w2_ref[cs, :], preferred_element_type=F32)
        ahead.step()
    x2 = x1 + mod_ref[5:6, :] * acc
    o_ref[...] = _rms(x2) * gf_ref[...] if final else x2

    o, st = ahead.finish()
    og_ref[...] = o.astype(BF16)
    st_ref[...] = st


def _gla_ffn(x, on, mod, wo, g2, w1, w2, gf, gq, gk, gv, gg, misc, wa_p, ba, gn, final):
    B, S, D = x.shape
    tm = GLA_TILE
    n = S // tm
    row = lambda w: pl.BlockSpec((None, tm, w), lambda b, i: (b, i, 0))
    first = lambda w: pl.BlockSpec((None, tm, w), lambda b, i: (0, 0, 0))
    ahead = lambda w: pl.BlockSpec((None, tm, w), lambda b, i: (
        jnp.where(i + 1 < n, b, jnp.minimum(b + 1, B - 1)), jnp.where(i + 1 < n, i + 1, 0), 0))
    whole = lambda a: pl.BlockSpec(a.shape, lambda b, i: (0,) * a.ndim)
    gla_in = (gq, gk, gv, gg, misc)
    return pl.pallas_call(
        functools.partial(_gla_ffn_kernel, ff_chunk=1024, final=final),
        grid=(B, n),
        in_specs=([row(D), row(on.shape[2]), pl.BlockSpec((None, N_MOD, D), lambda b, i: (b, 0, 0)),
                   whole(wo), whole(g2), whole(w1), whole(w2), whole(gf)]
                  + [first(a.shape[2]) for a in gla_in] + [ahead(a.shape[2]) for a in gla_in]
                  + [whole(wa_p), whole(ba), whole(gn)]),
        out_specs=row(D),
        out_shape=jax.ShapeDtypeStruct((B, S, D), F32),
        scratch_shapes=[pltpu.VMEM((GLA_HEADS, GLA_DK, GLA_DV), F32), pltpu.VMEM((tm, GLA_WIDTH), BF16)],
        compiler_params=_params("arbitrary", "arbitrary"),
        name="gla_ffn",
    )(x, on, mod, wo, g2, w1, w2, gf, *gla_in, *gla_in, wa_p, ba, gn)


def kernel(x, c, positions, w_ada, b_ada, norm1_g, w_in, gla_w_a2, gla_b_a, gla_norm_g, nsa_pe_k, nsa_pe_v, cmp_k_w1, cmp_k_w2, cmp_v_w1, cmp_v_w2, nsa_norm_g, w_o, norm2_g, w_ff1, w_ff2, final_norm_g):
    B, S, D = x.shape
    depth = w_in.shape[0]
    half = NSA_DH // 2
    inv = ROPE_THETA ** (-jnp.arange(half, dtype=F32) / half)
    inv = jnp.concatenate([inv, inv]).reshape(1, NSA_DH)
    for l in range(depth):
        mod = _adaln(c, w_ada[l], b_ada[l]).reshape(B, N_MOD, D)
        (gq, gk, gv, gg, nq, kc, vc, ks, vs, kw, vw, misc) = _in_proj(
            x, mod, norm1_g[l].reshape(1, D), positions, inv, w_in, l, tm=1024)

        k_cmp, v_cmp = _compress(kc, vc, cmp_k_w1[l], _compress_pe(nsa_pe_k[l]), cmp_k_w2[l], cmp_v_w1[l],
                                 _compress_pe(nsa_pe_v[l]), cmp_v_w2[l], nb=int(np.gcd(B, 4)))
        o_nsa = _nsa(nq, k_cmp, v_cmp, ks, vs, kw, vw, misc, nsa_norm_g[l].reshape(1, NSA_WIDTH),
                     tq=1024, tk=128)

        wa_p = jnp.zeros((MISC_W, GLA_QK), F32).at[0:GLA_RANK].set(gla_w_a2[l])
        x = _gla_ffn(x, o_nsa, mod, w_o[l].astype(BF16), norm2_g[l].reshape(1, D),
                     w_ff1[l].astype(BF16), w_ff2[l].astype(BF16), final_norm_g.reshape(1, D),
                     gq, gk, gv, gg, misc, wa_p, gla_b_a[l].reshape(1, GLA_QK),
                     gla_norm_g[l].reshape(1, GLA_WIDTH), final=(l == depth - 1))
    return x
```
